```python
import jax, jax.numpy as jnp
from jax import lax
import numpy as np

D_MODEL = 1024
BATCH = 4
SEQ = 8192
DEPTH = 1

SSD_D_INNER = 2 * D_MODEL
SSD_HEAD_DIM = 64
SSD_N_HEADS = SSD_D_INNER // SSD_HEAD_DIM
SSD_N_GROUPS = 8
SSD_HPG = SSD_N_HEADS // SSD_N_GROUPS
SSD_D_STATE = 128
SSD_CONV = 4
SSD_CHUNK = 128
SSD_XBC = SSD_D_INNER + 2 * SSD_N_GROUPS * SSD_D_STATE
GMLP_D = D_MODEL
GMLP_GROUPS = 8
GMLP_GROUP_DIM = GMLP_D // GMLP_GROUPS
GMLP_CHUNK = 128
IN_SPLITS = (SSD_D_INNER,
             SSD_D_INNER + SSD_XBC,
             SSD_D_INNER + SSD_XBC + SSD_N_HEADS,
             SSD_D_INNER + SSD_XBC + SSD_N_HEADS + 2 * GMLP_D,
             SSD_D_INNER + SSD_XBC + SSD_N_HEADS + 2 * GMLP_D + D_MODEL)
IN_COLS = IN_SPLITS[-1] + D_MODEL
N_EXPERTS = 256
TOP_K = 8
N_EXPERT_GROUPS = 8
TOPK_GROUPS = 4
EXPERT_FF = 256
SHARED_FF = 256
ROUTED_SCALE = 2.5
MOE_BLOCK = 128
DEEPNORM_ALPHA = (2 * DEPTH) ** 0.25
DEEPNORM_BETA = (8 * DEPTH) ** -0.25
LN_EPS = 1e-5
RMS_EPS = 1e-5

kernel_name = 'hybrid_ssd_gmlp_moe_deepnorm_adaln'


def layer_norm(x):
    xf = x.astype(jnp.float32)
    xc = xf - jnp.mean(xf, -1, keepdims=True)
    var = jnp.mean(xc * xc, -1, keepdims=True)
    return (xc * lax.rsqrt(var + LN_EPS)).astype(x.dtype)


def modulate(h, shift, scale):
    return layer_norm(h) * (1.0 + scale[:, None, :]) + shift[:, None, :]


def causal_dwconv(x, w, b):
    out = lax.conv_general_dilated(x, w[:, None, :], window_strides=(1,),
                                   padding=[(SSD_CONV - 1, 0)],
                                   dimension_numbers=('NWC', 'WIO', 'NWC'),
                                   feature_group_count=x.shape[-1])
    return out + b


def ssd_chunked(xh, dt, a, bm, cm):
    bsz, s = xh.shape[:2]
    nc = s // SSD_CHUNK
    xh = xh.reshape(bsz, nc, SSD_CHUNK, SSD_N_GROUPS, SSD_HPG, SSD_HEAD_DIM)
    dt = dt.reshape(bsz, nc, SSD_CHUNK, SSD_N_GROUPS, SSD_HPG)
    bm = bm.reshape(bsz, nc, SSD_CHUNK, SSD_N_GROUPS, SSD_D_STATE)
    cm = cm.reshape(bsz, nc, SSD_CHUNK, SSD_N_GROUPS, SSD_D_STATE)
    a_cs = jnp.cumsum(dt * a, axis=2)
    causal = jnp.tril(jnp.ones((SSD_CHUNK, SSD_CHUNK), bool))
    seg = a_cs[:, :, :, None] - a_cs[:, :, None, :]
    decay = jnp.exp(jnp.where(causal[:, :, None, None], seg, -jnp.inf))
    cb = jnp.einsum('bclgn,bcsgn->bclsg', cm, bm)
    w = cb[..., None] * decay * dt[:, :, None]
    y_diag = jnp.einsum('bclsgr,bcsgrp->bclgrp', w, xh)
    to_end = jnp.exp(a_cs[:, :, -1:] - a_cs) * dt
    states = jnp.einsum('bclgn,bclgrp->bcgrpn', bm, xh * to_end[..., None])
    chunk_decay = jnp.exp(a_cs[:, :, -1])

    def step(h, inp):
        st, dec = inp
        return dec[..., None, None] * h + st, h

    h0 = jnp.zeros((bsz, SSD_N_GROUPS, SSD_HPG, SSD_HEAD_DIM, SSD_D_STATE), jnp.float32)
    _, prev = lax.scan(step, h0, (jnp.moveaxis(states, 1, 0), jnp.moveaxis(chunk_decay, 1, 0)))
    prev = jnp.moveaxis(prev, 0, 1)
    y_off = jnp.einsum('bclgn,bcgrpn->bclgrp', cm, prev) * jnp.exp(a_cs)[..., None]
    return (y_diag + y_off).reshape(bsz, s, SSD_N_GROUPS, SSD_HPG, SSD_HEAD_DIM)


def ssd_branch(z, xbc, dt_raw, conv_w, conv_b, dt_bias, a_log, d_skip, norm_w):
    bsz, s, _ = z.shape
    gn = SSD_N_GROUPS * SSD_D_STATE
    xbc = jax.nn.silu(causal_dwconv(xbc, conv_w, conv_b)).astype(jnp.float32)
    xs = xbc[..., :SSD_D_INNER].reshape(bsz, s, SSD_N_GROUPS, SSD_HPG, SSD_HEAD_DIM)
    bm = xbc[..., SSD_D_INNER:SSD_D_INNER + gn].reshape(bsz, s, SSD_N_GROUPS, SSD_D_STATE)
    cm = xbc[..., SSD_D_INNER + gn:].reshape(bsz, s, SSD_N_GROUPS, SSD_D_STATE)
    dt = jax.nn.softplus((dt_raw + dt_bias).astype(jnp.float32)).reshape(bsz, s, SSD_N_GROUPS, SSD_HPG)
    a = -jnp.exp(a_log.astype(jnp.float32)).reshape(SSD_N_GROUPS, SSD_HPG)
    y = ssd_chunked(xs, dt, a, bm, cm)
    y = y + d_skip.astype(jnp.float32).reshape(SSD_N_GROUPS, SSD_HPG)[:, :, None] * xs
    y = y.reshape(bsz, s, SSD_D_INNER) * jax.nn.silu(z.astype(jnp.float32))
    y = y * lax.rsqrt(jnp.mean(y * y, -1, keepdims=True) + RMS_EPS)
    return (y * norm_w).astype(z.dtype)


def gmlp_branch(uv, ln_g, ln_b, ws, bs):
    bsz, s, _ = uv.shape
    uv = jax.nn.gelu(uv)
    u, v = uv[..., :GMLP_D], uv[..., GMLP_D:]
    v = layer_norm(v) * ln_g + ln_b
    v = v.reshape(bsz, s // GMLP_CHUNK, GMLP_CHUNK, GMLP_GROUPS, GMLP_GROUP_DIM)
    mask = jnp.tril(jnp.ones((GMLP_CHUNK, GMLP_CHUNK), ws.dtype))
    v = jnp.einsum('gts,bcsgd->bctgd', ws * mask, v) + bs.T[None, None, :, :, None]
    return u * v.reshape(bsz, s, GMLP_D)


def token_mixer(m, w_in, conv_w, conv_b, dt_bias, a_log, d_skip, ssd_norm_w,
                gmlp_ln_g, gmlp_ln_b, gmlp_ws, gmlp_bs, w_proj_ssd, w_proj_gmlp, w_out):
    proj = m @ w_in
    z, xbc, dt_raw, uv, gate_a, gate_b = jnp.split(proj, IN_SPLITS, axis=-1)
    y_a = ssd_branch(z, xbc, dt_raw, conv_w, conv_b, dt_bias, a_log, d_skip, ssd_norm_w) @ w_proj_ssd
    y_b = gmlp_branch(uv, gmlp_ln_g, gmlp_ln_b, gmlp_ws, gmlp_bs) @ w_proj_gmlp
    merged = jax.nn.sigmoid(gate_a) * y_a + jax.nn.sigmoid(gate_b) * y_b
    return merged @ w_out


def swiglu(x, wg, wu, wd):
    return (jax.nn.silu(x @ wg) * (x @ wu)) @ wd


def routed_experts(xs, top_idx, top_w, w_gate, w_up, w_down):
    t, d = xs.shape
    tk = t * TOP_K
    nb = -(-tk // MOE_BLOCK) + N_EXPERTS
    flat_e = top_idx.reshape(tk)
    flat_tok = jnp.repeat(jnp.arange(t, dtype=jnp.int32), TOP_K)
    flat_w = top_w.reshape(tk)
    order = jnp.argsort(flat_e)
    e_sorted = flat_e[order]
    counts = jnp.bincount(flat_e, length=N_EXPERTS)
    start = jnp.cumsum(counts) - counts
    padded = (counts + MOE_BLOCK - 1) // MOE_BLOCK * MOE_BLOCK
    pad_end = jnp.cumsum(padded)
    pad_start = pad_end - padded
    dest = pad_start[e_sorted] + jnp.arange(tk, dtype=jnp.int32) - start[e_sorted]
    slot_tok = jnp.full((nb * MOE_BLOCK,), t, jnp.int32).at[dest].set(flat_tok[order])
    slot_w = jnp.zeros((nb * MOE_BLOCK,), xs.dtype).at[dest].set(flat_w[order].astype(xs.dtype))
    block_e = jnp.searchsorted(pad_end, jnp.arange(nb, dtype=jnp.int32) * MOE_BLOCK, side='right')
    block_e = jnp.minimum(block_e, N_EXPERTS - 1)
    xs_pad = jnp.concatenate([xs, jnp.zeros((1, d), xs.dtype)], axis=0)

    def body(acc, blk):
        tok, wt, e = blk
        xb = xs_pad[tok]
        yb = swiglu(xb, w_gate[e], w_up[e], w_down[e]) * wt[:, None]
        return acc.at[tok].add(yb), None

    acc, _ = lax.scan(body, jnp.zeros((t + 1, d), xs.dtype),
                      (slot_tok.reshape(nb, MOE_BLOCK), slot_w.reshape(nb, MOE_BLOCK), block_e))
    return acc[:t]


def moe(h, w_router, router_bias, w_e_gate, w_e_up, w_e_down, w_sh_gate, w_sh_up, w_sh_down):
    bsz, s, d = h.shape
    t = bsz * s
    xs = h.reshape(t, d)
    scores = jax.nn.sigmoid((xs @ w_router).astype(jnp.float32))
    choice = scores + router_bias.astype(jnp.float32)
    grp = choice.reshape(t, N_EXPERT_GROUPS, N_EXPERTS // N_EXPERT_GROUPS)
    grp_score = lax.top_k(grp, 2)[0].sum(-1)
    _, grp_idx = lax.top_k(grp_score, TOPK_GROUPS)
    grp_mask = jnp.any(jax.nn.one_hot(grp_idx, N_EXPERT_GROUPS) > 0, axis=1)
    exp_mask = jnp.repeat(grp_mask, N_EXPERTS // N_EXPERT_GROUPS, axis=1)
    _, top_idx = lax.top_k(jnp.where(exp_mask, choice, -jnp.inf), TOP_K)
    top_w = jnp.take_along_axis(scores, top_idx, axis=1)
    top_w = top_w / (top_w.sum(-1, keepdims=True) + 1e-20) * ROUTED_SCALE
    routed = routed_experts(xs, top_idx, top_w, w_e_gate, w_e_up, w_e_down)
    shared = swiglu(xs, w_sh_gate, w_sh_up, w_sh_down)
    return (routed + shared).reshape(bsz, s, d)


def setup_inputs(seed: int = 0) -> dict:
    key = jax.random.key(seed)
    ks = iter(jax.random.split(key, 40))
    L, D = DEPTH, D_MODEL

    def nrm(shape, scale):
        return scale * jax.random.normal(next(ks), shape, jnp.float32)

    dt0 = jnp.exp(jax.random.uniform(next(ks), (L, SSD_N_HEADS), jnp.float32,
                                     minval=np.log(1e-3), maxval=np.log(1e-1)))
    return {
        'x': nrm((BATCH, SEQ, D), 1.0),
        'c': nrm((BATCH, D), 1.0),
        'w_ada': nrm((L, D, 6 * D), 0.5 * D ** -0.5),
        'b_ada': nrm((L, 6 * D), 0.02),
        'w_in': nrm((L, D, IN_COLS), D ** -0.5),
        'conv_w': nrm((L, SSD_CONV, SSD_XBC), SSD_CONV ** -0.5),
        'conv_b': nrm((L, SSD_XBC), 0.02),
        'dt_bias': dt0 + jnp.log(-jnp.expm1(-dt0)),
        'a_log': jnp.log(jax.random.uniform(next(ks), (L, SSD_N_HEADS), jnp.float32, minval=1.0, maxval=16.0)),
        'd_skip': 1.0 + nrm((L, SSD_N_HEADS), 0.1),
        'ssd_norm_w': 1.0 + nrm((L, SSD_D_INNER), 0.05),
        'gmlp_ln_g': 1.0 + nrm((L, GMLP_D), 0.05),
        'gmlp_ln_b': nrm((L, GMLP_D), 0.02),
        'gmlp_ws': nrm((L, GMLP_GROUPS, GMLP_CHUNK, GMLP_CHUNK), 0.5 * GMLP_CHUNK ** -0.5),
        'gmlp_bs': 1.0 + nrm((L, GMLP_GROUPS, GMLP_CHUNK), 0.1),
        'w_proj_ssd': nrm((L, SSD_D_INNER, D), DEEPNORM_BETA * SSD_D_INNER ** -0.5),
        'w_proj_gmlp': nrm((L, GMLP_D, D), DEEPNORM_BETA * GMLP_D ** -0.5),
        'w_out': nrm((L, D, D), DEEPNORM_BETA * D ** -0.5),
        'ln1_g': 1.0 + nrm((L, D), 0.05),
        'ln1_b': nrm((L, D), 0.02),
        'w_router': nrm((L, D, N_EXPERTS), D ** -0.5),
        'router_bias': nrm((L, N_EXPERTS), 0.01),
        'w_e_gate': nrm((L, N_EXPERTS, D, EXPERT_FF), D ** -0.5),
        'w_e_up': nrm((L, N_EXPERTS, D, EXPERT_FF), D ** -0.5),
        'w_e_down': nrm((L, N_EXPERTS, EXPERT_FF, D), DEEPNORM_BETA * EXPERT_FF ** -0.5),
        'w_sh_gate': nrm((L, D, SHARED_FF), D ** -0.5),
        'w_sh_up': nrm((L, D, SHARED_FF), D ** -0.5),
        'w_sh_down': nrm((L, SHARED_FF, D), DEEPNORM_BETA * SHARED_FF ** -0.5),
        'ln2_g': 1.0 + nrm((L, D), 0.05),
        'ln2_b': nrm((L, D), 0.02),
    }


def reference(x, c, w_ada, b_ada, w_in, conv_w, conv_b, dt_bias, a_log, d_skip, ssd_norm_w,
              gmlp_ln_g, gmlp_ln_b, gmlp_ws, gmlp_bs, w_proj_ssd, w_proj_gmlp, w_out,
              ln1_g, ln1_b, w_router, router_bias, w_e_gate, w_e_up, w_e_down,
              w_sh_gate, w_sh_up, w_sh_down, ln2_g, ln2_b):
    h = x
    for i in range(DEPTH):
        ada = jax.nn.silu(c) @ w_ada[i] + b_ada[i]
        sh1, sc1, g1, sh2, sc2, g2 = jnp.split(ada, 6, axis=-1)
        m = modulate(h, sh1, sc1)
        mix = token_mixer(m, w_in[i], conv_w[i], conv_b[i], dt_bias[i], a_log[i], d_skip[i],
                          ssd_norm_w[i], gmlp_ln_g[i], gmlp_ln_b[i], gmlp_ws[i], gmlp_bs[i],
                          w_proj_ssd[i], w_proj_gmlp[i], w_out[i])
        h = layer_norm(DEEPNORM_ALPHA * h + g1[:, None, :] * mix) * ln1_g[i] + ln1_b[i]
        m = modulate(h, sh2, sc2)
        f = moe(m, w_router[i], router_bias[i], w_e_gate[i], w_e_up[i], w_e_down[i],
                w_sh_gate[i], w_sh_up[i], w_sh_down[i])
        h = layer_norm(DEEPNORM_ALPHA * h + g2[:, None, :] * f) * ln2_g[i] + ln2_b[i]
    return h
```

```python
import functools

import jax
import jax.numpy as jnp
from jax import lax
from jax.experimental import pallas as pl
from jax.experimental.pallas import tpu as pltpu

F32 = jnp.float32
BF16 = jnp.bfloat16

HEAD_DIM = 64
N_GROUPS = 8
HEADS_PER_GROUP = 4
N_HEADS = N_GROUPS * HEADS_PER_GROUP
D_STATE = 128
CONV_K = 4
CHUNK = 128
GMLP_GROUPS = 8
N_EXPERTS = 256
TOP_K = 8
N_EXPERT_GROUPS = 8
EXPERTS_PER_GROUP = N_EXPERTS // N_EXPERT_GROUPS
TOPK_GROUPS = 4
ROUTED_SCALE = 2.5
DEPTH = 1
DEEPNORM_ALPHA = (2 * DEPTH) ** 0.25
LN_EPS = 1e-5
RMS_EPS = 1e-5

LANES = 128
SUBLANES = 8
VMEM_BYTES_V7X = 64 * 1024 * 1024

MIX_TILE = 256
ROUTE_TILE = 256
SCATTER_TILE = 256
EXPERT_BLOCK = 256
COMBINE_TILE = 128


def _dot(a, b):
    return jnp.dot(a, b, preferred_element_type=F32)


def _dot_nt(a, b):
    return lax.dot_general(a, b, (((1,), (1,)), ((), ())), preferred_element_type=F32)


def _dot_tn(a, b):
    return lax.dot_general(a, b, (((0,), (0,)), ((), ())), preferred_element_type=F32)


def _layer_norm(x):
    xc = x - jnp.mean(x, -1, keepdims=True)
    var = jnp.mean(xc * xc, -1, keepdims=True)
    return xc * lax.rsqrt(var + LN_EPS)


def _silu(x):
    return x * jax.nn.sigmoid(x)


def _softplus(x):
    return jnp.maximum(x, 0.0) + jnp.log1p(jnp.exp(-jnp.abs(x)))


def _ada_kernel(c_ref, w_ref, b_ref, o_ref):
    s = _silu(c_ref[...]).astype(BF16)
    o_ref[...] = _dot(s, w_ref[...].astype(BF16)) + b_ref[...]


def _ada(c, w_ada, b_ada):
    bsz, d = c.shape
    n = w_ada.shape[1]
    tn = 1024
    return pl.pallas_call(
        _ada_kernel,
        grid=(n // tn,),
        in_specs=[pl.BlockSpec((bsz, d), lambda j: (0, 0)),
                  pl.BlockSpec((d, tn), lambda j: (0, j)),
                  pl.BlockSpec((1, tn), lambda j: (0, j))],
        out_specs=pl.BlockSpec((bsz, tn), lambda j: (0, j)),
        out_shape=jax.ShapeDtypeStruct((bsz, n), F32),
        name="ada",
    )(c, w_ada, b_ada.reshape(1, n))


def _split3_bf16(a):
    hi = a.astype(BF16).astype(F32)
    r1 = a - hi
    mid = r1.astype(BF16).astype(F32)
    lo = r1 - mid
    pad = jnp.zeros((a.shape[0], LANES - 3 * N_HEADS), F32)
    return jnp.concatenate([hi, mid, lo, pad], axis=1).astype(BF16)


def _mixer_kernel(x_ref, mod_ref, w_ref, convw_ref, convb_ref, dtb_ref, alog_ref, dskip_ref,
                  normw_ref, glng_ref, glnb_ref, ws_ref, bsx_ref, e64_ref, e128_ref,
                  wpssd_ref, wpgmlp_ref, wout_ref, ln1g_ref, ln1b_ref,
                  o_ref,
                  m_s, xs_s, bm_s, cm_s, dt_s, y_s, state_s, tail_s,
                  *, tile, d_model, d_inner, offs):
    off_z, off_xbc, off_uv, off_ga, off_gb, off_dt = offs
    n_chunks = tile // CHUNK
    gw = HEADS_PER_GROUP * HEAD_DIM

    @pl.when(pl.program_id(1) == 0)
    def _():
        state_s[...] = jnp.zeros_like(state_s)
        tail_s[...] = jnp.zeros_like(tail_s)

    x = x_ref[0]
    mod = mod_ref[0]
    sh1, sc1, g1 = mod[0:1], mod[1:2], mod[2:3]
    m_s[...] = (_layer_norm(x) * (1.0 + sc1) + sh1).astype(BF16)

    row8 = lax.broadcasted_iota(jnp.int32, (SUBLANES, gw), 0)
    n_xblk = d_inner // gw
    n_bblk = N_GROUPS * D_STATE // gw
    for blk in range(n_xblk + 2 * n_bblk):
        c0 = blk * gw
        pre = _dot(m_s[...], w_ref[:, off_xbc + c0:off_xbc + c0 + gw])
        prev_tail = tail_s[:, c0:c0 + gw]
        tail_s[:, c0:c0 + gw] = pre[tile - SUBLANES:tile]
        cw = convw_ref[:, c0:c0 + gw]
        acc = pre * cw[CONV_K - 1:CONV_K] + convb_ref[:, c0:c0 + gw]
        for k in range(1, CONV_K):
            r = pltpu.roll(pre, k, axis=0)
            top = jnp.where(row8 < k, pltpu.roll(prev_tail, k, axis=0), r[0:SUBLANES])
            r = jnp.concatenate([top, r[SUBLANES:]], axis=0)
            acc = acc + r * cw[CONV_K - 1 - k:CONV_K - k]
        act = _silu(acc)
        if blk < n_xblk:
            xs_s[blk] = act
        else:
            dst = bm_s if blk < n_xblk + n_bblk else cm_s
            g0 = ((blk - n_xblk) % n_bblk) * (gw // D_STATE)
            for i in range(gw // D_STATE):
                dst[g0 + i] = act[:, i * D_STATE:(i + 1) * D_STATE].astype(BF16)

    dt_s[...] = _softplus(_dot(m_s[...], w_ref[:, off_dt:off_dt + LANES]) + dtb_ref[...])
    a_neg = -jnp.exp(alog_ref[...])

    rowc = lax.broadcasted_iota(jnp.int32, (CHUNK, LANES), 0)
    causal = (lax.broadcasted_iota(jnp.int32, (CHUNK, CHUNK), 0)
              >= lax.broadcasted_iota(jnp.int32, (CHUNK, CHUNK), 1))
    lane_g = lax.broadcasted_iota(jnp.int32, (CHUNK, gw), 1)

    def chunk_body(c, carry):
        r0 = pl.multiple_of(c * CHUNK, CHUNK)
        rows = pl.ds(r0, CHUNK)
        dt_c = dt_s[rows, :]
        acs = dt_c * a_neg
        sh = 1
        while sh < CHUNK:
            acs = acs + jnp.where(rowc >= sh, pltpu.roll(acs, sh, axis=0), 0.0)
            sh *= 2
        acs3 = _split3_bf16(acs[:, :N_HEADS])
        acs_x128 = _dot(acs3, e128_ref[...])
        acs_x64 = _dot(acs3, e64_ref[...])
        dt_x64 = _dot(_split3_bf16(dt_c[:, :N_HEADS]), e64_ref[...])
        acs_t = acs.T
        last = acs_x64[CHUNK - 1:CHUNK, :]
        grow = jnp.exp(acs_x64)
        to_end = jnp.exp(last - acs_x64)
        chunk_decay = jnp.exp(last)

        for g in range(N_GROUPS):
            cols = slice(g * gw, (g + 1) * gw)
            xs_g = xs_s[g, rows, :]
            bm_g = bm_s[g, rows, :]
            cm_g = cm_s[g, rows, :]
            xdt = xs_g * dt_x64[:, cols]
            cb = _dot_nt(cm_g, bm_g)
            y_g = None
            for r in range(HEADS_PER_GROUP):
                h = g * HEADS_PER_GROUP + r
                seg = acs_x128[:, h * CHUNK:(h + 1) * CHUNK] - acs_t[h:h + 1, :]
                w = (cb * jnp.exp(jnp.where(causal, seg, -jnp.inf))).astype(BF16)
                in_head = (lane_g >= r * HEAD_DIM) & (lane_g < (r + 1) * HEAD_DIM)
                part = _dot(w, jnp.where(in_head, xdt, 0.0).astype(BF16))
                y_g = part if y_g is None else y_g + part
            st = state_s[g]
            y_g = y_g + _dot(cm_g, st.astype(BF16)) * grow[:, cols]
            state_s[g] = chunk_decay[:, cols] * st + _dot_tn(bm_g, (xdt * to_end[:, cols]).astype(BF16))
            y_s[rows, cols] = y_g + dskip_ref[:, cols] * xs_g
        return carry

    lax.fori_loop(0, n_chunks, chunk_body, 0)

    z = _dot(m_s[...], w_ref[:, off_z:off_z + d_inner])
    y = y_s[...] * _silu(z)
    y = y * lax.rsqrt(jnp.mean(y * y, -1, keepdims=True) + RMS_EPS) * normw_ref[...]
    y_a = _dot(y.astype(BF16), wpssd_ref[...])
    merged = jax.nn.sigmoid(_dot(m_s[...], w_ref[:, off_ga:off_ga + d_model])) * y_a

    uv = jax.nn.gelu(_dot(m_s[...], w_ref[:, off_uv:off_uv + 2 * d_model]))
    u = uv[:, :d_model]
    v = _layer_norm(uv[:, d_model:]) * glng_ref[...] + glnb_ref[...]
    v = v.astype(BF16)
    gd = d_model // GMLP_GROUPS
    ws_m = [jnp.where(causal, ws_ref[g], 0.0).astype(BF16) for g in range(GMLP_GROUPS)]
    v_rows = []
    for c in range(n_chunks):
        v_c = v[c * CHUNK:(c + 1) * CHUNK]
        v_rows.append(jnp.concatenate(
            [_dot(ws_m[g], v_c[:, g * gd:(g + 1) * gd]) for g in range(GMLP_GROUPS)], axis=1)
            + bsx_ref[...])
    v_mix = jnp.concatenate(v_rows, axis=0) if n_chunks > 1 else v_rows[0]
    y_b = _dot((u * v_mix).astype(BF16), wpgmlp_ref[...])
    merged = merged + jax.nn.sigmoid(_dot(m_s[...], w_ref[:, off_gb:off_gb + d_model])) * y_b

    mix = _dot(merged.astype(BF16), wout_ref[...])
    o_ref[0] = _layer_norm(DEEPNORM_ALPHA * x + g1 * mix) * ln1g_ref[...] + ln1b_ref[...]


def _const_spec(shape):
    nd = len(shape)
    return pl.BlockSpec(shape, lambda b, j: (0,) * nd, pipeline_mode=pl.Buffered(1))


def _mixer(x, mod1, w_in, conv_w, conv_b, dt_bias, a_log, d_skip, ssd_norm_w,
           gmlp_ln_g, gmlp_ln_b, gmlp_ws, gmlp_bs, w_proj_ssd, w_proj_gmlp, w_out, ln1_g, ln1_b):
    bsz, seq, d = x.shape
    d_inner = N_HEADS * HEAD_DIM
    gn = N_GROUPS * D_STATE
    d_xbc = d_inner + 2 * gn
    tile = min(MIX_TILE, seq)
    assert seq % tile == 0 and tile % CHUNK == 0 and d_inner == 2 * d

    s0, s1, s2, s3, s4 = (d_inner, d_inner + d_xbc, d_inner + d_xbc + N_HEADS,
                          d_inner + d_xbc + N_HEADS + 2 * d, d_inner + d_xbc + N_HEADS + 3 * d)
    w_dt = jnp.pad(w_in[:, s1:s2], ((0, 0), (0, LANES - N_HEADS)))
    w_all = jnp.concatenate([w_in[:, :s1], w_in[:, s2:], w_dt], axis=1).astype(BF16)
    off_z, off_xbc = 0, d_inner
    off_uv = s1
    off_ga = off_uv + 2 * d
    off_gb = off_ga + d
    off_dt = off_gb + d
    offs = (off_z, off_xbc, off_uv, off_ga, off_gb, off_dt)

    pad_h = lambda v: jnp.pad(v.reshape(1, N_HEADS), ((0, 0), (0, LANES - N_HEADS)))
    head_of_col64 = jnp.arange(d_inner) // HEAD_DIM
    head_of_col128 = jnp.arange(N_HEADS * CHUNK) // CHUNK
    piece_head = jnp.where(jnp.arange(LANES) < 3 * N_HEADS, jnp.arange(LANES) % N_HEADS, -1)
    e64 = (piece_head[:, None] == head_of_col64[None, :]).astype(BF16)
    e128 = (piece_head[:, None] == head_of_col128[None, :]).astype(BF16)
    dskip_x = jnp.repeat(d_skip, HEAD_DIM).reshape(1, d_inner)
    bs_x = jnp.repeat(gmlp_bs.T, d // GMLP_GROUPS, axis=1)

    operands = [
        w_all, conv_w, conv_b.reshape(1, d_xbc), pad_h(dt_bias), pad_h(a_log), dskip_x,
        ssd_norm_w.reshape(1, d_inner), gmlp_ln_g.reshape(1, d), gmlp_ln_b.reshape(1, d),
        gmlp_ws, bs_x, e64, e128,
        w_proj_ssd.astype(BF16), w_proj_gmlp.astype(BF16), w_out.astype(BF16),
        ln1_g.reshape(1, d), ln1_b.reshape(1, d),
    ]
    gw = HEADS_PER_GROUP * HEAD_DIM
    kern = functools.partial(_mixer_kernel, tile=tile, d_model=d, d_inner=d_inner, offs=offs)
    return pl.pallas_call(
        kern,
        grid=(bsz, seq // tile),
        in_specs=[pl.BlockSpec((1, tile, d), lambda b, j: (b, j, 0)),
                  pl.BlockSpec((1, SUBLANES, d), lambda b, j: (b, 0, 0))]
                 + [_const_spec(op.shape) for op in operands],
        out_specs=pl.BlockSpec((1, tile, d), lambda b, j: (b, j, 0)),
        out_shape=jax.ShapeDtypeStruct((bsz, seq, d), F32),
        scratch_shapes=[
            pltpu.VMEM((tile, d), BF16),
            pltpu.VMEM((N_GROUPS, tile, gw), F32),
            pltpu.VMEM((N_GROUPS, tile, D_STATE), BF16),
            pltpu.VMEM((N_GROUPS, tile, D_STATE), BF16),
            pltpu.VMEM((tile, LANES), F32),
            pltpu.VMEM((tile, d_inner), F32),
            pltpu.VMEM((N_GROUPS, D_STATE, gw), F32),
            pltpu.VMEM((SUBLANES, d_xbc), F32),
        ],
        compiler_params=pltpu.CompilerParams(
            dimension_semantics=("arbitrary", "arbitrary"),
            vmem_limit_bytes=56 * 1024 * 1024),
        name="mixer",
    )(x, mod1, *operands)


def _route_kernel(h_ref, mod_ref, wr_ref, bias_ref, topi_ref, topw_ref, rank_ref, cnt_ref, cnt_s,
                  *, tile):
    @pl.when(pl.program_id(0) == 0)
    def _():
        cnt_s[...] = jnp.zeros_like(cnt_s)

    mod = mod_ref[0]
    m2 = _layer_norm(h_ref[...]) * (1.0 + mod[1:2]) + mod[0:1]
    scores = jax.nn.sigmoid(_dot_nt(wr_ref[...], m2.astype(BF16)))
    choice = scores + bias_ref[...]

    neg = -jnp.inf
    c3 = choice.reshape(N_EXPERT_GROUPS, EXPERTS_PER_GROUP, tile)
    i3 = lax.broadcasted_iota(jnp.int32, c3.shape, 1).astype(F32)
    m1 = jnp.max(c3, axis=1, keepdims=True)
    i1 = jnp.min(jnp.where(c3 == m1, i3, float(EXPERTS_PER_GROUP)), axis=1, keepdims=True)
    second = jnp.max(jnp.where(i3 == i1, neg, c3), axis=1)
    gs = m1[:, 0, :] + second

    gi = lax.broadcasted_iota(jnp.int32, gs.shape, 0).astype(F32)
    gsel = jnp.zeros(gs.shape, F32)
    for _ in range(TOPK_GROUPS):
        mx = jnp.max(gs, axis=0, keepdims=True)
        ix = jnp.min(jnp.where(gs == mx, gi, float(N_EXPERT_GROUPS)), axis=0, keepdims=True)
        hit = gi == ix
        gsel = jnp.where(hit, 1.0, gsel)
        gs = jnp.where(hit, neg, gs)
    emask = jnp.broadcast_to(gsel[:, None, :], c3.shape).reshape(N_EXPERTS, tile) > 0.0
    masked = jnp.where(emask, choice, neg)

    ei = lax.broadcasted_iota(jnp.int32, masked.shape, 0).astype(F32)
    hits, idx_rows, w_rows = [], [], []
    for _ in range(TOP_K):
        mx = jnp.max(masked, axis=0, keepdims=True)
        ix = jnp.min(jnp.where(masked == mx, ei, float(N_EXPERTS)), axis=0, keepdims=True)
        hit = ei == ix
        hits.append(hit)
        idx_rows.append(ix)
        w_rows.append(jnp.sum(jnp.where(hit, scores, 0.0), axis=0, keepdims=True))
        masked = jnp.where(hit, neg, masked)
    w_all = jnp.concatenate(w_rows, axis=0)
    topw_ref[...] = w_all / (jnp.sum(w_all, axis=0, keepdims=True) + 1e-20) * ROUTED_SCALE
    topi_ref[...] = jnp.concatenate(idx_rows, axis=0).astype(jnp.int32)

    assign = jnp.zeros(masked.shape, F32)
    for hit in hits:
        assign = jnp.where(hit, 1.0, assign)
    assign_b = assign.astype(BF16)
    t0 = lax.broadcasted_iota(jnp.int32, (tile, tile), 0)
    t1 = lax.broadcasted_iota(jnp.int32, (tile, tile), 1)
    before = jnp.where(t0 < t1, 1.0, 0.0).astype(BF16)
    base = cnt_s[...]
    pos = _dot(assign_b, before) + jnp.concatenate([base] * (tile // LANES), axis=1)
    rank_ref[...] = jnp.concatenate(
        [jnp.sum(jnp.where(hit, pos, 0.0), axis=0, keepdims=True) for hit in hits],
        axis=0).astype(jnp.int32)
    cnt_s[...] = base + _dot(assign_b, jnp.ones((tile, LANES), BF16))
    cnt_ref[...] = cnt_s[...]


def _route(h1, mod2, w_router, router_bias, tokens_per_batch):
    t, d = h1.shape
    tile = min(ROUTE_TILE, tokens_per_batch)
    assert tokens_per_batch % tile == 0 and tile % LANES == 0
    per_b = tokens_per_batch // tile
    wr_t = w_router.T.astype(BF16)
    bias_x = jnp.broadcast_to(router_bias.reshape(N_EXPERTS, 1), (N_EXPERTS, tile)).astype(F32)
    kern = functools.partial(_route_kernel, tile=tile)
    row_spec = pl.BlockSpec((TOP_K, tile), lambda i: (0, i))
    return pl.pallas_call(
        kern,
        grid=(t // tile,),
        in_specs=[pl.BlockSpec((tile, d), lambda i: (i, 0)),
                  pl.BlockSpec((1, SUBLANES, d), lambda i: (i // per_b, 0, 0)),
                  pl.BlockSpec((N_EXPERTS, d), lambda i: (0, 0)),
                  pl.BlockSpec((N_EXPERTS, tile), lambda i: (0, 0))],
        out_specs=[row_spec, row_spec, row_spec,
                   pl.BlockSpec((N_EXPERTS, LANES), lambda i: (0, 0))],
        out_shape=[jax.ShapeDtypeStruct((TOP_K, t), jnp.int32),
                   jax.ShapeDtypeStruct((TOP_K, t), F32),
                   jax.ShapeDtypeStruct((TOP_K, t), jnp.int32),
                   jax.ShapeDtypeStruct((N_EXPERTS, LANES), F32)],
        scratch_shapes=[pltpu.VMEM((N_EXPERTS, LANES), F32)],
        compiler_params=pltpu.CompilerParams(dimension_semantics=("arbitrary",)),
        name="route",
    )(h1, mod2, wr_t, bias_x)


def _row_copy_wait(src_rows_ref, dst_rows_ref, sem):
    pltpu.make_async_copy(src_rows_ref, dst_rows_ref, sem).wait()


def _scatter_kernel(dest_ref, h_ref, mod_ref, zeros_ref, out_ref, m2_s, sem, *, tile):
    del zeros_ref
    mod = mod_ref[0]
    m2_s[...] = _layer_norm(h_ref[...]) * (1.0 + mod[1:2]) + mod[0:1]

    def issue(t, carry):
        for k in range(TOP_K):
            d = dest_ref[t * TOP_K + k]
            pltpu.make_async_copy(m2_s.at[pl.ds(t, 1), :], out_ref.at[pl.ds(d, 1), :], sem).start()
        return carry

    lax.fori_loop(0, tile, issue, 0)
    for _ in range(TOP_K):
        _row_copy_wait(m2_s, out_ref.at[pl.ds(0, tile), :], sem)


def _scatter(h1, mod2, dest_flat, n_slots, tokens_per_batch):
    t, d = h1.shape
    tile = min(SCATTER_TILE, tokens_per_batch)
    per_b = tokens_per_batch // tile
    kern = functools.partial(_scatter_kernel, tile=tile)
    return pl.pallas_call(
        kern,
        grid=(t // tile,),
        in_specs=[pl.BlockSpec((tile * TOP_K,), lambda i: (i,), memory_space=pltpu.SMEM),
                  pl.BlockSpec((tile, d), lambda i: (i, 0)),
                  pl.BlockSpec((1, SUBLANES, d), lambda i: (i // per_b, 0, 0)),
                  pl.BlockSpec(memory_space=pl.ANY)],
        out_specs=pl.BlockSpec(memory_space=pl.ANY),
        out_shape=jax.ShapeDtypeStruct((n_slots, d), F32),
        scratch_shapes=[pltpu.VMEM((tile, d), F32), pltpu.SemaphoreType.DMA(())],
        input_output_aliases={3: 0},
        compiler_params=pltpu.CompilerParams(dimension_semantics=("arbitrary",),
                                             has_side_effects=True),
        name="scatter",
    )(dest_flat, h1, mod2, jnp.zeros((n_slots, d), F32))


def _expert_kernel(be_ref, nv_ref, x_ref, wg_ref, wu_ref, wd_ref, o_ref):
    del be_ref
    nv = nv_ref[pl.program_id(0)]

    @pl.when(nv > 0)
    def _():
        xb = x_ref[...].astype(BF16)
        gate = _dot(xb, wg_ref[0].astype(BF16))
        up = _dot(xb, wu_ref[0].astype(BF16))
        o_ref[...] = _dot((_silu(gate) * up).astype(BF16), wd_ref[0].astype(BF16))

    @pl.when(nv == 0)
    def _():
        o_ref[...] = jnp.zeros_like(o_ref)


def _experts(xs_sorted, block_e, block_nv, w_e_gate, w_e_up, w_e_down):
    n_slots, d = xs_sorted.shape
    ff = w_e_gate.shape[-1]
    nb = n_slots // EXPERT_BLOCK
    grid_spec = pltpu.PrefetchScalarGridSpec(
        num_scalar_prefetch=2,
        grid=(nb,),
        in_specs=[pl.BlockSpec((EXPERT_BLOCK, d), lambda b, be, nv: (b, 0)),
                  pl.BlockSpec((1, d, ff), lambda b, be, nv: (be[b], 0, 0)),
                  pl.BlockSpec((1, d, ff), lambda b, be, nv: (be[b], 0, 0)),
                  pl.BlockSpec((1, ff, d), lambda b, be, nv: (be[b], 0, 0))],
        out_specs=pl.BlockSpec((EXPERT_BLOCK, d), lambda b, be, nv: (b, 0)),
    )
    return pl.pallas_call(
        _expert_kernel,
        grid_spec=grid_spec,
        out_shape=jax.ShapeDtypeStruct((n_slots, d), F32),
        compiler_params=pltpu.CompilerParams(dimension_semantics=("arbitrary",)),
        name="experts",
    )(block_e, block_nv, xs_sorted, w_e_gate, w_e_up, w_e_down)


def _combine_kernel(dest_ref, h_ref, mod_ref, w_ref, y_ref, wsg_ref, wsu_ref, wsd_ref,
                    ln2g_ref, ln2b_ref, o_ref, ybuf_s, sem, *, tile):
    def issue(t, carry):
        for k in range(TOP_K):
            d = dest_ref[t * TOP_K + k]
            pltpu.make_async_copy(y_ref.at[pl.ds(d, 1), :], ybuf_s.at[k, pl.ds(t, 1), :], sem).start()
        return carry

    lax.fori_loop(0, tile, issue, 0)

    h = h_ref[...]
    mod = mod_ref[0]
    m2 = (_layer_norm(h) * (1.0 + mod[1:2]) + mod[0:1]).astype(BF16)
    acc = _dot((_silu(_dot(m2, wsg_ref[...])) * _dot(m2, wsu_ref[...])).astype(BF16), wsd_ref[...])

    for k in range(TOP_K):
        _row_copy_wait(y_ref.at[pl.ds(0, tile), :], ybuf_s.at[k], sem)
    w = w_ref[...]
    for k in range(TOP_K):
        acc = acc + ybuf_s[k] * w[:, k:k + 1]
    o_ref[...] = _layer_norm(DEEPNORM_ALPHA * h + mod[2:3] * acc) * ln2g_ref[...] + ln2b_ref[...]


def _combine(h1, mod2, dest_flat, topw, y_sorted, w_sh_gate, w_sh_up, w_sh_down, ln2_g, ln2_b,
             tokens_per_batch):
    t, d = h1.shape
    ff = w_sh_gate.shape[-1]
    tile = min(COMBINE_TILE, tokens_per_batch)
    per_b = tokens_per_batch // tile
    kern = functools.partial(_combine_kernel, tile=tile)
    const = lambda shape: pl.BlockSpec(shape, lambda i: (0,) * len(shape))
    return pl.pallas_call(
        kern,
        grid=(t // tile,),
        in_specs=[pl.BlockSpec((tile * TOP_K,), lambda i: (i,), memory_space=pltpu.SMEM),
                  pl.BlockSpec((tile, d), lambda i: (i, 0)),
                  pl.BlockSpec((1, SUBLANES, d), lambda i: (i // per_b, 0, 0)),
                  pl.BlockSpec((tile, TOP_K), lambda i: (i, 0)),
                  pl.BlockSpec(memory_space=pl.ANY),
                  const((d, ff)), const((d, ff)), const((ff, d)), const((1, d)), const((1, d))],
        out_specs=pl.BlockSpec((tile, d), lambda i: (i, 0)),
        out_shape=jax.ShapeDtypeStruct((t, d), F32),
        scratch_shapes=[pltpu.VMEM((TOP_K, tile, d), F32), pltpu.SemaphoreType.DMA(())],
        compiler_params=pltpu.CompilerParams(dimension_semantics=("arbitrary",)),
        name="combine",
    )(dest_flat, h1, mod2, topw, y_sorted, w_sh_gate.astype(BF16), w_sh_up.astype(BF16),
      w_sh_down.astype(BF16), ln2_g.reshape(1, d), ln2_b.reshape(1, d))


def _moe(h1, mod2, w_router, router_bias, w_e_gate, w_e_up, w_e_down,
         w_sh_gate, w_sh_up, w_sh_down, ln2_g, ln2_b, tokens_per_batch):
    t, d = h1.shape
    topi_t, topw_t, rank_t, cnt = _route(h1, mod2, w_router, router_bias, tokens_per_batch)

    blk = EXPERT_BLOCK
    nb = -(-(t * TOP_K) // blk) + N_EXPERTS
    counts = cnt[:, 0].astype(jnp.int32)
    padded = (counts + blk - 1) // blk * blk
    pad_end = jnp.cumsum(padded)
    pad_start = pad_end - padded
    dest_flat = (pad_start[topi_t] + rank_t).T.reshape(t * TOP_K)
    block_lo = jnp.arange(nb, dtype=jnp.int32) * blk
    block_e = jnp.minimum(jnp.searchsorted(pad_end, block_lo, side='right'),
                          N_EXPERTS - 1).astype(jnp.int32)
    block_nv = jnp.clip(counts[block_e] - (block_lo - pad_start[block_e]), 0, blk).astype(jnp.int32)

    xs_sorted = _scatter(h1, mod2, dest_flat, nb * blk, tokens_per_batch)
    y_sorted = _experts(xs_sorted, block_e, block_nv, w_e_gate, w_e_up, w_e_down)
    return _combine(h1, mod2, dest_flat, topw_t.T, y_sorted, w_sh_gate, w_sh_up, w_sh_down,
                    ln2_g, ln2_b, tokens_per_batch)


def _pad_rows(parts, d):
    rows = jnp.stack(parts, axis=1)
    return jnp.pad(rows, ((0, 0), (0, SUBLANES - rows.shape[1]), (0, 0)))


def kernel(x, c, w_ada, b_ada, w_in, conv_w, conv_b, dt_bias, a_log, d_skip, ssd_norm_w, gmlp_ln_g, gmlp_ln_b, gmlp_ws, gmlp_bs, w_proj_ssd, w_proj_gmlp, w_out, ln1_g, ln1_b, w_router, router_bias, w_e_gate, w_e_up, w_e_down, w_sh_gate, w_sh_up, w_sh_down, ln2_g, ln2_b):
    bsz, seq, d = x.shape
    h = x
    for i in range(w_ada.shape[0]):
        ada = _ada(c, w_ada[i], b_ada[i])
        sh1, sc1, g1, sh2, sc2, g2 = jnp.split(ada, 6, axis=-1)
        h = _mixer(h, _pad_rows([sh1, sc1, g1], d), w_in[i], conv_w[i], conv_b[i], dt_bias[i],
                   a_log[i], d_skip[i], ssd_norm_w[i], gmlp_ln_g[i], gmlp_ln_b[i], gmlp_ws[i],
                   gmlp_bs[i], w_proj_ssd[i], w_proj_gmlp[i], w_out[i], ln1_g[i], ln1_b[i])
        h = _moe(h.reshape(bsz * seq, d), _pad_rows([sh2, sc2, g2], d), w_router[i],
                 router_bias[i], w_e_gate[i], w_e_up[i], w_e_down[i], w_sh_gate[i], w_sh_up[i],
                 w_sh_down[i], ln2_g[i], ln2_b[i], seq).reshape(bsz, seq, d)
    return h
```

```python
import functools

import jax
import jax.numpy as jnp
from jax import lax
from jax.experimental import pallas as pl
from jax.experimental.pallas import tpu as pltpu

F32 = jnp.float32
BF16 = jnp.bfloat16

HEAD_DIM = 64
N_GROUPS = 8
HEADS_PER_GROUP = 4
N_HEADS = N_GROUPS * HEADS_PER_GROUP
D_STATE = 128
CONV_K = 4
CHUNK = 128
GMLP_GROUPS = 8
N_EXPERTS = 256
TOP_K = 8
N_EXPERT_GROUPS = 8
EXPERTS_PER_GROUP = N_EXPERTS // N_EXPERT_GROUPS
TOPK_GROUPS = 4
ROUTED_SCALE = 2.5
DEPTH = 1
DEEPNORM_ALPHA = (2 * DEPTH) ** 0.25
LN_EPS = 1e-5
RMS_EPS = 1e-5

LANES = 128
SUBLANES = 8
VMEM_BYTES_V7X = 64 * 1024 * 1024

MIX_TILE = 256
ROUTE_TILE = 256
SCATTER_TILE = 256
EXPERT_BLOCK = 256
COMBINE_TILE = 128
SLOTS_TILE = 512


def _dot(a, b):
    return jnp.dot(a, b, preferred_element_type=F32)


def _dot_nt(a, b):
    return lax.dot_general(a, b, (((1,), (1,)), ((), ())), preferred_element_type=F32)


def _dot_tn(a, b):
    return lax.dot_general(a, b, (((0,), (0,)), ((), ())), preferred_element_type=F32)


def _layer_norm(x):
    xc = x - jnp.mean(x, -1, keepdims=True)
    var = jnp.mean(xc * xc, -1, keepdims=True)
    return xc * lax.rsqrt(var + LN_EPS)


def _silu(x):
    return x * jax.nn.sigmoid(x)


def _softplus(x):
    return jnp.maximum(x, 0.0) + jnp.log1p(jnp.exp(-jnp.abs(x)))


def _ada_kernel(c_ref, w_ref, b_ref, o_ref):
    s = _silu(c_ref[...]).astype(BF16)
    o_ref[...] = _dot(s, w_ref[...].astype(BF16)) + b_ref[...]


def _ada(c, w_ada, b_ada):
    bsz, d = c.shape
    n = w_ada.shape[1]
    tn = 1024
    return pl.pallas_call(
        _ada_kernel,
        grid=(n // tn,),
        in_specs=[pl.BlockSpec((bsz, d), lambda j: (0, 0)),
                  pl.BlockSpec((d, tn), lambda j: (0, j)),
                  pl.BlockSpec((1, tn), lambda j: (0, j))],
        out_specs=pl.BlockSpec((bsz, tn), lambda j: (0, j)),
        out_shape=jax.ShapeDtypeStruct((bsz, n), F32),
        name="ada",
    )(c, w_ada, b_ada.reshape(1, n))


def _split3_bf16(a):
    hi = a.astype(BF16).astype(F32)
    r1 = a - hi
    mid = r1.astype(BF16).astype(F32)
    lo = r1 - mid
    pad = jnp.zeros((a.shape[0], LANES - 3 * N_HEADS), F32)
    return jnp.concatenate([hi, mid, lo, pad], axis=1).astype(BF16)


def _mixer_kernel(x_ref, mod_ref, w_ref, convw_ref, convb_ref, dtb_ref, alog_ref, dskip_ref,
                  normw_ref, glng_ref, glnb_ref, ws_ref, bsx_ref, e64_ref, e128_ref,
                  wpssd_ref, wpgmlp_ref, wout_ref, ln1g_ref, ln1b_ref,
                  o_ref,
                  m_s, xs_s, bm_s, cm_s, dt_s, y_s, state_s, tail_s,
                  *, tile, d_model, d_inner, offs):
    off_z, off_xbc, off_uv, off_ga, off_gb, off_dt = offs
    n_chunks = tile // CHUNK
    gw = HEADS_PER_GROUP * HEAD_DIM

    @pl.when(pl.program_id(1) == 0)
    def _():
        state_s[...] = jnp.zeros_like(state_s)
        tail_s[...] = jnp.zeros_like(tail_s)

    x = x_ref[0]
    mod = mod_ref[0]
    sh1, sc1, g1 = mod[0:1], mod[1:2], mod[2:3]
    m_s[...] = (_layer_norm(x) * (1.0 + sc1) + sh1).astype(BF16)

    row8 = lax.broadcasted_iota(jnp.int32, (SUBLANES, gw), 0)
    n_xblk = d_inner // gw
    n_bblk = N_GROUPS * D_STATE // gw
    for blk in range(n_xblk + 2 * n_bblk):
        c0 = blk * gw
        pre = _dot(m_s[...], w_ref[:, off_xbc + c0:off_xbc + c0 + gw])
        prev_tail = tail_s[:, c0:c0 + gw]
        tail_s[:, c0:c0 + gw] = pre[tile - SUBLANES:tile]
        cw = convw_ref[:, c0:c0 + gw]
        acc = pre * cw[CONV_K - 1:CONV_K] + convb_ref[:, c0:c0 + gw]
        for k in range(1, CONV_K):
            r = pltpu.roll(pre, k, axis=0)
            top = jnp.where(row8 < k, pltpu.roll(prev_tail, k, axis=0), r[0:SUBLANES])
            r = jnp.concatenate([top, r[SUBLANES:]], axis=0)
            acc = acc + r * cw[CONV_K - 1 - k:CONV_K - k]
        act = _silu(acc)
        if blk < n_xblk:
            xs_s[blk] = act
        else:
            dst = bm_s if blk < n_xblk + n_bblk else cm_s
            g0 = ((blk - n_xblk) % n_bblk) * (gw // D_STATE)
            for i in range(gw // D_STATE):
                dst[g0 + i] = act[:, i * D_STATE:(i + 1) * D_STATE].astype(BF16)

    dt_s[...] = _softplus(_dot(m_s[...], w_ref[:, off_dt:off_dt + LANES]) + dtb_ref[...])
    a_neg = -jnp.exp(alog_ref[...])

    rowc = lax.broadcasted_iota(jnp.int32, (CHUNK, LANES), 0)
    causal = (lax.broadcasted_iota(jnp.int32, (CHUNK, CHUNK), 0)
              >= lax.broadcasted_iota(jnp.int32, (CHUNK, CHUNK), 1))
    lane_g = lax.broadcasted_iota(jnp.int32, (CHUNK, gw), 1)

    def chunk_body(c, carry):
        r0 = pl.multiple_of(c * CHUNK, CHUNK)
        rows = pl.ds(r0, CHUNK)
        dt_c = dt_s[rows, :]
        acs = dt_c * a_neg
        sh = 1
        while sh < CHUNK:
            acs = acs + jnp.where(rowc >= sh, pltpu.roll(acs, sh, axis=0), 0.0)
            sh *= 2
        acs3 = _split3_bf16(acs[:, :N_HEADS])
        acs_x128 = _dot(acs3, e128_ref[...])
        acs_x64 = _dot(acs3, e64_ref[...])
        dt_x64 = _dot(_split3_bf16(dt_c[:, :N_HEADS]), e64_ref[...])
        acs_t = acs.T
        last = acs_x64[CHUNK - 1:CHUNK, :]
        grow = jnp.exp(acs_x64)
        to_end = jnp.exp(last - acs_x64)
        chunk_decay = jnp.exp(last)

        for g in range(N_GROUPS):
            cols = slice(g * gw, (g + 1) * gw)
            xs_g = xs_s[g, rows, :]
            bm_g = bm_s[g, rows, :]
            cm_g = cm_s[g, rows, :]
            xdt = xs_g * dt_x64[:, cols]
            cb = _dot_nt(cm_g, bm_g)
            y_g = None
            for r in range(HEADS_PER_GROUP):
                h = g * HEADS_PER_GROUP + r
                seg = acs_x128[:, h * CHUNK:(h + 1) * CHUNK] - acs_t[h:h + 1, :]
                w = (cb * jnp.exp(jnp.where(causal, seg, -jnp.inf))).astype(BF16)
                in_head = (lane_g >= r * HEAD_DIM) & (lane_g < (r + 1) * HEAD_DIM)
                part = _dot(w, jnp.where(in_head, xdt, 0.0).astype(BF16))
                y_g = part if y_g is None else y_g + part
            st = state_s[g]
            y_g = y_g + _dot(cm_g, st.astype(BF16)) * grow[:, cols]
            state_s[g] = chunk_decay[:, cols] * st + _dot_tn(bm_g, (xdt * to_end[:, cols]).astype(BF16))
            y_s[rows, cols] = y_g + dskip_ref[:, cols] * xs_g
        return carry

    lax.fori_loop(0, n_chunks, chunk_body, 0)

    z = _dot(m_s[...], w_ref[:, off_z:off_z + d_inner])
    y = y_s[...] * _silu(z)
    y = y * lax.rsqrt(jnp.mean(y * y, -1, keepdims=True) + RMS_EPS) * normw_ref[...]
    y_a = _dot(y.astype(BF16), wpssd_ref[...])
    merged = jax.nn.sigmoid(_dot(m_s[...], w_ref[:, off_ga:off_ga + d_model])) * y_a

    uv = jax.nn.gelu(_dot(m_s[...], w_ref[:, off_uv:off_uv + 2 * d_model]))
    u = uv[:, :d_model]
    v = _layer_norm(uv[:, d_model:]) * glng_ref[...] + glnb_ref[...]
    v = v.astype(BF16)
    gd = d_model // GMLP_GROUPS
    ws_m = [jnp.where(causal, ws_ref[g], 0.0).astype(BF16) for g in range(GMLP_GROUPS)]
    v_rows = []
    for c in range(n_chunks):
        v_c = v[c * CHUNK:(c + 1) * CHUNK]
        v_rows.append(jnp.concatenate(
            [_dot(ws_m[g], v_c[:, g * gd:(g + 1) * gd]) for g in range(GMLP_GROUPS)], axis=1)
            + bsx_ref[...])
    v_mix = jnp.concatenate(v_rows, axis=0) if n_chunks > 1 else v_rows[0]
    y_b = _dot((u * v_mix).astype(BF16), wpgmlp_ref[...])
    merged = merged + jax.nn.sigmoid(_dot(m_s[...], w_ref[:, off_gb:off_gb + d_model])) * y_b

    mix = _dot(merged.astype(BF16), wout_ref[...])
    o_ref[0] = _layer_norm(DEEPNORM_ALPHA * x + g1 * mix) * ln1g_ref[...] + ln1b_ref[...]


def _const_spec(shape):
    nd = len(shape)
    return pl.BlockSpec(shape, lambda b, j: (0,) * nd, pipeline_mode=pl.Buffered(1))


def _mixer(x, mod1, w_in, conv_w, conv_b, dt_bias, a_log, d_skip, ssd_norm_w,
           gmlp_ln_g, gmlp_ln_b, gmlp_ws, gmlp_bs, w_proj_ssd, w_proj_gmlp, w_out, ln1_g, ln1_b):
    bsz, seq, d = x.shape
    d_inner = N_HEADS * HEAD_DIM
    gn = N_GROUPS * D_STATE
    d_xbc = d_inner + 2 * gn
    tile = min(MIX_TILE, seq)
    assert seq % tile == 0 and tile % CHUNK == 0 and d_inner == 2 * d

    s0, s1, s2, s3, s4 = (d_inner, d_inner + d_xbc, d_inner + d_xbc + N_HEADS,
                          d_inner + d_xbc + N_HEADS + 2 * d, d_inner + d_xbc + N_HEADS + 3 * d)
    w_dt = jnp.pad(w_in[:, s1:s2], ((0, 0), (0, LANES - N_HEADS)))
    w_all = jnp.concatenate([w_in[:, :s1], w_in[:, s2:], w_dt], axis=1).astype(BF16)
    off_z, off_xbc = 0, d_inner
    off_uv = s1
    off_ga = off_uv + 2 * d
    off_gb = off_ga + d
    off_dt = off_gb + d
    offs = (off_z, off_xbc, off_uv, off_ga, off_gb, off_dt)

    pad_h = lambda v: jnp.pad(v.reshape(1, N_HEADS), ((0, 0), (0, LANES - N_HEADS)))
    head_of_col64 = jnp.arange(d_inner) // HEAD_DIM
    head_of_col128 = jnp.arange(N_HEADS * CHUNK) // CHUNK
    piece_head = jnp.where(jnp.arange(LANES) < 3 * N_HEADS, jnp.arange(LANES) % N_HEADS, -1)
    e64 = (piece_head[:, None] == head_of_col64[None, :]).astype(BF16)
    e128 = (piece_head[:, None] == head_of_col128[None, :]).astype(BF16)
    dskip_x = jnp.broadcast_to(d_skip[:, None], (N_HEADS, HEAD_DIM)).reshape(1, d_inner)
    bs_x = jnp.broadcast_to(gmlp_bs.T[:, :, None], (CHUNK, GMLP_GROUPS, d // GMLP_GROUPS)).reshape(CHUNK, d)

    operands = [
        w_all, conv_w, conv_b.reshape(1, d_xbc), pad_h(dt_bias), pad_h(a_log), dskip_x,
        ssd_norm_w.reshape(1, d_inner), gmlp_ln_g.reshape(1, d), gmlp_ln_b.reshape(1, d),
        gmlp_ws, bs_x, e64, e128,
        w_proj_ssd.astype(BF16), w_proj_gmlp.astype(BF16), w_out.astype(BF16),
        ln1_g.reshape(1, d), ln1_b.reshape(1, d),
    ]
    gw = HEADS_PER_GROUP * HEAD_DIM
    kern = functools.partial(_mixer_kernel, tile=tile, d_model=d, d_inner=d_inner, offs=offs)
    return pl.pallas_call(
        kern,
        grid=(bsz, seq // tile),
        in_specs=[pl.BlockSpec((1, tile, d), lambda b, j: (b, j, 0)),
                  pl.BlockSpec((1, SUBLANES, d), lambda b, j: (b, 0, 0))]
                 + [_const_spec(op.shape) for op in operands],
        out_specs=pl.BlockSpec((1, tile, d), lambda b, j: (b, j, 0)),
        out_shape=jax.ShapeDtypeStruct((bsz, seq, d), F32),
        scratch_shapes=[
            pltpu.VMEM((tile, d), BF16),
            pltpu.VMEM((N_GROUPS, tile, gw), F32),
            pltpu.VMEM((N_GROUPS, tile, D_STATE), BF16),
            pltpu.VMEM((N_GROUPS, tile, D_STATE), BF16),
            pltpu.VMEM((tile, LANES), F32),
            pltpu.VMEM((tile, d_inner), F32),
            pltpu.VMEM((N_GROUPS, D_STATE, gw), F32),
            pltpu.VMEM((SUBLANES, d_xbc), F32),
        ],
        compiler_params=pltpu.CompilerParams(
            dimension_semantics=("arbitrary", "arbitrary"),
            vmem_limit_bytes=56 * 1024 * 1024),
        name="mixer",
    )(x, mod1, *operands)


def _route_kernel(h_ref, mod_ref, wr_ref, bias_ref, topi_ref, topw_ref, rank_ref, cnt_ref, cnt_s,
                  *, tile):
    @pl.when(pl.program_id(0) == 0)
    def _():
        cnt_s[...] = jnp.zeros_like(cnt_s)

    mod = mod_ref[0]
    m2 = _layer_norm(h_ref[...]) * (1.0 + mod[1:2]) + mod[0:1]
    scores = jax.nn.sigmoid(_dot_nt(wr_ref[...], m2.astype(BF16)))
    choice = scores + bias_ref[...]

    neg = -jnp.inf
    c3 = choice.reshape(N_EXPERT_GROUPS, EXPERTS_PER_GROUP, tile)
    i3 = lax.broadcasted_iota(jnp.int32, c3.shape, 1).astype(F32)
    m1 = jnp.max(c3, axis=1, keepdims=True)
    i1 = jnp.min(jnp.where(c3 == m1, i3, float(EXPERTS_PER_GROUP)), axis=1, keepdims=True)
    second = jnp.max(jnp.where(i3 == i1, neg, c3), axis=1)
    gs = m1[:, 0, :] + second

    gi = lax.broadcasted_iota(jnp.int32, gs.shape, 0).astype(F32)
    gsel = jnp.zeros(gs.shape, F32)
    for _ in range(TOPK_GROUPS):
        mx = jnp.max(gs, axis=0, keepdims=True)
        ix = jnp.min(jnp.where(gs == mx, gi, float(N_EXPERT_GROUPS)), axis=0, keepdims=True)
        hit = gi == ix
        gsel = jnp.where(hit, 1.0, gsel)
        gs = jnp.where(hit, neg, gs)
    emask = jnp.broadcast_to(gsel[:, None, :], c3.shape).reshape(N_EXPERTS, tile) > 0.0
    masked = jnp.where(emask, choice, neg)

    ei = lax.broadcasted_iota(jnp.int32, masked.shape, 0).astype(F32)
    hits, idx_rows, w_rows = [], [], []
    for _ in range(TOP_K):
        mx = jnp.max(masked, axis=0, keepdims=True)
        ix = jnp.min(jnp.where(masked == mx, ei, float(N_EXPERTS)), axis=0, keepdims=True)
        hit = ei == ix
        hits.append(hit)
        idx_rows.append(ix)
        w_rows.append(jnp.sum(jnp.where(hit, scores, 0.0), axis=0, keepdims=True))
        masked = jnp.where(hit, neg, masked)
    w_all = jnp.concatenate(w_rows, axis=0)
    topw_ref[...] = w_all / (jnp.sum(w_all, axis=0, keepdims=True) + 1e-20) * ROUTED_SCALE
    topi_ref[...] = jnp.concatenate(idx_rows, axis=0).astype(jnp.int32)

    assign = jnp.zeros(masked.shape, F32)
    for hit in hits:
        assign = jnp.where(hit, 1.0, assign)
    assign_b = assign.astype(BF16)
    t0 = lax.broadcasted_iota(jnp.int32, (tile, tile), 0)
    t1 = lax.broadcasted_iota(jnp.int32, (tile, tile), 1)
    before = jnp.where(t0 < t1, 1.0, 0.0).astype(BF16)
    base = cnt_s[...]
    pos = _dot(assign_b, before) + jnp.concatenate([base] * (tile // LANES), axis=1)
    rank_ref[...] = jnp.concatenate(
        [jnp.sum(jnp.where(hit, pos, 0.0), axis=0, keepdims=True) for hit in hits],
        axis=0).astype(jnp.int32)
    cnt_s[...] = base + _dot(assign_b, jnp.ones((tile, LANES), BF16))
    cnt_ref[...] = cnt_s[...]


def _route(h1, mod2, w_router, router_bias, tokens_per_batch):
    t, d = h1.shape
    tile = min(ROUTE_TILE, tokens_per_batch)
    assert tokens_per_batch % tile == 0 and tile % LANES == 0
    per_b = tokens_per_batch // tile
    wr_t = w_router.T.astype(BF16)
    bias_x = jnp.broadcast_to(router_bias.reshape(N_EXPERTS, 1), (N_EXPERTS, tile)).astype(F32)
    kern = functools.partial(_route_kernel, tile=tile)
    row_spec = pl.BlockSpec((TOP_K, tile), lambda i: (0, i))
    return pl.pallas_call(
        kern,
        grid=(t // tile,),
        in_specs=[pl.BlockSpec((tile, d), lambda i: (i, 0)),
                  pl.BlockSpec((1, SUBLANES, d), lambda i: (i // per_b, 0, 0)),
                  pl.BlockSpec((N_EXPERTS, d), lambda i: (0, 0)),
                  pl.BlockSpec((N_EXPERTS, tile), lambda i: (0, 0))],
        out_specs=[row_spec, row_spec, row_spec,
                   pl.BlockSpec((N_EXPERTS, LANES), lambda i: (0, 0))],
        out_shape=[jax.ShapeDtypeStruct((TOP_K, t), jnp.int32),
                   jax.ShapeDtypeStruct((TOP_K, t), F32),
                   jax.ShapeDtypeStruct((TOP_K, t), jnp.int32),
                   jax.ShapeDtypeStruct((N_EXPERTS, LANES), F32)],
        scratch_shapes=[pltpu.VMEM((N_EXPERTS, LANES), F32)],
        compiler_params=pltpu.CompilerParams(dimension_semantics=("arbitrary",)),
        name="route",
    )(h1, mod2, wr_t, bias_x)


def _slots_kernel(topi_ref, rank_ref, start_ref, dest_ref, *, tile):
    ei = lax.broadcasted_iota(jnp.int32, (N_EXPERTS, tile), 0)
    start = jnp.concatenate([start_ref[...]] * (tile // LANES), axis=1)
    topi = topi_ref[...]
    rows = [jnp.sum(jnp.where(ei == topi[k:k + 1, :], start, 0.0), axis=0, keepdims=True)
            for k in range(TOP_K)]
    dest_ref[...] = jnp.concatenate(rows, axis=0).astype(jnp.int32) + rank_ref[...]


def _slots(topi_t, rank_t, pad_start):
    t = topi_t.shape[1]
    tile = min(SLOTS_TILE, t)
    start_x = jnp.broadcast_to(pad_start.astype(F32).reshape(N_EXPERTS, 1), (N_EXPERTS, LANES))
    row_spec = pl.BlockSpec((TOP_K, tile), lambda i: (0, i))
    return pl.pallas_call(
        functools.partial(_slots_kernel, tile=tile),
        grid=(t // tile,),
        in_specs=[row_spec, row_spec, pl.BlockSpec((N_EXPERTS, LANES), lambda i: (0, 0))],
        out_specs=row_spec,
        out_shape=jax.ShapeDtypeStruct((TOP_K, t), jnp.int32),
        name="slots",
    )(topi_t, rank_t, start_x)


def _row_copy_wait(src_rows_ref, dst_rows_ref, sem):
    pltpu.make_async_copy(src_rows_ref, dst_rows_ref, sem).wait()


def _pack_bf16_pairs(x):
    n = x.shape[1] // 2
    bits = lax.bitcast_convert_type(x.astype(BF16).astype(F32), jnp.uint32)
    return bits[:, n:] | (bits[:, :n] >> 16)


def _unpack_bf16_pairs(p):
    lo = lax.bitcast_convert_type(p << 16, F32)
    hi = lax.bitcast_convert_type(p & jnp.uint32(0xFFFF0000), F32)
    return jnp.concatenate([lo, hi], axis=1).astype(BF16)


def _scatter_kernel(dest_ref, h_ref, mod_ref, zeros_ref, out_ref, m2_s, sem, *, tile):
    del zeros_ref
    mod = mod_ref[0]
    m2_s[...] = _pack_bf16_pairs(_layer_norm(h_ref[...]) * (1.0 + mod[1:2]) + mod[0:1])

    def issue(t, carry):
        for k in range(TOP_K):
            d = dest_ref[t * TOP_K + k]
            pltpu.make_async_copy(m2_s.at[pl.ds(t, 1), :], out_ref.at[pl.ds(d, 1), :], sem).start()
        return carry

    lax.fori_loop(0, tile, issue, 0)
    for _ in range(TOP_K):
        _row_copy_wait(m2_s, out_ref.at[pl.ds(0, tile), :], sem)


def _scatter(h1, mod2, dest_flat, n_slots, tokens_per_batch):
    t, d = h1.shape
    tile = min(SCATTER_TILE, tokens_per_batch)
    per_b = tokens_per_batch // tile
    kern = functools.partial(_scatter_kernel, tile=tile)
    return pl.pallas_call(
        kern,
        grid=(t // tile,),
        in_specs=[pl.BlockSpec((tile * TOP_K,), lambda i: (i,), memory_space=pltpu.SMEM),
                  pl.BlockSpec((tile, d), lambda i: (i, 0)),
                  pl.BlockSpec((1, SUBLANES, d), lambda i: (i // per_b, 0, 0)),
                  pl.BlockSpec(memory_space=pl.ANY)],
        out_specs=pl.BlockSpec(memory_space=pl.ANY),
        out_shape=jax.ShapeDtypeStruct((n_slots, d // 2), jnp.uint32),
        scratch_shapes=[pltpu.VMEM((tile, d // 2), jnp.uint32), pltpu.SemaphoreType.DMA(())],
        input_output_aliases={3: 0},
        compiler_params=pltpu.CompilerParams(dimension_semantics=("arbitrary",),
                                             has_side_effects=True),
        name="scatter",
    )(dest_flat, h1, mod2, jnp.zeros((n_slots, d // 2), jnp.uint32))


def _expert_kernel(be_ref, bx_ref, nv_ref, x_ref, wg_ref, wu_ref, wd_ref, o_ref):
    del be_ref, bx_ref
    nv = nv_ref[pl.program_id(0)]

    @pl.when(nv > 0)
    def _():
        xb = _unpack_bf16_pairs(x_ref[...])
        gate = _dot(xb, wg_ref[0].astype(BF16))
        up = _dot(xb, wu_ref[0].astype(BF16))
        o_ref[...] = _dot((_silu(gate) * up).astype(BF16), wd_ref[0].astype(BF16))

    @pl.when(nv == 0)
    def _():
        o_ref[...] = jnp.zeros_like(o_ref)


def _experts(xs_sorted, block_e, block_x, block_nv, w_e_gate, w_e_up, w_e_down):
    n_slots = xs_sorted.shape[0]
    d, ff = w_e_gate.shape[-2:]
    nb = n_slots // EXPERT_BLOCK
    grid_spec = pltpu.PrefetchScalarGridSpec(
        num_scalar_prefetch=3,
        grid=(nb,),
        in_specs=[pl.BlockSpec((EXPERT_BLOCK, d // 2), lambda b, be, bx, nv: (bx[b], 0)),
                  pl.BlockSpec((1, d, ff), lambda b, be, bx, nv: (be[b], 0, 0)),
                  pl.BlockSpec((1, d, ff), lambda b, be, bx, nv: (be[b], 0, 0)),
                  pl.BlockSpec((1, ff, d), lambda b, be, bx, nv: (be[b], 0, 0))],
        out_specs=pl.BlockSpec((EXPERT_BLOCK, d), lambda b, be, bx, nv: (b, 0)),
    )
    return pl.pallas_call(
        _expert_kernel,
        grid_spec=grid_spec,
        out_shape=jax.ShapeDtypeStruct((n_slots, d), F32),
        compiler_params=pltpu.CompilerParams(dimension_semantics=("arbitrary",)),
        name="experts",
    )(block_e, block_x, block_nv, xs_sorted, w_e_gate, w_e_up, w_e_down)


def _combine_kernel(dest_ref, h_ref, mod_ref, w_ref, y_ref, wsg_ref, wsu_ref, wsd_ref,
                    ln2g_ref, ln2b_ref, o_ref, ybuf_s, sem, *, tile):
    def issue(t, carry):
        for k in range(TOP_K):
            d = dest_ref[t * TOP_K + k]
            pltpu.make_async_copy(y_ref.at[pl.ds(d, 1), :], ybuf_s.at[k, pl.ds(t, 1), :], sem).start()
        return carry

    lax.fori_loop(0, tile, issue, 0)

    h = h_ref[...]
    mod = mod_ref[0]
    m2 = (_layer_norm(h) * (1.0 + mod[1:2]) + mod[0:1]).astype(BF16)
    acc = _dot((_silu(_dot(m2, wsg_ref[...])) * _dot(m2, wsu_ref[...])).astype(BF16), wsd_ref[...])

    for k in range(TOP_K):
        _row_copy_wait(y_ref.at[pl.ds(0, tile), :], ybuf_s.at[k], sem)
    w = w_ref[...]
    for k in range(TOP_K):
        acc = acc + ybuf_s[k] * w[:, k:k + 1]
    o_ref[...] = _layer_norm(DEEPNORM_ALPHA * h + mod[2:3] * acc) * ln2g_ref[...] + ln2b_ref[...]


def _combine(h1, mod2, dest_flat, topw, y_sorted, w_sh_gate, w_sh_up, w_sh_down, ln2_g, ln2_b,
             tokens_per_batch):
    t, d = h1.shape
    ff = w_sh_gate.shape[-1]
    tile = min(COMBINE_TILE, tokens_per_batch)
    per_b = tokens_per_batch // tile
    kern = functools.partial(_combine_kernel, tile=tile)
    const = lambda shape: pl.BlockSpec(shape, lambda i: (0,) * len(shape))
    return pl.pallas_call(
        kern,
        grid=(t // tile,),
        in_specs=[pl.BlockSpec((tile * TOP_K,), lambda i: (i,), memory_space=pltpu.SMEM),
                  pl.BlockSpec((tile, d), lambda i: (i, 0)),
                  pl.BlockSpec((1, SUBLANES, d), lambda i: (i // per_b, 0, 0)),
                  pl.BlockSpec((tile, TOP_K), lambda i: (i, 0)),
                  pl.BlockSpec(memory_space=pl.ANY),
                  const((d, ff)), const((d, ff)), const((ff, d)), const((1, d)), const((1, d))],
        out_specs=pl.BlockSpec((tile, d), lambda i: (i, 0)),
        out_shape=jax.ShapeDtypeStruct((t, d), F32),
        scratch_shapes=[pltpu.VMEM((TOP_K, tile, d), F32), pltpu.SemaphoreType.DMA(())],
        compiler_params=pltpu.CompilerParams(dimension_semantics=("arbitrary",)),
        name="combine",
    )(dest_flat, h1, mod2, topw, y_sorted, w_sh_gate.astype(BF16), w_sh_up.astype(BF16),
      w_sh_down.astype(BF16), ln2_g.reshape(1, d), ln2_b.reshape(1, d))


def _moe(h1, mod2, w_router, router_bias, w_e_gate, w_e_up, w_e_down,
         w_sh_gate, w_sh_up, w_sh_down, ln2_g, ln2_b, tokens_per_batch):
    t, d = h1.shape
    topi_t, topw_t, rank_t, cnt = _route(h1, mod2, w_router, router_bias, tokens_per_batch)

    blk = EXPERT_BLOCK
    nb = -(-(t * TOP_K) // blk) + N_EXPERTS
    counts = cnt[:, 0].astype(jnp.int32)
    padded = (counts + blk - 1) // blk * blk
    pad_end = jnp.cumsum(padded)
    pad_start = pad_end - padded
    dest_flat = _slots(topi_t, rank_t, pad_start).T.reshape(t * TOP_K)
    block_id = jnp.arange(nb, dtype=jnp.int32)
    block_lo = block_id * blk
    block_e = jnp.minimum(jnp.sum(pad_end[None, :] <= block_lo[:, None], axis=1),
                          N_EXPERTS - 1).astype(jnp.int32)
    is_e = block_e[:, None] == jnp.arange(N_EXPERTS, dtype=jnp.int32)[None, :]
    count_b = jnp.sum(jnp.where(is_e, counts[None, :], 0), axis=1)
    start_b = jnp.sum(jnp.where(is_e, pad_start[None, :], 0), axis=1)
    block_nv = jnp.clip(count_b - (block_lo - start_b), 0, blk).astype(jnp.int32)
    block_x = jnp.minimum(block_id, pad_end[-1] // blk - 1).astype(jnp.int32)

    xs_sorted = _scatter(h1, mod2, dest_flat, nb * blk, tokens_per_batch)
    y_sorted = _experts(xs_sorted, block_e, block_x, block_nv, w_e_gate, w_e_up, w_e_down)
    return _combine(h1, mod2, dest_flat, topw_t.T, y_sorted, w_sh_gate, w_sh_up, w_sh_down,
                    ln2_g, ln2_b, tokens_per_batch)


def _pad_rows(parts, d):
    rows = jnp.stack(parts, axis=1)
    return jnp.pad(rows, ((0, 0), (0, SUBLANES - rows.shape[1]), (0, 0)))


def kernel(x, c, w_ada, b_ada, w_in, conv_w, conv_b, dt_bias, a_log, d_skip, ssd_norm_w, gmlp_ln_g, gmlp_ln_b, gmlp_ws, gmlp_bs, w_proj_ssd, w_proj_gmlp, w_out, ln1_g, ln1_b, w_router, router_bias, w_e_gate, w_e_up, w_e_down, w_sh_gate, w_sh_up, w_sh_down, ln2_g, ln2_b):
    bsz, seq, d = x.shape
    h = x
    for i in range(w_ada.shape[0]):
        ada = _ada(c, w_ada[i], b_ada[i])
        sh1, sc1, g1, sh2, sc2, g2 = jnp.split(ada, 6, axis=-1)
        h = _mixer(h, _pad_rows([sh1, sc1, g1], d), w_in[i], conv_w[i], conv_b[i], dt_bias[i],
                   a_log[i], d_skip[i], ssd_norm_w[i], gmlp_ln_g[i], gmlp_ln_b[i], gmlp_ws[i],
                   gmlp_bs[i], w_proj_ssd[i], w_proj_gmlp[i], w_out[i], ln1_g[i], ln1_b[i])
        h = _moe(h.reshape(bsz * seq, d), _pad_rows([sh2, sc2, g2], d), w_router[i],
                 router_bias[i], w_e_gate[i], w_e_up[i], w_e_down[i], w_sh_gate[i], w_sh_up[i],
                 w_sh_down[i], ln2_g[i], ln2_b[i], seq).reshape(bsz, seq, d)
    return h
```

```python
import functools

import jax
import jax.numpy as jnp
from jax import lax
from jax.experimental import pallas as pl
from jax.experimental.pallas import tpu as pltpu

F32 = jnp.float32
BF16 = jnp.bfloat16

HEAD_DIM = 64
N_GROUPS = 8
HEADS_PER_GROUP = 4
N_HEADS = N_GROUPS * HEADS_PER_GROUP
D_STATE = 128
CONV_K = 4
CHUNK = 128
GMLP_GROUPS = 8
N_EXPERTS = 256
TOP_K = 8
N_EXPERT_GROUPS = 8
EXPERTS_PER_GROUP = N_EXPERTS // N_EXPERT_GROUPS
TOPK_GROUPS = 4
ROUTED_SCALE = 2.5
DEPTH = 1
DEEPNORM_ALPHA = (2 * DEPTH) ** 0.25
LN_EPS = 1e-5
RMS_EPS = 1e-5

LANES = 128
SUBLANES = 8
VMEM_BYTES_V7X = 64 * 1024 * 1024

MIX_TILE = 256
ROUTE_TILE = 256
SCATTER_TILE = 256
EXPERT_BLOCK = 256
COMBINE_TILE = 128
SLOTS_TILE = 512


def _dot(a, b):
    return jnp.dot(a, b, preferred_element_type=F32)


def _dot_nt(a, b):
    return lax.dot_general(a, b, (((1,), (1,)), ((), ())), preferred_element_type=F32)


def _dot_tn(a, b):
    return lax.dot_general(a, b, (((0,), (0,)), ((), ())), preferred_element_type=F32)


def _layer_norm(x):
    xc = x - jnp.mean(x, -1, keepdims=True)
    var = jnp.mean(xc * xc, -1, keepdims=True)
    return xc * lax.rsqrt(var + LN_EPS)


def _silu(x):
    return x * jax.nn.sigmoid(x)


def _softplus(x):
    return jnp.maximum(x, 0.0) + jnp.log1p(jnp.exp(-jnp.abs(x)))


def _ada_kernel(c_ref, w_ref, b_ref, o_ref):
    s = _silu(c_ref[...]).astype(BF16)
    o_ref[...] = _dot(s, w_ref[...].astype(BF16)) + b_ref[...]


def _ada(c, w_ada, b_ada):
    bsz, d = c.shape
    n = w_ada.shape[1]
    tn = 1024
    return pl.pallas_call(
        _ada_kernel,
        grid=(n // tn,),
        in_specs=[pl.BlockSpec((bsz, d), lambda j: (0, 0)),
                  pl.BlockSpec((d, tn), lambda j: (0, j)),
                  pl.BlockSpec((1, tn), lambda j: (0, j))],
        out_specs=pl.BlockSpec((bsz, tn), lambda j: (0, j)),
        out_shape=jax.ShapeDtypeStruct((bsz, n), F32),
        name="ada",
    )(c, w_ada, b_ada.reshape(1, n))


def _split3_bf16(a):
    hi = a.astype(BF16).astype(F32)
    r1 = a - hi
    mid = r1.astype(BF16).astype(F32)
    lo = r1 - mid
    pad = jnp.zeros((a.shape[0], LANES - 3 * N_HEADS), F32)
    return jnp.concatenate([hi, mid, lo, pad], axis=1).astype(BF16)


def _mixer_kernel(x_ref, mod_ref, w_ref, convw_ref, convb_ref, dtb_ref, alog_ref, dskip_ref,
                  normw_ref, glng_ref, glnb_ref, ws_ref, bsx_ref, e64_ref, e128_ref,
                  wpssd_ref, wpgmlp_ref, wout_ref, ln1g_ref, ln1b_ref,
                  o_ref,
                  m_s, xs_s, bm_s, cm_s, dt_s, y_s, state_s, tail_s,
                  *, tile, d_model, d_inner, offs):
    off_z, off_xbc, off_uv, off_ga, off_gb, off_dt = offs
    n_chunks = tile // CHUNK
    gw = HEADS_PER_GROUP * HEAD_DIM

    @pl.when(pl.program_id(1) == 0)
    def _():
        state_s[...] = jnp.zeros_like(state_s)
        tail_s[...] = jnp.zeros_like(tail_s)

    x = x_ref[0]
    mod = mod_ref[0]
    sh1, sc1, g1 = mod[0:1], mod[1:2], mod[2:3]
    m_s[...] = (_layer_norm(x) * (1.0 + sc1) + sh1).astype(BF16)

    row8 = lax.broadcasted_iota(jnp.int32, (SUBLANES, gw), 0)
    n_xblk = d_inner // gw
    n_bblk = N_GROUPS * D_STATE // gw
    for blk in range(n_xblk + 2 * n_bblk):
        c0 = blk * gw
        pre = _dot(m_s[...], w_ref[:, off_xbc + c0:off_xbc + c0 + gw])
        prev_tail = tail_s[:, c0:c0 + gw]
        tail_s[:, c0:c0 + gw] = pre[tile - SUBLANES:tile]
        cw = convw_ref[:, c0:c0 + gw]
        acc = pre * cw[CONV_K - 1:CONV_K] + convb_ref[:, c0:c0 + gw]
        for k in range(1, CONV_K):
            r = pltpu.roll(pre, k, axis=0)
            top = jnp.where(row8 < k, pltpu.roll(prev_tail, k, axis=0), r[0:SUBLANES])
            r = jnp.concatenate([top, r[SUBLANES:]], axis=0)
            acc = acc + r * cw[CONV_K - 1 - k:CONV_K - k]
        act = _silu(acc)
        if blk < n_xblk:
            xs_s[blk] = act
        else:
            dst = bm_s if blk < n_xblk + n_bblk else cm_s
            g0 = ((blk - n_xblk) % n_bblk) * (gw // D_STATE)
            for i in range(gw // D_STATE):
                dst[g0 + i] = act[:, i * D_STATE:(i + 1) * D_STATE].astype(BF16)

    dt_s[...] = _softplus(_dot(m_s[...], w_ref[:, off_dt:off_dt + LANES]) + dtb_ref[...])
    a_neg = -jnp.exp(alog_ref[...])

    rowc = lax.broadcasted_iota(jnp.int32, (CHUNK, LANES), 0)
    causal = (lax.broadcasted_iota(jnp.int32, (CHUNK, CHUNK), 0)
              >= lax.broadcasted_iota(jnp.int32, (CHUNK, CHUNK), 1))
    lane_g = lax.broadcasted_iota(jnp.int32, (CHUNK, gw), 1)

    def chunk_body(c, carry):
        r0 = pl.multiple_of(c * CHUNK, CHUNK)
        rows = pl.ds(r0, CHUNK)
        dt_c = dt_s[rows, :]
        acs = dt_c * a_neg
        sh = 1
        while sh < CHUNK:
            acs = acs + jnp.where(rowc >= sh, pltpu.roll(acs, sh, axis=0), 0.0)
            sh *= 2
        acs3 = _split3_bf16(acs[:, :N_HEADS])
        acs_x128 = _dot(acs3, e128_ref[...])
        acs_x64 = _dot(acs3, e64_ref[...])
        dt_x64 = _dot(_split3_bf16(dt_c[:, :N_HEADS]), e64_ref[...])
        acs_t = acs.T
        last = acs_x64[CHUNK - 1:CHUNK, :]
        grow = jnp.exp(acs_x64)
        to_end = jnp.exp(last - acs_x64)
        chunk_decay = jnp.exp(last)

        for g in range(N_GROUPS):
            cols = slice(g * gw, (g + 1) * gw)
            xs_g = xs_s[g, rows, :]
            bm_g = bm_s[g, rows, :]
            cm_g = cm_s[g, rows, :]
            xdt = xs_g * dt_x64[:, cols]
            cb = _dot_nt(cm_g, bm_g)
            y_g = None
            for r in range(HEADS_PER_GROUP):
                h = g * HEADS_PER_GROUP + r
                seg = acs_x128[:, h * CHUNK:(h + 1) * CHUNK] - acs_t[h:h + 1, :]
                w = (cb * jnp.exp(jnp.where(causal, seg, -jnp.inf))).astype(BF16)
                in_head = (lane_g >= r * HEAD_DIM) & (lane_g < (r + 1) * HEAD_DIM)
                part = _dot(w, jnp.where(in_head, xdt, 0.0).astype(BF16))
                y_g = part if y_g is None else y_g + part
            st = state_s[g]
            y_g = y_g + _dot(cm_g, st.astype(BF16)) * grow[:, cols]
            state_s[g] = chunk_decay[:, cols] * st + _dot_tn(bm_g, (xdt * to_end[:, cols]).astype(BF16))
            y_s[rows, cols] = y_g + dskip_ref[:, cols] * xs_g
        return carry

    lax.fori_loop(0, n_chunks, chunk_body, 0)

    z = _dot(m_s[...], w_ref[:, off_z:off_z + d_inner])
    y = y_s[...] * _silu(z)
    y = y * lax.rsqrt(jnp.mean(y * y, -1, keepdims=True) + RMS_EPS) * normw_ref[...]
    y_a = _dot(y.astype(BF16), wpssd_ref[...])
    merged = jax.nn.sigmoid(_dot(m_s[...], w_ref[:, off_ga:off_ga + d_model])) * y_a

    uv = jax.nn.gelu(_dot(m_s[...], w_ref[:, off_uv:off_uv + 2 * d_model]))
    u = uv[:, :d_model]
    v = _layer_norm(uv[:, d_model:]) * glng_ref[...] + glnb_ref[...]
    v = v.astype(BF16)
    gd = d_model // GMLP_GROUPS
    ws_m = [jnp.where(causal, ws_ref[g], 0.0).astype(BF16) for g in range(GMLP_GROUPS)]
    v_rows = []
    for c in range(n_chunks):
        v_c = v[c * CHUNK:(c + 1) * CHUNK]
        v_rows.append(jnp.concatenate(
            [_dot(ws_m[g], v_c[:, g * gd:(g + 1) * gd]) for g in range(GMLP_GROUPS)], axis=1)
            + bsx_ref[...])
    v_mix = jnp.concatenate(v_rows, axis=0) if n_chunks > 1 else v_rows[0]
    y_b = _dot((u * v_mix).astype(BF16), wpgmlp_ref[...])
    merged = merged + jax.nn.sigmoid(_dot(m_s[...], w_ref[:, off_gb:off_gb + d_model])) * y_b

    mix = _dot(merged.astype(BF16), wout_ref[...])
    o_ref[0] = _layer_norm(DEEPNORM_ALPHA * x + g1 * mix) * ln1g_ref[...] + ln1b_ref[...]


def _const_spec(shape):
    nd = len(shape)
    return pl.BlockSpec(shape, lambda b, j: (0,) * nd, pipeline_mode=pl.Buffered(1))


def _mixer(x, mod1, w_in, conv_w, conv_b, dt_bias, a_log, d_skip, ssd_norm_w,
           gmlp_ln_g, gmlp_ln_b, gmlp_ws, gmlp_bs, w_proj_ssd, w_proj_gmlp, w_out, ln1_g, ln1_b):
    bsz, seq, d = x.shape
    d_inner = N_HEADS * HEAD_DIM
    gn = N_GROUPS * D_STATE
    d_xbc = d_inner + 2 * gn
    tile = min(MIX_TILE, seq)
    assert seq % tile == 0 and tile % CHUNK == 0 and d_inner == 2 * d

    s0, s1, s2, s3, s4 = (d_inner, d_inner + d_xbc, d_inner + d_xbc + N_HEADS,
                          d_inner + d_xbc + N_HEADS + 2 * d, d_inner + d_xbc + N_HEADS + 3 * d)
    w_dt = jnp.pad(w_in[:, s1:s2], ((0, 0), (0, LANES - N_HEADS)))
    w_all = jnp.concatenate([w_in[:, :s1], w_in[:, s2:], w_dt], axis=1).astype(BF16)
    off_z, off_xbc = 0, d_inner
    off_uv = s1
    off_ga = off_uv + 2 * d
    off_gb = off_ga + d
    off_dt = off_gb + d
    offs = (off_z, off_xbc, off_uv, off_ga, off_gb, off_dt)

    pad_h = lambda v: jnp.pad(v.reshape(1, N_HEADS), ((0, 0), (0, LANES - N_HEADS)))
    head_of_col64 = jnp.arange(d_inner) // HEAD_DIM
    head_of_col128 = jnp.arange(N_HEADS * CHUNK) // CHUNK
    piece_head = jnp.where(jnp.arange(LANES) < 3 * N_HEADS, jnp.arange(LANES) % N_HEADS, -1)
    e64 = (piece_head[:, None] == head_of_col64[None, :]).astype(BF16)
    e128 = (piece_head[:, None] == head_of_col128[None, :]).astype(BF16)
    dskip_x = jnp.broadcast_to(d_skip[:, None], (N_HEADS, HEAD_DIM)).reshape(1, d_inner)
    bs_x = jnp.broadcast_to(gmlp_bs.T[:, :, None], (CHUNK, GMLP_GROUPS, d // GMLP_GROUPS)).reshape(CHUNK, d)

    operands = [
        w_all, conv_w, conv_b.reshape(1, d_xbc), pad_h(dt_bias), pad_h(a_log), dskip_x,
        ssd_norm_w.reshape(1, d_inner), gmlp_ln_g.reshape(1, d), gmlp_ln_b.reshape(1, d),
        gmlp_ws, bs_x, e64, e128,
        w_proj_ssd.astype(BF16), w_proj_gmlp.astype(BF16), w_out.astype(BF16),
        ln1_g.reshape(1, d), ln1_b.reshape(1, d),
    ]
    gw = HEADS_PER_GROUP * HEAD_DIM
    kern = functools.partial(_mixer_kernel, tile=tile, d_model=d, d_inner=d_inner, offs=offs)
    return pl.pallas_call(
        kern,
        grid=(bsz, seq // tile),
        in_specs=[pl.BlockSpec((1, tile, d), lambda b, j: (b, j, 0)),
                  pl.BlockSpec((1, SUBLANES, d), lambda b, j: (b, 0, 0))]
                 + [_const_spec(op.shape) for op in operands],
        out_specs=pl.BlockSpec((1, tile, d), lambda b, j: (b, j, 0)),
        out_shape=jax.ShapeDtypeStruct((bsz, seq, d), F32),
        scratch_shapes=[
            pltpu.VMEM((tile, d), BF16),
            pltpu.VMEM((N_GROUPS, tile, gw), F32),
            pltpu.VMEM((N_GROUPS, tile, D_STATE), BF16),
            pltpu.VMEM((N_GROUPS, tile, D_STATE), BF16),
            pltpu.VMEM((tile, LANES), F32),
            pltpu.VMEM((tile, d_inner), F32),
            pltpu.VMEM((N_GROUPS, D_STATE, gw), F32),
            pltpu.VMEM((SUBLANES, d_xbc), F32),
        ],
        compiler_params=pltpu.CompilerParams(
            dimension_semantics=("arbitrary", "arbitrary"),
            vmem_limit_bytes=56 * 1024 * 1024),
        name="mixer",
    )(x, mod1, *operands)


def _route_kernel(h_ref, mod_ref, wr_ref, bias_ref, topi_ref, topw_ref, rank_ref, cnt_ref, cnt_s,
                  *, tile):
    @pl.when(pl.program_id(0) == 0)
    def _():
        cnt_s[...] = jnp.zeros_like(cnt_s)

    mod = mod_ref[0]
    m2 = _layer_norm(h_ref[...]) * (1.0 + mod[1:2]) + mod[0:1]
    scores = jax.nn.sigmoid(_dot_nt(wr_ref[...], m2.astype(BF16)))
    choice = scores + bias_ref[...]

    neg = -jnp.inf
    c3 = choice.reshape(N_EXPERT_GROUPS, EXPERTS_PER_GROUP, tile)
    i3 = lax.broadcasted_iota(jnp.int32, c3.shape, 1).astype(F32)
    m1 = jnp.max(c3, axis=1, keepdims=True)
    i1 = jnp.min(jnp.where(c3 == m1, i3, float(EXPERTS_PER_GROUP)), axis=1, keepdims=True)
    second = jnp.max(jnp.where(i3 == i1, neg, c3), axis=1)
    gs = m1[:, 0, :] + second

    gi = lax.broadcasted_iota(jnp.int32, gs.shape, 0).astype(F32)
    gsel = jnp.zeros(gs.shape, F32)
    for _ in range(TOPK_GROUPS):
        mx = jnp.max(gs, axis=0, keepdims=True)
        ix = jnp.min(jnp.where(gs == mx, gi, float(N_EXPERT_GROUPS)), axis=0, keepdims=True)
        hit = gi == ix
        gsel = jnp.where(hit, 1.0, gsel)
        gs = jnp.where(hit, neg, gs)
    emask = jnp.broadcast_to(gsel[:, None, :], c3.shape).reshape(N_EXPERTS, tile) > 0.0
    masked = jnp.where(emask, choice, neg)

    ei = lax.broadcasted_iota(jnp.int32, masked.shape, 0).astype(F32)
    hits, idx_rows, w_rows = [], [], []
    for _ in range(TOP_K):
        mx = jnp.max(masked, axis=0, keepdims=True)
        ix = jnp.min(jnp.where(masked == mx, ei, float(N_EXPERTS)), axis=0, keepdims=True)
        hit = ei == ix
        hits.append(hit)
        idx_rows.append(ix)
        w_rows.append(jnp.sum(jnp.where(hit, scores, 0.0), axis=0, keepdims=True))
        masked = jnp.where(hit, neg, masked)
    w_all = jnp.concatenate(w_rows, axis=0)
    topw_ref[...] = w_all / (jnp.sum(w_all, axis=0, keepdims=True) + 1e-20) * ROUTED_SCALE
    topi_ref[...] = jnp.concatenate(idx_rows, axis=0).astype(jnp.int32)

    assign = jnp.zeros(masked.shape, F32)
    for hit in hits:
        assign = jnp.where(hit, 1.0, assign)
    assign_b = assign.astype(BF16)
    t0 = lax.broadcasted_iota(jnp.int32, (tile, tile), 0)
    t1 = lax.broadcasted_iota(jnp.int32, (tile, tile), 1)
    before = jnp.where(t0 < t1, 1.0, 0.0).astype(BF16)
    base = cnt_s[...]
    pos = _dot(assign_b, before) + jnp.concatenate([base] * (tile // LANES), axis=1)
    rank_ref[...] = jnp.concatenate(
        [jnp.sum(jnp.where(hit, pos, 0.0), axis=0, keepdims=True) for hit in hits],
        axis=0).astype(jnp.int32)
    cnt_s[...] = base + _dot(assign_b, jnp.ones((tile, LANES), BF16))
    cnt_ref[...] = cnt_s[...]


def _route(h1, mod2, w_router, router_bias, tokens_per_batch):
    t, d = h1.shape
    tile = min(ROUTE_TILE, tokens_per_batch)
    assert tokens_per_batch % tile == 0 and tile % LANES == 0
    per_b = tokens_per_batch // tile
    wr_t = w_router.T.astype(BF16)
    bias_x = jnp.broadcast_to(router_bias.reshape(N_EXPERTS, 1), (N_EXPERTS, tile)).astype(F32)
    kern = functools.partial(_route_kernel, tile=tile)
    row_spec = pl.BlockSpec((TOP_K, tile), lambda i: (0, i))
    return pl.pallas_call(
        kern,
        grid=(t // tile,),
        in_specs=[pl.BlockSpec((tile, d), lambda i: (i, 0)),
                  pl.BlockSpec((1, SUBLANES, d), lambda i: (i // per_b, 0, 0)),
                  pl.BlockSpec((N_EXPERTS, d), lambda i: (0, 0)),
                  pl.BlockSpec((N_EXPERTS, tile), lambda i: (0, 0))],
        out_specs=[row_spec, row_spec, row_spec,
                   pl.BlockSpec((N_EXPERTS, LANES), lambda i: (0, 0))],
        out_shape=[jax.ShapeDtypeStruct((TOP_K, t), jnp.int32),
                   jax.ShapeDtypeStruct((TOP_K, t), F32),
                   jax.ShapeDtypeStruct((TOP_K, t), jnp.int32),
                   jax.ShapeDtypeStruct((N_EXPERTS, LANES), F32)],
        scratch_shapes=[pltpu.VMEM((N_EXPERTS, LANES), F32)],
        compiler_params=pltpu.CompilerParams(dimension_semantics=("arbitrary",)),
        name="route",
    )(h1, mod2, wr_t, bias_x)


def _slots_kernel(topi_ref, rank_ref, start_ref, dest_ref, *, tile):
    ei = lax.broadcasted_iota(jnp.int32, (N_EXPERTS, tile), 0)
    start = jnp.concatenate([start_ref[...]] * (tile // LANES), axis=1)
    topi = topi_ref[...]
    rows = [jnp.sum(jnp.where(ei == topi[k:k + 1, :], start, 0.0), axis=0, keepdims=True)
            for k in range(TOP_K)]
    dest_ref[...] = jnp.concatenate(rows, axis=0).astype(jnp.int32) + rank_ref[...]


def _slots(topi_t, rank_t, pad_start):
    t = topi_t.shape[1]
    tile = min(SLOTS_TILE, t)
    start_x = jnp.broadcast_to(pad_start.astype(F32).reshape(N_EXPERTS, 1), (N_EXPERTS, LANES))
    row_spec = pl.BlockSpec((TOP_K, tile), lambda i: (0, i))
    return pl.pallas_call(
        functools.partial(_slots_kernel, tile=tile),
        grid=(t // tile,),
        in_specs=[row_spec, row_spec, pl.BlockSpec((N_EXPERTS, LANES), lambda i: (0, 0))],
        out_specs=row_spec,
        out_shape=jax.ShapeDtypeStruct((TOP_K, t), jnp.int32),
        name="slots",
    )(topi_t, rank_t, start_x)


def _to_slabs(slab_ref, x):
    rows, width = x.shape
    n = width // LANES
    for j in range(n):
        slab_ref[pl.ds(j, rows, stride=n), :] = x[:, j * LANES:(j + 1) * LANES]


def _from_slabs(slab_ref, rows, n):
    return jnp.concatenate([slab_ref[pl.ds(j, rows, stride=n), :] for j in range(n)], axis=1)


def _slab(ref, r, n):
    return ref.at[pl.ds(pl.multiple_of(r * n, n), n), :]


def _pack_bf16_pairs(x):
    n = x.shape[1] // 2
    bits = lax.bitcast_convert_type(x.astype(BF16).astype(F32), jnp.uint32)
    return bits[:, n:] | (bits[:, :n] >> 16)


def _unpack_bf16_pairs(p):
    lo = lax.bitcast_convert_type(p << 16, F32)
    hi = lax.bitcast_convert_type(p & jnp.uint32(0xFFFF0000), F32)
    return jnp.concatenate([lo, hi], axis=1).astype(BF16)


def _scatter_kernel(dest_ref, h_ref, mod_ref, zeros_ref, out_ref, slab_s, sem, *, tile, n):
    del zeros_ref
    mod = mod_ref[0]
    _to_slabs(slab_s, _pack_bf16_pairs(_layer_norm(h_ref[...]) * (1.0 + mod[1:2]) + mod[0:1]))

    def issue(t, carry):
        src = _slab(slab_s, t, n)
        for k in range(TOP_K):
            d = dest_ref[t * TOP_K + k]
            pltpu.make_async_copy(src, _slab(out_ref, d, n), sem).start(priority=k % 2)
        return carry

    lax.fori_loop(0, tile, issue, 0)
    for _ in range(TOP_K):
        pltpu.make_async_copy(slab_s, out_ref.at[pl.ds(0, tile * n), :], sem).wait()


def _scatter(h1, mod2, dest_flat, n_slots, tokens_per_batch):
    t, d = h1.shape
    n = d // 2 // LANES
    tile = min(SCATTER_TILE, tokens_per_batch)
    per_b = tokens_per_batch // tile
    kern = functools.partial(_scatter_kernel, tile=tile, n=n)
    return pl.pallas_call(
        kern,
        grid=(t // tile,),
        in_specs=[pl.BlockSpec((tile * TOP_K,), lambda i: (i,), memory_space=pltpu.SMEM),
                  pl.BlockSpec((tile, d), lambda i: (i, 0)),
                  pl.BlockSpec((1, SUBLANES, d), lambda i: (i // per_b, 0, 0)),
                  pl.BlockSpec(memory_space=pl.ANY)],
        out_specs=pl.BlockSpec(memory_space=pl.ANY),
        out_shape=jax.ShapeDtypeStruct((n_slots * n, LANES), jnp.uint32),
        scratch_shapes=[pltpu.VMEM((tile * n, LANES), jnp.uint32), pltpu.SemaphoreType.DMA(())],
        input_output_aliases={3: 0},
        compiler_params=pltpu.CompilerParams(dimension_semantics=("arbitrary",),
                                             has_side_effects=True),
        name="scatter",
    )(dest_flat, h1, mod2, jnp.zeros((n_slots * n, LANES), jnp.uint32))


def _expert_kernel(blk0_ref, nblk_ref, x_ref, wg_ref, wu_ref, wd_ref, y_ref,
                   xbuf_s, ybuf_s, wg_s, wu_s, wd_s, in_sem, out_sem, *, nx, ny):
    e = pl.program_id(0)
    blk0 = blk0_ref[e]
    nblk = nblk_ref[e]
    rows = EXPERT_BLOCK

    def in_copy(c, slot):
        return pltpu.make_async_copy(x_ref.at[pl.ds(pl.multiple_of((blk0 + c) * rows * nx, rows * nx), rows * nx), :],
                                     xbuf_s.at[slot], in_sem.at[slot])

    def out_copy(c, slot):
        return pltpu.make_async_copy(ybuf_s.at[slot],
                                     y_ref.at[pl.ds(pl.multiple_of((blk0 + c) * rows * ny, rows * ny), rows * ny), :],
                                     out_sem.at[slot])

    @pl.when((e < N_EXPERTS) & (nblk > 0))
    def _():
        in_copy(0, 0).start()
        wg_s[...] = wg_ref[0].astype(BF16)
        wu_s[...] = wu_ref[0].astype(BF16)
        wd_s[...] = wd_ref[0].astype(BF16)

        def block(c, carry):
            slot = c % 2
            in_copy(c, slot).wait()

            @pl.when(c + 1 < nblk)
            def _():
                in_copy(c + 1, 1 - slot).start()

            @pl.when(c >= 2)
            def _():
                out_copy(c - 2, slot).wait()

            xb = _unpack_bf16_pairs(_from_slabs(xbuf_s.at[slot], rows, nx))
            hid = (_silu(_dot(xb, wg_s[...])) * _dot(xb, wu_s[...])).astype(BF16)
            _to_slabs(ybuf_s.at[slot], _dot(hid, wd_s[...]))
            out_copy(c, slot).start()
            return carry

        lax.fori_loop(0, nblk, block, 0)

        @pl.when(nblk >= 2)
        def _():
            out_copy(nblk - 2, nblk % 2).wait()

        out_copy(nblk - 1, (nblk - 1) % 2).wait()

    @pl.when((e == N_EXPERTS) & (nblk > 0))
    def _():
        ybuf_s[0] = jnp.zeros(ybuf_s.shape[1:], F32)

        def zero_block(c, carry):
            out_copy(c, 0).start()
            out_copy(c, 0).wait()
            return carry

        lax.fori_loop(0, nblk, zero_block, 0)


def _experts(xs_slabs, blk0, nblk, w_e_gate, w_e_up, w_e_down):
    d, ff = w_e_gate.shape[-2:]
    nx = d // 2 // LANES
    ny = d // LANES
    n_slots = xs_slabs.shape[0] // nx
    rows = EXPERT_BLOCK
    w_idx = lambda e, b0, nb: (jnp.minimum(e, N_EXPERTS - 1), 0, 0)
    grid_spec = pltpu.PrefetchScalarGridSpec(
        num_scalar_prefetch=2,
        grid=(N_EXPERTS + 1,),
        in_specs=[pl.BlockSpec(memory_space=pl.ANY),
                  pl.BlockSpec((1, d, ff), w_idx),
                  pl.BlockSpec((1, d, ff), w_idx),
                  pl.BlockSpec((1, ff, d), w_idx)],
        out_specs=pl.BlockSpec(memory_space=pl.ANY),
        scratch_shapes=[pltpu.VMEM((2, rows * nx, LANES), jnp.uint32),
                        pltpu.VMEM((2, rows * ny, LANES), F32),
                        pltpu.VMEM((d, ff), BF16), pltpu.VMEM((d, ff), BF16),
                        pltpu.VMEM((ff, d), BF16),
                        pltpu.SemaphoreType.DMA((2,)), pltpu.SemaphoreType.DMA((2,))],
    )
    return pl.pallas_call(
        functools.partial(_expert_kernel, nx=nx, ny=ny),
        grid_spec=grid_spec,
        out_shape=jax.ShapeDtypeStruct((n_slots * ny, LANES), F32),
        compiler_params=pltpu.CompilerParams(dimension_semantics=("arbitrary",),
                                             has_side_effects=True),
        name="experts",
    )(blk0, nblk, xs_slabs, w_e_gate, w_e_up, w_e_down)


def _combine_kernel(dest_ref, h_ref, mod_ref, w_ref, y_ref, wsg_ref, wsu_ref, wsd_ref,
                    ln2g_ref, ln2b_ref, o_ref, ybuf_s, sem, *, tile, n):
    def issue(t, carry):
        for k in range(TOP_K):
            d = dest_ref[t * TOP_K + k]
            pltpu.make_async_copy(_slab(y_ref, d, n), _slab(ybuf_s.at[k], t, n),
                                  sem).start(priority=k % 2)
        return carry

    lax.fori_loop(0, tile, issue, 0)

    h = h_ref[...]
    mod = mod_ref[0]
    m2 = (_layer_norm(h) * (1.0 + mod[1:2]) + mod[0:1]).astype(BF16)
    acc = _dot((_silu(_dot(m2, wsg_ref[...])) * _dot(m2, wsu_ref[...])).astype(BF16), wsd_ref[...])

    for k in range(TOP_K):
        pltpu.make_async_copy(y_ref.at[pl.ds(0, tile * n), :], ybuf_s.at[k], sem).wait()
    w = w_ref[...]
    for k in range(TOP_K):
        acc = acc + _from_slabs(ybuf_s.at[k], tile, n) * w[:, k:k + 1]
    o_ref[...] = _layer_norm(DEEPNORM_ALPHA * h + mod[2:3] * acc) * ln2g_ref[...] + ln2b_ref[...]


def _combine(h1, mod2, dest_flat, topw, y_slabs, w_sh_gate, w_sh_up, w_sh_down, ln2_g, ln2_b,
             tokens_per_batch):
    t, d = h1.shape
    n = d // LANES
    ff = w_sh_gate.shape[-1]
    tile = min(COMBINE_TILE, tokens_per_batch)
    per_b = tokens_per_batch // tile
    kern = functools.partial(_combine_kernel, tile=tile, n=n)
    const = lambda shape: pl.BlockSpec(shape, lambda i: (0,) * len(shape))
    return pl.pallas_call(
        kern,
        grid=(t // tile,),
        in_specs=[pl.BlockSpec((tile * TOP_K,), lambda i: (i,), memory_space=pltpu.SMEM),
                  pl.BlockSpec((tile, d), lambda i: (i, 0)),
                  pl.BlockSpec((1, SUBLANES, d), lambda i: (i // per_b, 0, 0)),
                  pl.BlockSpec((tile, TOP_K), lambda i: (i, 0)),
                  pl.BlockSpec(memory_space=pl.ANY),
                  const((d, ff)), const((d, ff)), const((ff, d)), const((1, d)), const((1, d))],
        out_specs=pl.BlockSpec((tile, d), lambda i: (i, 0)),
        out_shape=jax.ShapeDtypeStruct((t, d), F32),
        scratch_shapes=[pltpu.VMEM((TOP_K, tile * n, LANES), F32), pltpu.SemaphoreType.DMA(())],
        compiler_params=pltpu.CompilerParams(dimension_semantics=("arbitrary",)),
        name="combine",
    )(dest_flat, h1, mod2, topw, y_slabs, w_sh_gate.astype(BF16), w_sh_up.astype(BF16),
      w_sh_down.astype(BF16), ln2_g.reshape(1, d), ln2_b.reshape(1, d))


def _moe(h1, mod2, w_router, router_bias, w_e_gate, w_e_up, w_e_down,
         w_sh_gate, w_sh_up, w_sh_down, ln2_g, ln2_b, tokens_per_batch):
    t, d = h1.shape
    topi_t, topw_t, rank_t, cnt = _route(h1, mod2, w_router, router_bias, tokens_per_batch)

    blk = EXPERT_BLOCK
    nb = -(-(t * TOP_K) // blk) + N_EXPERTS
    counts = cnt[:, 0].astype(jnp.int32)
    padded = (counts + blk - 1) // blk * blk
    pad_end = jnp.cumsum(padded)
    pad_start = pad_end - padded
    dest_flat = _slots(topi_t, rank_t, pad_start).T.reshape(t * TOP_K)
    used = pad_end[-1] // blk
    blk0 = jnp.concatenate([pad_start // blk, used[None]]).astype(jnp.int32)
    nblk = jnp.concatenate([padded // blk, (nb - used)[None]]).astype(jnp.int32)

    xs_slabs = _scatter(h1, mod2, dest_flat, nb * blk, tokens_per_batch)
    y_slabs = _experts(xs_slabs, blk0, nblk, w_e_gate, w_e_up, w_e_down)
    return _combine(h1, mod2, dest_flat, topw_t.T, y_slabs, w_sh_gate, w_sh_up, w_sh_down,
                    ln2_g, ln2_b, tokens_per_batch)


def _pad_rows(parts, d):
    rows = jnp.stack(parts, axis=1)
    return jnp.pad(rows, ((0, 0), (0, SUBLANES - rows.shape[1]), (0, 0)))


def kernel(x, c, w_ada, b_ada, w_in, conv_w, conv_b, dt_bias, a_log, d_skip, ssd_norm_w, gmlp_ln_g, gmlp_ln_b, gmlp_ws, gmlp_bs, w_proj_ssd, w_proj_gmlp, w_out, ln1_g, ln1_b, w_router, router_bias, w_e_gate, w_e_up, w_e_down, w_sh_gate, w_sh_up, w_sh_down, ln2_g, ln2_b):
    bsz, seq, d = x.shape
    h = x
    for i in range(w_ada.shape[0]):
        ada = _ada(c, w_ada[i], b_ada[i])
        sh1, sc1, g1, sh2, sc2, g2 = jnp.split(ada, 6, axis=-1)
        h = _mixer(h, _pad_rows([sh1, sc1, g1], d), w_in[i], conv_w[i], conv_b[i], dt_bias[i],
                   a_log[i], d_skip[i], ssd_norm_w[i], gmlp_ln_g[i], gmlp_ln_b[i], gmlp_ws[i],
                   gmlp_bs[i], w_proj_ssd[i], w_proj_gmlp[i], w_out[i], ln1_g[i], ln1_b[i])
        h = _moe(h.reshape(bsz * seq, d), _pad_rows([sh2, sc2, g2], d), w_router[i],
                 router_bias[i], w_e_gate[i], w_e_up[i], w_e_down[i], w_sh_gate[i], w_sh_up[i],
                 w_sh_down[i], ln2_g[i], ln2_b[i], seq).reshape(bsz, seq, d)
    return h
```

```python
import functools

import jax
import jax.numpy as jnp
from jax import lax
from jax.experimental import pallas as pl
from jax.experimental.pallas import tpu as pltpu

F32 = jnp.float32
BF16 = jnp.bfloat16

HEAD_DIM = 64
N_GROUPS = 8
HEADS_PER_GROUP = 4
N_HEADS = N_GROUPS * HEADS_PER_GROUP
D_STATE = 128
CONV_K = 4
CHUNK = 128
GMLP_GROUPS = 8
N_EXPERTS = 256
TOP_K = 8
N_EXPERT_GROUPS = 8
EXPERTS_PER_GROUP = N_EXPERTS // N_EXPERT_GROUPS
TOPK_GROUPS = 4
ROUTED_SCALE = 2.5
DEPTH = 1
DEEPNORM_ALPHA = (2 * DEPTH) ** 0.25
LN_EPS = 1e-5
RMS_EPS = 1e-5

LANES = 128
SUBLANES = 8
VMEM_BYTES_V7X = 64 * 1024 * 1024

MIX_TILE = 256
ROUTE_TILE = 256
SCATTER_TILE = 256
EXPERT_BLOCK = 256
COMBINE_TILE = 128
SLOTS_TILE = 512


def _dot(a, b):
    return jnp.dot(a, b, preferred_element_type=F32)


def _dot_nt(a, b):
    return lax.dot_general(a, b, (((1,), (1,)), ((), ())), preferred_element_type=F32)


def _dot_tn(a, b):
    return lax.dot_general(a, b, (((0,), (0,)), ((), ())), preferred_element_type=F32)


def _layer_norm(x):
    xc = x - jnp.mean(x, -1, keepdims=True)
    var = jnp.mean(xc * xc, -1, keepdims=True)
    return xc * lax.rsqrt(var + LN_EPS)


def _silu(x):
    return x * jax.nn.sigmoid(x)


def _softplus(x):
    return jnp.maximum(x, 0.0) + jnp.log1p(jnp.exp(-jnp.abs(x)))


def _ada_kernel(c_ref, w_ref, b_ref, o_ref):
    s = _silu(c_ref[...]).astype(BF16)
    o_ref[...] = _dot(s, w_ref[...].astype(BF16)) + b_ref[...]


def _ada(c, w_ada, b_ada):
    bsz, d = c.shape
    n = w_ada.shape[1]
    tn = 1024
    return pl.pallas_call(
        _ada_kernel,
        grid=(n // tn,),
        in_specs=[pl.BlockSpec((bsz, d), lambda j: (0, 0)),
                  pl.BlockSpec((d, tn), lambda j: (0, j)),
                  pl.BlockSpec((1, tn), lambda j: (0, j))],
        out_specs=pl.BlockSpec((bsz, tn), lambda j: (0, j)),
        out_shape=jax.ShapeDtypeStruct((bsz, n), F32),
        name="ada",
    )(c, w_ada, b_ada.reshape(1, n))


def _split3_bf16(a):
    hi = a.astype(BF16).astype(F32)
    r1 = a - hi
    mid = r1.astype(BF16).astype(F32)
    lo = r1 - mid
    pad = jnp.zeros((a.shape[0], LANES - 3 * N_HEADS), F32)
    return jnp.concatenate([hi, mid, lo, pad], axis=1).astype(BF16)


def _mixer_kernel(x_ref, mod_ref, w_ref, convw_ref, convb_ref, dtb_ref, alog_ref, dskip_ref,
                  normw_ref, glng_ref, glnb_ref, ws_ref, bsx_ref, e64_ref, e128_ref,
                  wpssd_ref, wpgmlp_ref, wout_ref, ln1g_ref, ln1b_ref,
                  o_ref,
                  m_s, xs_s, bm_s, cm_s, dt_s, y_s, state_s, tail_s,
                  *, tile, d_model, d_inner, offs):
    off_z, off_xbc, off_uv, off_ga, off_gb, off_dt = offs
    n_chunks = tile // CHUNK
    gw = HEADS_PER_GROUP * HEAD_DIM

    @pl.when(pl.program_id(1) == 0)
    def _():
        state_s[...] = jnp.zeros_like(state_s)
        tail_s[...] = jnp.zeros_like(tail_s)

    x = x_ref[0]
    mod = mod_ref[0]
    sh1, sc1, g1 = mod[0:1], mod[1:2], mod[2:3]
    m_s[...] = (_layer_norm(x) * (1.0 + sc1) + sh1).astype(BF16)

    row8 = lax.broadcasted_iota(jnp.int32, (SUBLANES, gw), 0)
    n_xblk = d_inner // gw
    n_bblk = N_GROUPS * D_STATE // gw
    for blk in range(n_xblk + 2 * n_bblk):
        c0 = blk * gw
        pre = _dot(m_s[...], w_ref[:, off_xbc + c0:off_xbc + c0 + gw])
        prev_tail = tail_s[:, c0:c0 + gw]
        tail_s[:, c0:c0 + gw] = pre[tile - SUBLANES:tile]
        cw = convw_ref[:, c0:c0 + gw]
        acc = pre * cw[CONV_K - 1:CONV_K] + convb_ref[:, c0:c0 + gw]
        for k in range(1, CONV_K):
            r = pltpu.roll(pre, k, axis=0)
            top = jnp.where(row8 < k, pltpu.roll(prev_tail, k, axis=0), r[0:SUBLANES])
            r = jnp.concatenate([top, r[SUBLANES:]], axis=0)
            acc = acc + r * cw[CONV_K - 1 - k:CONV_K - k]
        act = _silu(acc)
        if blk < n_xblk:
            xs_s[blk] = act
        else:
            dst = bm_s if blk < n_xblk + n_bblk else cm_s
            g0 = ((blk - n_xblk) % n_bblk) * (gw // D_STATE)
            for i in range(gw // D_STATE):
                dst[g0 + i] = act[:, i * D_STATE:(i + 1) * D_STATE].astype(BF16)

    dt_s[...] = _softplus(_dot(m_s[...], w_ref[:, off_dt:off_dt + LANES]) + dtb_ref[...])
    a_neg = -jnp.exp(alog_ref[...])

    rowc = lax.broadcasted_iota(jnp.int32, (CHUNK, LANES), 0)
    causal = (lax.broadcasted_iota(jnp.int32, (CHUNK, CHUNK), 0)
              >= lax.broadcasted_iota(jnp.int32, (CHUNK, CHUNK), 1))
    lane_g = lax.broadcasted_iota(jnp.int32, (CHUNK, gw), 1)
    lane_c = lax.broadcasted_iota(jnp.int32, (CHUNK, CHUNK), 1)

    def chunk_body(c, carry):
        r0 = pl.multiple_of(c * CHUNK, CHUNK)
        rows = pl.ds(r0, CHUNK)
        dt_c = dt_s[rows, :]
        acs = dt_c * a_neg
        sh = 1
        while sh < CHUNK:
            acs = acs + jnp.where(rowc >= sh, pltpu.roll(acs, sh, axis=0), 0.0)
            sh *= 2
        acs3 = _split3_bf16(acs[:, :N_HEADS])
        acs_x128 = _dot(acs3, e128_ref[...])
        acs_x64 = jnp.concatenate(
            [jnp.where(lane_c < HEAD_DIM, acs_x128[:, (2 * i) * CHUNK:(2 * i + 1) * CHUNK],
                       acs_x128[:, (2 * i + 1) * CHUNK:(2 * i + 2) * CHUNK])
             for i in range(N_HEADS // 2)], axis=1)
        dt_x64 = _dot(_split3_bf16(dt_c[:, :N_HEADS]), e64_ref[...])
        acs_t = acs.T
        last = acs_x64[CHUNK - 1:CHUNK, :]
        grow = jnp.exp(acs_x64)
        to_end = jnp.exp(last - acs_x64)
        chunk_decay = jnp.exp(last)

        for g in range(N_GROUPS):
            cols = slice(g * gw, (g + 1) * gw)
            xs_g = xs_s[g, rows, :]
            bm_g = bm_s[g, rows, :]
            cm_g = cm_s[g, rows, :]
            xdt = xs_g * dt_x64[:, cols]
            cb = _dot_nt(cm_g, bm_g)
            ws, xr = [], []
            for r in range(HEADS_PER_GROUP):
                h = g * HEADS_PER_GROUP + r
                seg = acs_x128[:, h * CHUNK:(h + 1) * CHUNK] - acs_t[h:h + 1, :]
                ws.append((cb * jnp.exp(jnp.where(causal, seg, -jnp.inf))).astype(BF16))
                in_head = (lane_g >= r * HEAD_DIM) & (lane_g < (r + 1) * HEAD_DIM)
                xr.append(jnp.where(in_head, xdt, 0.0).astype(BF16))
            y_g = None
            for r in range(0, HEADS_PER_GROUP, 2):
                part = _dot(jnp.concatenate(ws[r:r + 2], axis=1), jnp.concatenate(xr[r:r + 2], axis=0))
                y_g = part if y_g is None else y_g + part
            st = state_s[g]
            y_g = y_g + _dot(cm_g, st.astype(BF16)) * grow[:, cols]
            state_s[g] = chunk_decay[:, cols] * st + _dot_tn(bm_g, (xdt * to_end[:, cols]).astype(BF16))
            y_s[rows, cols] = y_g + dskip_ref[:, cols] * xs_g
        return carry

    lax.fori_loop(0, n_chunks, chunk_body, 0)

    z = _dot(m_s[...], w_ref[:, off_z:off_z + d_inner])
    y = y_s[...] * _silu(z)
    y = y * lax.rsqrt(jnp.mean(y * y, -1, keepdims=True) + RMS_EPS) * normw_ref[...]
    y_a = _dot(y.astype(BF16), wpssd_ref[...])
    merged = jax.nn.sigmoid(_dot(m_s[...], w_ref[:, off_ga:off_ga + d_model])) * y_a

    uv = jax.nn.gelu(_dot(m_s[...], w_ref[:, off_uv:off_uv + 2 * d_model]))
    u = uv[:, :d_model]
    v = _layer_norm(uv[:, d_model:]) * glng_ref[...] + glnb_ref[...]
    v = v.astype(BF16)
    gd = d_model // GMLP_GROUPS
    ws_m = [jnp.where(causal, ws_ref[g], 0.0).astype(BF16) for g in range(GMLP_GROUPS)]
    v_rows = []
    for c in range(n_chunks):
        v_c = v[c * CHUNK:(c + 1) * CHUNK]
        v_rows.append(jnp.concatenate(
            [_dot(ws_m[g], v_c[:, g * gd:(g + 1) * gd]) for g in range(GMLP_GROUPS)], axis=1)
            + bsx_ref[...])
    v_mix = jnp.concatenate(v_rows, axis=0) if n_chunks > 1 else v_rows[0]
    y_b = _dot((u * v_mix).astype(BF16), wpgmlp_ref[...])
    merged = merged + jax.nn.sigmoid(_dot(m_s[...], w_ref[:, off_gb:off_gb + d_model])) * y_b

    mix = _dot(merged.astype(BF16), wout_ref[...])
    o_ref[0] = _layer_norm(DEEPNORM_ALPHA * x + g1 * mix) * ln1g_ref[...] + ln1b_ref[...]


def _const_spec(shape):
    nd = len(shape)
    return pl.BlockSpec(shape, lambda b, j: (0,) * nd, pipeline_mode=pl.Buffered(1))


def _mixer(x, mod1, w_in, conv_w, conv_b, dt_bias, a_log, d_skip, ssd_norm_w,
           gmlp_ln_g, gmlp_ln_b, gmlp_ws, gmlp_bs, w_proj_ssd, w_proj_gmlp, w_out, ln1_g, ln1_b):
    bsz, seq, d = x.shape
    d_inner = N_HEADS * HEAD_DIM
    gn = N_GROUPS * D_STATE
    d_xbc = d_inner + 2 * gn
    tile = min(MIX_TILE, seq)
    assert seq % tile == 0 and tile % CHUNK == 0 and d_inner == 2 * d

    s0, s1, s2, s3, s4 = (d_inner, d_inner + d_xbc, d_inner + d_xbc + N_HEADS,
                          d_inner + d_xbc + N_HEADS + 2 * d, d_inner + d_xbc + N_HEADS + 3 * d)
    w_dt = jnp.pad(w_in[:, s1:s2], ((0, 0), (0, LANES - N_HEADS)))
    w_all = jnp.concatenate([w_in[:, :s1], w_in[:, s2:], w_dt], axis=1).astype(BF16)
    off_z, off_xbc = 0, d_inner
    off_uv = s1
    off_ga = off_uv + 2 * d
    off_gb = off_ga + d
    off_dt = off_gb + d
    offs = (off_z, off_xbc, off_uv, off_ga, off_gb, off_dt)

    pad_h = lambda v: jnp.pad(v.reshape(1, N_HEADS), ((0, 0), (0, LANES - N_HEADS)))
    head_of_col64 = jnp.arange(d_inner) // HEAD_DIM
    head_of_col128 = jnp.arange(N_HEADS * CHUNK) // CHUNK
    piece_head = jnp.where(jnp.arange(LANES) < 3 * N_HEADS, jnp.arange(LANES) % N_HEADS, -1)
    e64 = (piece_head[:, None] == head_of_col64[None, :]).astype(BF16)
    e128 = (piece_head[:, None] == head_of_col128[None, :]).astype(BF16)
    dskip_x = jnp.broadcast_to(d_skip[:, None], (N_HEADS, HEAD_DIM)).reshape(1, d_inner)
    bs_x = jnp.broadcast_to(gmlp_bs.T[:, :, None], (CHUNK, GMLP_GROUPS, d // GMLP_GROUPS)).reshape(CHUNK, d)

    operands = [
        w_all, conv_w, conv_b.reshape(1, d_xbc), pad_h(dt_bias), pad_h(a_log), dskip_x,
        ssd_norm_w.reshape(1, d_inner), gmlp_ln_g.reshape(1, d), gmlp_ln_b.reshape(1, d),
        gmlp_ws, bs_x, e64, e128,
        w_proj_ssd.astype(BF16), w_proj_gmlp.astype(BF16), w_out.astype(BF16),
        ln1_g.reshape(1, d), ln1_b.reshape(1, d),
    ]
    gw = HEADS_PER_GROUP * HEAD_DIM
    kern = functools.partial(_mixer_kernel, tile=tile, d_model=d, d_inner=d_inner, offs=offs)
    return pl.pallas_call(
        kern,
        grid=(bsz, seq // tile),
        in_specs=[pl.BlockSpec((1, tile, d), lambda b, j: (b, j, 0)),
                  pl.BlockSpec((1, SUBLANES, d), lambda b, j: (b, 0, 0))]
                 + [_const_spec(op.shape) for op in operands],
        out_specs=pl.BlockSpec((1, tile, d), lambda b, j: (b, j, 0)),
        out_shape=jax.ShapeDtypeStruct((bsz, seq, d), F32),
        scratch_shapes=[
            pltpu.VMEM((tile, d), BF16),
            pltpu.VMEM((N_GROUPS, tile, gw), F32),
            pltpu.VMEM((N_GROUPS, tile, D_STATE), BF16),
            pltpu.VMEM((N_GROUPS, tile, D_STATE), BF16),
            pltpu.VMEM((tile, LANES), F32),
            pltpu.VMEM((tile, d_inner), F32),
            pltpu.VMEM((N_GROUPS, D_STATE, gw), F32),
            pltpu.VMEM((SUBLANES, d_xbc), F32),
        ],
        compiler_params=pltpu.CompilerParams(
            dimension_semantics=("arbitrary", "arbitrary"),
            vmem_limit_bytes=56 * 1024 * 1024),
        name="mixer",
    )(x, mod1, *operands)


def _route_kernel(h_ref, mod_ref, wr_ref, bias_ref, topi_ref, topw_ref, rank_ref, cnt_ref, cnt_s,
                  *, tile):
    @pl.when(pl.program_id(0) == 0)
    def _():
        cnt_s[...] = jnp.zeros_like(cnt_s)

    mod = mod_ref[0]
    m2 = _layer_norm(h_ref[...]) * (1.0 + mod[1:2]) + mod[0:1]
    scores = jax.nn.sigmoid(_dot_nt(wr_ref[...], m2.astype(BF16)))
    choice = scores + bias_ref[...]

    neg = -jnp.inf
    c3 = choice.reshape(N_EXPERT_GROUPS, EXPERTS_PER_GROUP, tile)
    i3 = lax.broadcasted_iota(jnp.int32, c3.shape, 1).astype(F32)
    m1 = jnp.max(c3, axis=1, keepdims=True)
    i1 = jnp.min(jnp.where(c3 == m1, i3, float(EXPERTS_PER_GROUP)), axis=1, keepdims=True)
    second = jnp.max(jnp.where(i3 == i1, neg, c3), axis=1)
    gs = m1[:, 0, :] + second

    gi = lax.broadcasted_iota(jnp.int32, gs.shape, 0).astype(F32)
    gsel = jnp.zeros(gs.shape, F32)
    for _ in range(TOPK_GROUPS):
        mx = jnp.max(gs, axis=0, keepdims=True)
        ix = jnp.min(jnp.where(gs == mx, gi, float(N_EXPERT_GROUPS)), axis=0, keepdims=True)
        hit = gi == ix
        gsel = jnp.where(hit, 1.0, gsel)
        gs = jnp.where(hit, neg, gs)
    emask = jnp.broadcast_to(gsel[:, None, :], c3.shape).reshape(N_EXPERTS, tile) > 0.0
    masked = jnp.where(emask, choice, neg)

    ei = lax.broadcasted_iota(jnp.int32, masked.shape, 0).astype(F32)
    hits, idx_rows, w_rows = [], [], []
    for _ in range(TOP_K):
        mx = jnp.max(masked, axis=0, keepdims=True)
        ix = jnp.min(jnp.where(masked == mx, ei, float(N_EXPERTS)), axis=0, keepdims=True)
        hit = ei == ix
        hits.append(hit)
        idx_rows.append(ix)
        w_rows.append(jnp.sum(jnp.where(hit, scores, 0.0), axis=0, keepdims=True))
        masked = jnp.where(hit, neg, masked)
    w_all = jnp.concatenate(w_rows, axis=0)
    topw_ref[...] = w_all / (jnp.sum(w_all, axis=0, keepdims=True) + 1e-20) * ROUTED_SCALE
    topi_ref[...] = jnp.concatenate(idx_rows, axis=0).astype(jnp.int32)

    assign = jnp.zeros(masked.shape, F32)
    for hit in hits:
        assign = jnp.where(hit, 1.0, assign)
    assign_b = assign.astype(BF16)
    t0 = lax.broadcasted_iota(jnp.int32, (tile, tile), 0)
    t1 = lax.broadcasted_iota(jnp.int32, (tile, tile), 1)
    before = jnp.where(t0 < t1, 1.0, 0.0).astype(BF16)
    base = cnt_s[...]
    pos = _dot(assign_b, before) + jnp.concatenate([base] * (tile // LANES), axis=1)
    rank_ref[...] = jnp.concatenate(
        [jnp.sum(jnp.where(hit, pos, 0.0), axis=0, keepdims=True) for hit in hits],
        axis=0).astype(jnp.int32)
    cnt_s[...] = base + _dot(assign_b, jnp.ones((tile, LANES), BF16))
    cnt_ref[...] = cnt_s[...]


def _route(h1, mod2, w_router, router_bias, tokens_per_batch):
    t, d = h1.shape
    tile = min(ROUTE_TILE, tokens_per_batch)
    assert tokens_per_batch % tile == 0 and tile % LANES == 0
    per_b = tokens_per_batch // tile
    wr_t = w_router.T.astype(BF16)
    bias_x = jnp.broadcast_to(router_bias.reshape(N_EXPERTS, 1), (N_EXPERTS, tile)).astype(F32)
    kern = functools.partial(_route_kernel, tile=tile)
    row_spec = pl.BlockSpec((TOP_K, tile), lambda i: (0, i))
    return pl.pallas_call(
        kern,
        grid=(t // tile,),
        in_specs=[pl.BlockSpec((tile, d), lambda i: (i, 0)),
                  pl.BlockSpec((1, SUBLANES, d), lambda i: (i // per_b, 0, 0)),
                  pl.BlockSpec((N_EXPERTS, d), lambda i: (0, 0)),
                  pl.BlockSpec((N_EXPERTS, tile), lambda i: (0, 0))],
        out_specs=[row_spec, row_spec, row_spec,
                   pl.BlockSpec((N_EXPERTS, LANES), lambda i: (0, 0))],
        out_shape=[jax.ShapeDtypeStruct((TOP_K, t), jnp.int32),
                   jax.ShapeDtypeStruct((TOP_K, t), F32),
                   jax.ShapeDtypeStruct((TOP_K, t), jnp.int32),
                   jax.ShapeDtypeStruct((N_EXPERTS, LANES), F32)],
        scratch_shapes=[pltpu.VMEM((N_EXPERTS, LANES), F32)],
        compiler_params=pltpu.CompilerParams(dimension_semantics=("arbitrary",)),
        name="route",
    )(h1, mod2, wr_t, bias_x)


def _slots_kernel(topi_ref, rank_ref, start_ref, dest_ref, *, tile):
    ei = lax.broadcasted_iota(jnp.int32, (N_EXPERTS, tile), 0)
    start = jnp.concatenate([start_ref[...]] * (tile // LANES), axis=1)
    topi = topi_ref[...]
    rows = [jnp.sum(jnp.where(ei == topi[k:k + 1, :], start, 0.0), axis=0, keepdims=True)
            for k in range(TOP_K)]
    dest_ref[...] = jnp.concatenate(rows, axis=0).astype(jnp.int32) + rank_ref[...]


def _slots(topi_t, rank_t, pad_start):
    t = topi_t.shape[1]
    tile = min(SLOTS_TILE, t)
    start_x = jnp.broadcast_to(pad_start.astype(F32).reshape(N_EXPERTS, 1), (N_EXPERTS, LANES))
    row_spec = pl.BlockSpec((TOP_K, tile), lambda i: (0, i))
    return pl.pallas_call(
        functools.partial(_slots_kernel, tile=tile),
        grid=(t // tile,),
        in_specs=[row_spec, row_spec, pl.BlockSpec((N_EXPERTS, LANES), lambda i: (0, 0))],
        out_specs=row_spec,
        out_shape=jax.ShapeDtypeStruct((TOP_K, t), jnp.int32),
        name="slots",
    )(topi_t, rank_t, start_x)


def _to_slabs(slab_ref, x):
    rows, width = x.shape
    n = width // LANES
    for j in range(n):
        slab_ref[pl.ds(j, rows, stride=n), :] = x[:, j * LANES:(j + 1) * LANES]


def _from_slabs(slab_ref, rows, n):
    return jnp.concatenate([slab_ref[pl.ds(j, rows, stride=n), :] for j in range(n)], axis=1)


def _slab(ref, r, n):
    return ref.at[pl.ds(pl.multiple_of(r * n, n), n), :]


def _pack_bf16_pairs(x):
    n = x.shape[1] // 2
    bits = lax.bitcast_convert_type(x.astype(BF16).astype(F32), jnp.uint32)
    return bits[:, n:] | (bits[:, :n] >> 16)


def _unpack_bf16_pairs(p):
    lo = lax.bitcast_convert_type(p << 16, F32)
    hi = lax.bitcast_convert_type(p & jnp.uint32(0xFFFF0000), F32)
    return jnp.concatenate([lo, hi], axis=1).astype(BF16)


def _scatter_kernel(dest_ref, h_ref, mod_ref, zeros_ref, out_ref, slab_s, sem, *, tile, n, n_steps):
    del zeros_ref
    i = pl.program_id(0)
    slot = i % 2
    mod = mod_ref[0]
    _to_slabs(slab_s.at[slot],
              _pack_bf16_pairs(_layer_norm(h_ref[...]) * (1.0 + mod[1:2]) + mod[0:1]))

    def issue(t, carry):
        src = _slab(slab_s.at[slot], t, n)
        for k in range(TOP_K):
            d = dest_ref[t * TOP_K + k]
            pltpu.make_async_copy(src, _slab(out_ref, d, n), sem.at[slot]).start(priority=k % 2)
        return carry

    lax.fori_loop(0, tile, issue, 0)

    def drain(s):
        for _ in range(TOP_K):
            pltpu.make_async_copy(slab_s.at[s], out_ref.at[pl.ds(0, tile * n), :], sem.at[s]).wait()

    @pl.when(i >= 1)
    def _():
        drain(1 - slot)

    @pl.when(i == n_steps - 1)
    def _():
        drain(slot)


def _scatter(h1, mod2, dest_flat, n_slots, tokens_per_batch):
    t, d = h1.shape
    n = d // 2 // LANES
    tile = min(SCATTER_TILE, tokens_per_batch)
    per_b = tokens_per_batch // tile
    kern = functools.partial(_scatter_kernel, tile=tile, n=n, n_steps=t // tile)
    return pl.pallas_call(
        kern,
        grid=(t // tile,),
        in_specs=[pl.BlockSpec((tile * TOP_K,), lambda i: (i,), memory_space=pltpu.SMEM),
                  pl.BlockSpec((tile, d), lambda i: (i, 0)),
                  pl.BlockSpec((1, SUBLANES, d), lambda i: (i // per_b, 0, 0)),
                  pl.BlockSpec(memory_space=pl.ANY)],
        out_specs=pl.BlockSpec(memory_space=pl.ANY),
        out_shape=jax.ShapeDtypeStruct((n_slots * n, LANES), jnp.uint32),
        scratch_shapes=[pltpu.VMEM((2, tile * n, LANES), jnp.uint32),
                        pltpu.SemaphoreType.DMA((2,))],
        input_output_aliases={3: 0},
        compiler_params=pltpu.CompilerParams(dimension_semantics=("arbitrary",),
                                             has_side_effects=True),
        name="scatter",
    )(dest_flat, h1, mod2, jnp.zeros((n_slots * n, LANES), jnp.uint32))


def _expert_kernel(blk0_ref, nblk_ref, x_ref, wg_ref, wu_ref, wd_ref, y_ref,
                   xbuf_s, ybuf_s, wg_s, wu_s, wd_s, in_sem, out_sem, *, nx, ny):
    e = pl.program_id(0)
    blk0 = blk0_ref[e]
    nblk = nblk_ref[e]
    used = blk0_ref[N_EXPERTS]
    rows = EXPERT_BLOCK

    def in_copy(gb, slot):
        return pltpu.make_async_copy(x_ref.at[pl.ds(pl.multiple_of(gb * rows * nx, rows * nx), rows * nx), :],
                                     xbuf_s.at[slot], in_sem.at[slot])

    def out_copy(gb, slot):
        return pltpu.make_async_copy(ybuf_s.at[slot],
                                     y_ref.at[pl.ds(pl.multiple_of(gb * rows * ny, rows * ny), rows * ny), :],
                                     out_sem.at[slot])

    @pl.when((e == 0) & (used > 0))
    def _():
        in_copy(0, 0).start()

    @pl.when((e < N_EXPERTS) & (nblk > 0))
    def _():
        wg_s[...] = wg_ref[0].astype(BF16)
        wu_s[...] = wu_ref[0].astype(BF16)
        wd_s[...] = wd_ref[0].astype(BF16)

        def block(c, carry):
            gb = blk0 + c
            slot = gb % 2
            in_copy(gb, slot).wait()

            @pl.when(gb + 1 < used)
            def _():
                in_copy(gb + 1, 1 - slot).start()

            @pl.when(gb >= 2)
            def _():
                out_copy(gb - 2, slot).wait()

            xb = _unpack_bf16_pairs(_from_slabs(xbuf_s.at[slot], rows, nx))
            hid = (_silu(_dot(xb, wg_s[...])) * _dot(xb, wu_s[...])).astype(BF16)
            _to_slabs(ybuf_s.at[slot], _dot(hid, wd_s[...]))
            out_copy(gb, slot).start()
            return carry

        lax.fori_loop(0, nblk, block, 0)

    @pl.when(e == N_EXPERTS)
    def _():
        @pl.when(used >= 2)
        def _():
            out_copy(used - 2, used % 2).wait()

        @pl.when(used >= 1)
        def _():
            out_copy(used - 1, (used - 1) % 2).wait()

        @pl.when(nblk > 0)
        def _():
            ybuf_s[0] = jnp.zeros(ybuf_s.shape[1:], F32)

            def zero_block(c, carry):
                out_copy(blk0 + c, 0).start()
                out_copy(blk0 + c, 0).wait()
                return carry

            lax.fori_loop(0, nblk, zero_block, 0)


def _experts(xs_slabs, blk0, nblk, w_e_gate, w_e_up, w_e_down):
    d, ff = w_e_gate.shape[-2:]
    nx = d // 2 // LANES
    ny = d // LANES
    n_slots = xs_slabs.shape[0] // nx
    rows = EXPERT_BLOCK
    w_idx = lambda e, b0, nb: (jnp.minimum(e, N_EXPERTS - 1), 0, 0)
    grid_spec = pltpu.PrefetchScalarGridSpec(
        num_scalar_prefetch=2,
        grid=(N_EXPERTS + 1,),
        in_specs=[pl.BlockSpec(memory_space=pl.ANY),
                  pl.BlockSpec((1, d, ff), w_idx),
                  pl.BlockSpec((1, d, ff), w_idx),
                  pl.BlockSpec((1, ff, d), w_idx)],
        out_specs=pl.BlockSpec(memory_space=pl.ANY),
        scratch_shapes=[pltpu.VMEM((2, rows * nx, LANES), jnp.uint32),
                        pltpu.VMEM((2, rows * ny, LANES), F32),
                        pltpu.VMEM((d, ff), BF16), pltpu.VMEM((d, ff), BF16),
                        pltpu.VMEM((ff, d), BF16),
                        pltpu.SemaphoreType.DMA((2,)), pltpu.SemaphoreType.DMA((2,))],
    )
    return pl.pallas_call(
        functools.partial(_expert_kernel, nx=nx, ny=ny),
        grid_spec=grid_spec,
        out_shape=jax.ShapeDtypeStruct((n_slots * ny, LANES), F32),
        compiler_params=pltpu.CompilerParams(dimension_semantics=("arbitrary",),
                                             has_side_effects=True),
        name="experts",
    )(blk0, nblk, xs_slabs, w_e_gate, w_e_up, w_e_down)


def _combine_kernel(dest_ref, dest_next_ref, h_ref, mod_ref, w_ref, y_ref, wsg_ref, wsu_ref, wsd_ref,
                    ln2g_ref, ln2b_ref, o_ref, ybuf_s, sem, *, tile, n, n_steps):
    i = pl.program_id(0)
    slot = i % 2

    def issue_tile(dref, s):
        def issue(t, carry):
            for k in range(TOP_K):
                d = dref[t * TOP_K + k]
                pltpu.make_async_copy(_slab(y_ref, d, n), _slab(ybuf_s.at[s, k], t, n),
                                      sem.at[s]).start(priority=k % 2)
            return carry
        lax.fori_loop(0, tile, issue, 0)

    @pl.when(i == 0)
    def _():
        issue_tile(dest_ref, 0)

    @pl.when(i + 1 < n_steps)
    def _():
        issue_tile(dest_next_ref, 1 - slot)

    h = h_ref[...]
    mod = mod_ref[0]
    m2 = (_layer_norm(h) * (1.0 + mod[1:2]) + mod[0:1]).astype(BF16)
    acc = _dot((_silu(_dot(m2, wsg_ref[...])) * _dot(m2, wsu_ref[...])).astype(BF16), wsd_ref[...])

    for k in range(TOP_K):
        pltpu.make_async_copy(y_ref.at[pl.ds(0, tile * n), :], ybuf_s.at[slot, k],
                              sem.at[slot]).wait()
    w = w_ref[...]
    for k in range(TOP_K):
        acc = acc + _from_slabs(ybuf_s.at[slot, k], tile, n) * w[:, k:k + 1]
    o_ref[...] = _layer_norm(DEEPNORM_ALPHA * h + mod[2:3] * acc) * ln2g_ref[...] + ln2b_ref[...]


def _combine(h1, mod2, dest_flat, topw, y_slabs, w_sh_gate, w_sh_up, w_sh_down, ln2_g, ln2_b,
             tokens_per_batch):
    t, d = h1.shape
    n = d // LANES
    ff = w_sh_gate.shape[-1]
    tile = min(COMBINE_TILE, tokens_per_batch)
    per_b = tokens_per_batch // tile
    n_steps = t // tile
    kern = functools.partial(_combine_kernel, tile=tile, n=n, n_steps=n_steps)
    const = lambda shape: pl.BlockSpec(shape, lambda i: (0,) * len(shape))
    return pl.pallas_call(
        kern,
        grid=(n_steps,),
        in_specs=[pl.BlockSpec((tile * TOP_K,), lambda i: (i,), memory_space=pltpu.SMEM),
                  pl.BlockSpec((tile * TOP_K,), lambda i: (jnp.minimum(i + 1, n_steps - 1),),
                               memory_space=pltpu.SMEM),
                  pl.BlockSpec((tile, d), lambda i: (i, 0)),
                  pl.BlockSpec((1, SUBLANES, d), lambda i: (i // per_b, 0, 0)),
                  pl.BlockSpec((tile, TOP_K), lambda i: (i, 0)),
                  pl.BlockSpec(memory_space=pl.ANY),
                  const((d, ff)), const((d, ff)), const((ff, d)), const((1, d)), const((1, d))],
        out_specs=pl.BlockSpec((tile, d), lambda i: (i, 0)),
        out_shape=jax.ShapeDtypeStruct((t, d), F32),
        scratch_shapes=[pltpu.VMEM((2, TOP_K, tile * n, LANES), F32),
                        pltpu.SemaphoreType.DMA((2,))],
        compiler_params=pltpu.CompilerParams(dimension_semantics=("arbitrary",)),
        name="combine",
    )(dest_flat, dest_flat, h1, mod2, topw, y_slabs, w_sh_gate.astype(BF16), w_sh_up.astype(BF16),
      w_sh_down.astype(BF16), ln2_g.reshape(1, d), ln2_b.reshape(1, d))


def _moe(h1, mod2, w_router, router_bias, w_e_gate, w_e_up, w_e_down,
         w_sh_gate, w_sh_up, w_sh_down, ln2_g, ln2_b, tokens_per_batch):
    t, d = h1.shape
    topi_t, topw_t, rank_t, cnt = _route(h1, mod2, w_router, router_bias, tokens_per_batch)

    blk = EXPERT_BLOCK
    nb = -(-(t * TOP_K) // blk) + N_EXPERTS
    counts = cnt[:, 0].astype(jnp.int32)
    padded = (counts + blk - 1) // blk * blk
    pad_end = jnp.cumsum(padded)
    pad_start = pad_end - padded
    dest_flat = _slots(topi_t, rank_t, pad_start).T.reshape(t * TOP_K)
    used = pad_end[-1] // blk
    blk0 = jnp.concatenate([pad_start // blk, used[None]]).astype(jnp.int32)
    nblk = jnp.concatenate([padded // blk, (nb - used)[None]]).astype(jnp.int32)

    xs_slabs = _scatter(h1, mod2, dest_flat, nb * blk, tokens_per_batch)
    y_slabs = _experts(xs_slabs, blk0, nblk, w_e_gate, w_e_up, w_e_down)
    return _combine(h1, mod2, dest_flat, topw_t.T, y_slabs, w_sh_gate, w_sh_up, w_sh_down,
                    ln2_g, ln2_b, tokens_per_batch)


def _pad_rows(parts, d):
    rows = jnp.stack(parts, axis=1)
    return jnp.pad(rows, ((0, 0), (0, SUBLANES - rows.shape[1]), (0, 0)))


def kernel(x, c, w_ada, b_ada, w_in, conv_w, conv_b, dt_bias, a_log, d_skip, ssd_norm_w, gmlp_ln_g, gmlp_ln_b, gmlp_ws, gmlp_bs, w_proj_ssd, w_proj_gmlp, w_out, ln1_g, ln1_b, w_router, router_bias, w_e_gate, w_e_up, w_e_down, w_sh_gate, w_sh_up, w_sh_down, ln2_g, ln2_b):
    bsz, seq, d = x.shape
    h = x
    for i in range(w_ada.shape[0]):
        ada = _ada(c, w_ada[i], b_ada[i])
        sh1, sc1, g1, sh2, sc2, g2 = jnp.split(ada, 6, axis=-1)
        h = _mixer(h, _pad_rows([sh1, sc1, g1], d), w_in[i], conv_w[i], conv_b[i], dt_bias[i],
                   a_log[i], d_skip[i], ssd_norm_w[i], gmlp_ln_g[i], gmlp_ln_b[i], gmlp_ws[i],
                   gmlp_bs[i], w_proj_ssd[i], w_proj_gmlp[i], w_out[i], ln1_g[i], ln1_b[i])
        h = _moe(h.reshape(bsz * seq, d), _pad_rows([sh2, sc2, g2], d), w_router[i],
                 router_bias[i], w_e_gate[i], w_e_up[i], w_e_down[i], w_sh_gate[i], w_sh_up[i],
                 w_sh_down[i], ln2_g[i], ln2_b[i], seq).reshape(bsz, seq, d)
    return h
```

```python
import functools

import jax
import jax.numpy as jnp
from jax import lax
from jax.experimental import pallas as pl
from jax.experimental.pallas import tpu as pltpu

F32 = jnp.float32
BF16 = jnp.bfloat16

HEAD_DIM = 64
N_GROUPS = 8
HEADS_PER_GROUP = 4
N_HEADS = N_GROUPS * HEADS_PER_GROUP
D_STATE = 128
CONV_K = 4
CHUNK = 128
GMLP_GROUPS = 8
N_EXPERTS = 256
TOP_K = 8
N_EXPERT_GROUPS = 8
EXPERTS_PER_GROUP = N_EXPERTS // N_EXPERT_GROUPS
TOPK_GROUPS = 4
ROUTED_SCALE = 2.5
DEPTH = 1
DEEPNORM_ALPHA = (2 * DEPTH) ** 0.25
LN_EPS = 1e-5
RMS_EPS = 1e-5

LANES = 128
SUBLANES = 8
VMEM_BYTES_V7X = 64 * 1024 * 1024
MIXER_VMEM_LIMIT = VMEM_BYTES_V7X - 8 * 1024 * 1024

MIX_TILE = 256
ROUTE_TILE = 256
SCATTER_TILE = 256
EXPERT_BLOCK = 256
EXPERT_IN_BUFS = 4
EXPERT_OUT_BUFS = 4
COMBINE_TILE = 128
SLOTS_TILE = 512


def _dot(a, b):
    return jnp.dot(a, b, preferred_element_type=F32)


def _dot_nt(a, b):
    return lax.dot_general(a, b, (((1,), (1,)), ((), ())), preferred_element_type=F32)


def _dot_tn(a, b):
    return lax.dot_general(a, b, (((0,), (0,)), ((), ())), preferred_element_type=F32)


def _layer_norm(x):
    xc = x - jnp.mean(x, -1, keepdims=True)
    var = jnp.mean(xc * xc, -1, keepdims=True)
    return xc * lax.rsqrt(var + LN_EPS)


def _silu(x):
    return x * jax.nn.sigmoid(x)


def _softplus(x):
    return jnp.maximum(x, 0.0) + jnp.log1p(jnp.exp(-jnp.abs(x)))


def _ada_kernel(c_ref, w_ref, b_ref, o_ref):
    s = _silu(c_ref[...]).astype(BF16)
    o_ref[...] = _dot(s, w_ref[...].astype(BF16)) + b_ref[...]


def _ada(c, w_ada, b_ada):
    bsz, d = c.shape
    n = w_ada.shape[1]
    tn = 1024
    return pl.pallas_call(
        _ada_kernel,
        grid=(n // tn,),
        in_specs=[pl.BlockSpec((bsz, d), lambda j: (0, 0)),
                  pl.BlockSpec((d, tn), lambda j: (0, j)),
                  pl.BlockSpec((1, tn), lambda j: (0, j))],
        out_specs=pl.BlockSpec((bsz, tn), lambda j: (0, j)),
        out_shape=jax.ShapeDtypeStruct((bsz, n), F32),
        name="ada",
    )(c, w_ada, b_ada.reshape(1, n))


def _split3_bf16(a):
    hi = a.astype(BF16).astype(F32)
    r1 = a - hi
    mid = r1.astype(BF16).astype(F32)
    lo = r1 - mid
    pad = jnp.zeros((a.shape[0], LANES - 3 * N_HEADS), F32)
    return jnp.concatenate([hi, mid, lo, pad], axis=1).astype(BF16)


def _mixer_kernel(x_ref, mod_ref, w_ref, convw_ref, convb_ref, dtb_ref, alog_ref, dskip_ref,
                  normw_ref, glng_ref, glnb_ref, ws_ref, bsx_ref, e64_ref, e128_ref,
                  wpssd_ref, wpgmlp_ref, wout_ref, ln1g_ref, ln1b_ref,
                  o_ref,
                  m_s, xs_s, bm_s, cm_s, dt_s, y_s, state_s, tail_s,
                  *, tile, d_model, d_inner, offs):
    off_z, off_xbc, off_uv, off_ga, off_gb, off_dt = offs
    n_chunks = tile // CHUNK
    gw = HEADS_PER_GROUP * HEAD_DIM

    @pl.when(pl.program_id(1) == 0)
    def _():
        state_s[...] = jnp.zeros_like(state_s)
        tail_s[...] = jnp.zeros_like(tail_s)

    x = x_ref[0]
    mod = mod_ref[0]
    sh1, sc1, g1 = mod[0:1], mod[1:2], mod[2:3]
    m_s[...] = (_layer_norm(x) * (1.0 + sc1) + sh1).astype(BF16)

    row8 = lax.broadcasted_iota(jnp.int32, (SUBLANES, gw), 0)
    n_xblk = d_inner // gw
    n_bblk = N_GROUPS * D_STATE // gw
    for blk in range(n_xblk + 2 * n_bblk):
        c0 = blk * gw
        pre = _dot(m_s[...], w_ref[:, off_xbc + c0:off_xbc + c0 + gw])
        prev_tail = tail_s[:, c0:c0 + gw]
        tail_s[:, c0:c0 + gw] = pre[tile - SUBLANES:tile]
        cw = convw_ref[:, c0:c0 + gw]
        acc = pre * cw[CONV_K - 1:CONV_K] + convb_ref[:, c0:c0 + gw]
        for k in range(1, CONV_K):
            r = pltpu.roll(pre, k, axis=0)
            top = jnp.where(row8 < k, pltpu.roll(prev_tail, k, axis=0), r[0:SUBLANES])
            r = jnp.concatenate([top, r[SUBLANES:]], axis=0)
            acc = acc + r * cw[CONV_K - 1 - k:CONV_K - k]
        act = _silu(acc)
        if blk < n_xblk:
            xs_s[blk] = act
        else:
            dst = bm_s if blk < n_xblk + n_bblk else cm_s
            g0 = ((blk - n_xblk) % n_bblk) * (gw // D_STATE)
            for i in range(gw // D_STATE):
                dst[g0 + i] = act[:, i * D_STATE:(i + 1) * D_STATE].astype(BF16)

    dt_s[...] = _softplus(_dot(m_s[...], w_ref[:, off_dt:off_dt + LANES]) + dtb_ref[...])
    a_neg = -jnp.exp(alog_ref[...])

    rowc = lax.broadcasted_iota(jnp.int32, (CHUNK, LANES), 0)
    causal = (lax.broadcasted_iota(jnp.int32, (CHUNK, CHUNK), 0)
              >= lax.broadcasted_iota(jnp.int32, (CHUNK, CHUNK), 1))
    lane_g = lax.broadcasted_iota(jnp.int32, (CHUNK, gw), 1)
    lane_c = lax.broadcasted_iota(jnp.int32, (CHUNK, CHUNK), 1)

    def chunk_body(c, carry):
        r0 = pl.multiple_of(c * CHUNK, CHUNK)
        rows = pl.ds(r0, CHUNK)
        dt_c = dt_s[rows, :]
        acs = dt_c * a_neg
        sh = 1
        while sh < CHUNK:
            acs = acs + jnp.where(rowc >= sh, pltpu.roll(acs, sh, axis=0), 0.0)
            sh *= 2
        acs3 = _split3_bf16(acs[:, :N_HEADS])
        acs_x128 = _dot(acs3, e128_ref[...])
        acs_x64 = jnp.concatenate(
            [jnp.where(lane_c < HEAD_DIM, acs_x128[:, (2 * i) * CHUNK:(2 * i + 1) * CHUNK],
                       acs_x128[:, (2 * i + 1) * CHUNK:(2 * i + 2) * CHUNK])
             for i in range(N_HEADS // 2)], axis=1)
        dt_x64 = _dot(_split3_bf16(dt_c[:, :N_HEADS]), e64_ref[...])
        acs_t = acs.T
        last = acs_x64[CHUNK - 1:CHUNK, :]
        grow = jnp.exp(acs_x64)
        to_end = jnp.exp(last - acs_x64)
        chunk_decay = jnp.exp(last)

        for g in range(N_GROUPS):
            cols = slice(g * gw, (g + 1) * gw)
            xs_g = xs_s[g, rows, :]
            bm_g = bm_s[g, rows, :]
            cm_g = cm_s[g, rows, :]
            xdt = xs_g * dt_x64[:, cols]
            cb = _dot_nt(cm_g, bm_g)
            ws, xr = [], []
            for r in range(HEADS_PER_GROUP):
                h = g * HEADS_PER_GROUP + r
                seg = acs_x128[:, h * CHUNK:(h + 1) * CHUNK] - acs_t[h:h + 1, :]
                ws.append((cb * jnp.exp(jnp.where(causal, seg, -jnp.inf))).astype(BF16))
                in_head = (lane_g >= r * HEAD_DIM) & (lane_g < (r + 1) * HEAD_DIM)
                xr.append(jnp.where(in_head, xdt, 0.0).astype(BF16))
            y_g = None
            for r in range(0, HEADS_PER_GROUP, 2):
                part = _dot(jnp.concatenate(ws[r:r + 2], axis=1), jnp.concatenate(xr[r:r + 2], axis=0))
                y_g = part if y_g is None else y_g + part
            st = state_s[g]
            y_g = y_g + _dot(cm_g, st.astype(BF16)) * grow[:, cols]
            state_s[g] = chunk_decay[:, cols] * st + _dot_tn(bm_g, (xdt * to_end[:, cols]).astype(BF16))
            y_s[rows, cols] = y_g + dskip_ref[:, cols] * xs_g
        return carry

    lax.fori_loop(0, n_chunks, chunk_body, 0)

    z = _dot(m_s[...], w_ref[:, off_z:off_z + d_inner])
    y = y_s[...] * _silu(z)
    y = y * lax.rsqrt(jnp.mean(y * y, -1, keepdims=True) + RMS_EPS) * normw_ref[...]
    y_a = _dot(y.astype(BF16), wpssd_ref[...])
    merged = jax.nn.sigmoid(_dot(m_s[...], w_ref[:, off_ga:off_ga + d_model])) * y_a

    uv = jax.nn.gelu(_dot(m_s[...], w_ref[:, off_uv:off_uv + 2 * d_model]))
    u = uv[:, :d_model]
    v = _layer_norm(uv[:, d_model:]) * glng_ref[...] + glnb_ref[...]
    v = v.astype(BF16)
    gd = d_model // GMLP_GROUPS
    ws_m = [jnp.where(causal, ws_ref[g], 0.0).astype(BF16) for g in range(GMLP_GROUPS)]
    v_rows = []
    for c in range(n_chunks):
        v_c = v[c * CHUNK:(c + 1) * CHUNK]
        v_rows.append(jnp.concatenate(
            [_dot(ws_m[g], v_c[:, g * gd:(g + 1) * gd]) for g in range(GMLP_GROUPS)], axis=1)
            + bsx_ref[...])
    v_mix = jnp.concatenate(v_rows, axis=0) if n_chunks > 1 else v_rows[0]
    y_b = _dot((u * v_mix).astype(BF16), wpgmlp_ref[...])
    merged = merged + jax.nn.sigmoid(_dot(m_s[...], w_ref[:, off_gb:off_gb + d_model])) * y_b

    mix = _dot(merged.astype(BF16), wout_ref[...])
    o_ref[0] = _layer_norm(DEEPNORM_ALPHA * x + g1 * mix) * ln1g_ref[...] + ln1b_ref[...]


def _const_spec(shape):
    nd = len(shape)
    return pl.BlockSpec(shape, lambda b, j: (0,) * nd, pipeline_mode=pl.Buffered(1))


def _mixer(x, mod1, w_in, conv_w, conv_b, dt_bias, a_log, d_skip, ssd_norm_w,
           gmlp_ln_g, gmlp_ln_b, gmlp_ws, gmlp_bs, w_proj_ssd, w_proj_gmlp, w_out, ln1_g, ln1_b):
    bsz, seq, d = x.shape
    d_inner = N_HEADS * HEAD_DIM
    gn = N_GROUPS * D_STATE
    d_xbc = d_inner + 2 * gn
    tile = min(MIX_TILE, seq)
    assert seq % tile == 0 and tile % CHUNK == 0 and d_inner == 2 * d

    s0, s1, s2, s3, s4 = (d_inner, d_inner + d_xbc, d_inner + d_xbc + N_HEADS,
                          d_inner + d_xbc + N_HEADS + 2 * d, d_inner + d_xbc + N_HEADS + 3 * d)
    w_dt = jnp.pad(w_in[:, s1:s2], ((0, 0), (0, LANES - N_HEADS)))
    w_all = jnp.concatenate([w_in[:, :s1], w_in[:, s2:], w_dt], axis=1).astype(BF16)
    off_z, off_xbc = 0, d_inner
    off_uv = s1
    off_ga = off_uv + 2 * d
    off_gb = off_ga + d
    off_dt = off_gb + d
    offs = (off_z, off_xbc, off_uv, off_ga, off_gb, off_dt)

    pad_h = lambda v: jnp.pad(v.reshape(1, N_HEADS), ((0, 0), (0, LANES - N_HEADS)))
    head_of_col64 = jnp.arange(d_inner) // HEAD_DIM
    head_of_col128 = jnp.arange(N_HEADS * CHUNK) // CHUNK
    piece_head = jnp.where(jnp.arange(LANES) < 3 * N_HEADS, jnp.arange(LANES) % N_HEADS, -1)
    e64 = (piece_head[:, None] == head_of_col64[None, :]).astype(BF16)
    e128 = (piece_head[:, None] == head_of_col128[None, :]).astype(BF16)
    dskip_x = jnp.broadcast_to(d_skip[:, None], (N_HEADS, HEAD_DIM)).reshape(1, d_inner)
    bs_x = jnp.broadcast_to(gmlp_bs.T[:, :, None], (CHUNK, GMLP_GROUPS, d // GMLP_GROUPS)).reshape(CHUNK, d)

    operands = [
        w_all, conv_w, conv_b.reshape(1, d_xbc), pad_h(dt_bias), pad_h(a_log), dskip_x,
        ssd_norm_w.reshape(1, d_inner), gmlp_ln_g.reshape(1, d), gmlp_ln_b.reshape(1, d),
        gmlp_ws, bs_x, e64, e128,
        w_proj_ssd.astype(BF16), w_proj_gmlp.astype(BF16), w_out.astype(BF16),
        ln1_g.reshape(1, d), ln1_b.reshape(1, d),
    ]
    gw = HEADS_PER_GROUP * HEAD_DIM
    kern = functools.partial(_mixer_kernel, tile=tile, d_model=d, d_inner=d_inner, offs=offs)
    return pl.pallas_call(
        kern,
        grid=(bsz, seq // tile),
        in_specs=[pl.BlockSpec((1, tile, d), lambda b, j: (b, j, 0)),
                  pl.BlockSpec((1, SUBLANES, d), lambda b, j: (b, 0, 0))]
                 + [_const_spec(op.shape) for op in operands],
        out_specs=pl.BlockSpec((1, tile, d), lambda b, j: (b, j, 0)),
        out_shape=jax.ShapeDtypeStruct((bsz, seq, d), F32),
        scratch_shapes=[
            pltpu.VMEM((tile, d), BF16),
            pltpu.VMEM((N_GROUPS, tile, gw), F32),
            pltpu.VMEM((N_GROUPS, tile, D_STATE), BF16),
            pltpu.VMEM((N_GROUPS, tile, D_STATE), BF16),
            pltpu.VMEM((tile, LANES), F32),
            pltpu.VMEM((tile, d_inner), F32),
            pltpu.VMEM((N_GROUPS, D_STATE, gw), F32),
            pltpu.VMEM((SUBLANES, d_xbc), F32),
        ],
        compiler_params=pltpu.CompilerParams(
            dimension_semantics=("arbitrary", "arbitrary"),
            vmem_limit_bytes=MIXER_VMEM_LIMIT),
        name="mixer",
    )(x, mod1, *operands)


def _route_kernel(h_ref, mod_ref, wr_ref, bias_ref, topi_ref, topw_ref, rank_ref, cnt_ref, cnt_s,
                  *, tile):
    @pl.when(pl.program_id(0) == 0)
    def _():
        cnt_s[...] = jnp.zeros_like(cnt_s)

    mod = mod_ref[0]
    m2 = _layer_norm(h_ref[...]) * (1.0 + mod[1:2]) + mod[0:1]
    scores = jax.nn.sigmoid(_dot_nt(wr_ref[...], m2.astype(BF16)))
    choice = scores + bias_ref[...]

    neg = -jnp.inf
    c3 = choice.reshape(N_EXPERT_GROUPS, EXPERTS_PER_GROUP, tile)
    i3 = lax.broadcasted_iota(jnp.int32, c3.shape, 1).astype(F32)
    m1 = jnp.max(c3, axis=1, keepdims=True)
    i1 = jnp.min(jnp.where(c3 == m1, i3, float(EXPERTS_PER_GROUP)), axis=1, keepdims=True)
    second = jnp.max(jnp.where(i3 == i1, neg, c3), axis=1)
    gs = m1[:, 0, :] + second

    gi = lax.broadcasted_iota(jnp.int32, gs.shape, 0).astype(F32)
    gsel = jnp.zeros(gs.shape, F32)
    for _ in range(TOPK_GROUPS):
        mx = jnp.max(gs, axis=0, keepdims=True)
        ix = jnp.min(jnp.where(gs == mx, gi, float(N_EXPERT_GROUPS)), axis=0, keepdims=True)
        hit = gi == ix
        gsel = jnp.where(hit, 1.0, gsel)
        gs = jnp.where(hit, neg, gs)
    emask = jnp.broadcast_to(gsel[:, None, :], c3.shape).reshape(N_EXPERTS, tile) > 0.0
    masked = jnp.where(emask, choice, neg)

    ei = lax.broadcasted_iota(jnp.int32, masked.shape, 0).astype(F32)
    hits, idx_rows, w_rows = [], [], []
    for _ in range(TOP_K):
        mx = jnp.max(masked, axis=0, keepdims=True)
        ix = jnp.min(jnp.where(masked == mx, ei, float(N_EXPERTS)), axis=0, keepdims=True)
        hit = ei == ix
        hits.append(hit)
        idx_rows.append(ix)
        w_rows.append(jnp.sum(jnp.where(hit, scores, 0.0), axis=0, keepdims=True))
        masked = jnp.where(hit, neg, masked)
    w_all = jnp.concatenate(w_rows, axis=0)
    topw_ref[...] = w_all / (jnp.sum(w_all, axis=0, keepdims=True) + 1e-20) * ROUTED_SCALE
    topi_ref[...] = jnp.concatenate(idx_rows, axis=0).astype(jnp.int32)

    assign = jnp.zeros(masked.shape, F32)
    for hit in hits:
        assign = jnp.where(hit, 1.0, assign)
    assign_b = assign.astype(BF16)
    t0 = lax.broadcasted_iota(jnp.int32, (tile, tile), 0)
    t1 = lax.broadcasted_iota(jnp.int32, (tile, tile), 1)
    before = jnp.where(t0 < t1, 1.0, 0.0).astype(BF16)
    base = cnt_s[...]
    pos = _dot(assign_b, before) + jnp.concatenate([base] * (tile // LANES), axis=1)
    rank_ref[...] = jnp.concatenate(
        [jnp.sum(jnp.where(hit, pos, 0.0), axis=0, keepdims=True) for hit in hits],
        axis=0).astype(jnp.int32)
    cnt_s[...] = base + _dot(assign_b, jnp.ones((tile, LANES), BF16))
    cnt_ref[...] = cnt_s[...]


def _route(h1, mod2, w_router, router_bias, tokens_per_batch):
    t, d = h1.shape
    tile = min(ROUTE_TILE, tokens_per_batch)
    assert tokens_per_batch % tile == 0 and tile % LANES == 0
    per_b = tokens_per_batch // tile
    wr_t = w_router.T.astype(BF16)
    bias_x = jnp.broadcast_to(router_bias.reshape(N_EXPERTS, 1), (N_EXPERTS, tile)).astype(F32)
    kern = functools.partial(_route_kernel, tile=tile)
    row_spec = pl.BlockSpec((TOP_K, tile), lambda i: (0, i))
    return pl.pallas_call(
        kern,
        grid=(t // tile,),
        in_specs=[pl.BlockSpec((tile, d), lambda i: (i, 0)),
                  pl.BlockSpec((1, SUBLANES, d), lambda i: (i // per_b, 0, 0)),
                  pl.BlockSpec((N_EXPERTS, d), lambda i: (0, 0)),
                  pl.BlockSpec((N_EXPERTS, tile), lambda i: (0, 0))],
        out_specs=[row_spec, row_spec, row_spec,
                   pl.BlockSpec((N_EXPERTS, LANES), lambda i: (0, 0))],
        out_shape=[jax.ShapeDtypeStruct((TOP_K, t), jnp.int32),
                   jax.ShapeDtypeStruct((TOP_K, t), F32),
                   jax.ShapeDtypeStruct((TOP_K, t), jnp.int32),
                   jax.ShapeDtypeStruct((N_EXPERTS, LANES), F32)],
        scratch_shapes=[pltpu.VMEM((N_EXPERTS, LANES), F32)],
        compiler_params=pltpu.CompilerParams(dimension_semantics=("arbitrary",)),
        name="route",
    )(h1, mod2, wr_t, bias_x)


def _slots_kernel(topi_ref, rank_ref, start_ref, dest_ref, *, tile):
    ei = lax.broadcasted_iota(jnp.int32, (N_EXPERTS, tile), 0)
    start = jnp.concatenate([start_ref[...]] * (tile // LANES), axis=1)
    topi = topi_ref[...]
    rows = [jnp.sum(jnp.where(ei == topi[k:k + 1, :], start, 0.0), axis=0, keepdims=True)
            for k in range(TOP_K)]
    dest_ref[...] = jnp.concatenate(rows, axis=0).astype(jnp.int32) + rank_ref[...]


def _slots(topi_t, rank_t, pad_start):
    t = topi_t.shape[1]
    tile = min(SLOTS_TILE, t)
    start_x = jnp.broadcast_to(pad_start.astype(F32).reshape(N_EXPERTS, 1), (N_EXPERTS, LANES))
    row_spec = pl.BlockSpec((TOP_K, tile), lambda i: (0, i))
    return pl.pallas_call(
        functools.partial(_slots_kernel, tile=tile),
        grid=(t // tile,),
        in_specs=[row_spec, row_spec, pl.BlockSpec((N_EXPERTS, LANES), lambda i: (0, 0))],
        out_specs=row_spec,
        out_shape=jax.ShapeDtypeStruct((TOP_K, t), jnp.int32),
        name="slots",
    )(topi_t, rank_t, start_x)


def _to_slabs(slab_ref, x):
    rows, width = x.shape
    n = width // LANES
    for j in range(n):
        slab_ref[pl.ds(j, rows, stride=n), :] = x[:, j * LANES:(j + 1) * LANES]


def _from_slabs(slab_ref, rows, n):
    return jnp.concatenate([slab_ref[pl.ds(j, rows, stride=n), :] for j in range(n)], axis=1)


def _slab(ref, r, n):
    return ref.at[pl.ds(pl.multiple_of(r * n, n), n), :]


def _pack_bf16_pairs(x):
    n = x.shape[1] // 2
    bits = lax.bitcast_convert_type(x.astype(BF16).astype(F32), jnp.uint32)
    return bits[:, n:] | (bits[:, :n] >> 16)


def _unpack_bf16_pairs(p):
    lo = lax.bitcast_convert_type(p << 16, F32)
    hi = lax.bitcast_convert_type(p & jnp.uint32(0xFFFF0000), F32)
    return jnp.concatenate([lo, hi], axis=1).astype(BF16)


def _scatter_kernel(zflag_ref, dest_ref, h_ref, mod_ref, out_ref, slab_s, zero_s, sem, zsem,
                    *, tile, n, n_steps, n_blocks):
    i = pl.program_id(0)
    slot = i % 2

    @pl.when(i == 0)
    def _():
        zero_s[...] = jnp.zeros_like(zero_s)
        brows = EXPERT_BLOCK * n

        def zero_copy(b):
            return pltpu.make_async_copy(
                zero_s, out_ref.at[pl.ds(pl.multiple_of(b * brows, brows), brows), :], zsem)

        def start(b, carry):
            @pl.when(zflag_ref[b] > 0)
            def _():
                zero_copy(b).start()
            return carry

        def wait(b, carry):
            @pl.when(zflag_ref[b] > 0)
            def _():
                zero_copy(b).wait()
            return carry

        lax.fori_loop(0, n_blocks, start, 0)
        lax.fori_loop(0, n_blocks, wait, 0)

    mod = mod_ref[0]
    _to_slabs(slab_s.at[slot],
              _pack_bf16_pairs(_layer_norm(h_ref[...]) * (1.0 + mod[1:2]) + mod[0:1]))

    def issue(t, carry):
        src = _slab(slab_s.at[slot], t, n)
        for k in range(TOP_K):
            d = dest_ref[t * TOP_K + k]
            pltpu.make_async_copy(src, _slab(out_ref, d, n), sem.at[slot]).start(priority=k % 2)
        return carry

    lax.fori_loop(0, tile, issue, 0)

    def drain(s):
        for _ in range(TOP_K):
            pltpu.make_async_copy(slab_s.at[s], out_ref.at[pl.ds(0, tile * n), :], sem.at[s]).wait()

    @pl.when(i >= 1)
    def _():
        drain(1 - slot)

    @pl.when(i == n_steps - 1)
    def _():
        drain(slot)


def _scatter(h1, mod2, dest_flat, zero_flag, tokens_per_batch):
    t, d = h1.shape
    n = d // 2 // LANES
    n_blocks = zero_flag.shape[0]
    tile = min(SCATTER_TILE, tokens_per_batch)
    per_b = tokens_per_batch // tile
    kern = functools.partial(_scatter_kernel, tile=tile, n=n, n_steps=t // tile, n_blocks=n_blocks)
    grid_spec = pltpu.PrefetchScalarGridSpec(
        num_scalar_prefetch=1,
        grid=(t // tile,),
        in_specs=[pl.BlockSpec((tile * TOP_K,), lambda i, zf: (i,), memory_space=pltpu.SMEM),
                  pl.BlockSpec((tile, d), lambda i, zf: (i, 0)),
                  pl.BlockSpec((1, SUBLANES, d), lambda i, zf: (i // per_b, 0, 0))],
        out_specs=pl.BlockSpec(memory_space=pl.ANY),
        scratch_shapes=[pltpu.VMEM((2, tile * n, LANES), jnp.uint32),
                        pltpu.VMEM((EXPERT_BLOCK * n, LANES), jnp.uint32),
                        pltpu.SemaphoreType.DMA((2,)), pltpu.SemaphoreType.DMA(())],
    )
    return pl.pallas_call(
        kern,
        grid_spec=grid_spec,
        out_shape=jax.ShapeDtypeStruct((n_blocks * EXPERT_BLOCK * n, LANES), jnp.uint32),
        compiler_params=pltpu.CompilerParams(dimension_semantics=("arbitrary",),
                                             has_side_effects=True),
        name="scatter",
    )(zero_flag, dest_flat, h1, mod2)


def _expert_kernel(blk0_ref, nblk_ref, x_ref, wg_ref, wu_ref, wd_ref, y_ref,
                   xbuf_s, ybuf_s, wg_s, wu_s, wd_s, in_sem, out_sem, *, nx, ny):
    e = pl.program_id(0)
    blk0 = blk0_ref[e]
    nblk = nblk_ref[e]
    used = blk0_ref[N_EXPERTS]
    rows = EXPERT_BLOCK
    ahead = EXPERT_IN_BUFS - 1

    def in_copy(gb):
        slot = gb % EXPERT_IN_BUFS
        return pltpu.make_async_copy(x_ref.at[pl.ds(pl.multiple_of(gb * rows * nx, rows * nx), rows * nx), :],
                                     xbuf_s.at[slot], in_sem.at[slot])

    def out_copy(gb):
        slot = gb % EXPERT_OUT_BUFS
        return pltpu.make_async_copy(ybuf_s.at[slot],
                                     y_ref.at[pl.ds(pl.multiple_of(gb * rows * ny, rows * ny), rows * ny), :],
                                     out_sem.at[slot])

    @pl.when(e == 0)
    def _():
        for b in range(ahead):
            @pl.when(b < used)
            def _():
                in_copy(b).start()

    @pl.when((e < N_EXPERTS) & (nblk > 0))
    def _():
        wg_s[...] = wg_ref[0].astype(BF16)
        wu_s[...] = wu_ref[0].astype(BF16)
        wd_s[...] = wd_ref[0].astype(BF16)

        def block(c, carry):
            gb = blk0 + c
            in_copy(gb).wait()

            @pl.when(gb + ahead < used)
            def _():
                in_copy(gb + ahead).start()

            @pl.when(gb >= EXPERT_OUT_BUFS)
            def _():
                out_copy(gb - EXPERT_OUT_BUFS).wait()

            xb = _unpack_bf16_pairs(_from_slabs(xbuf_s.at[gb % EXPERT_IN_BUFS], rows, nx))
            hid = (_silu(_dot(xb, wg_s[...])) * _dot(xb, wu_s[...])).astype(BF16)
            _to_slabs(ybuf_s.at[gb % EXPERT_OUT_BUFS], _dot(hid, wd_s[...]))
            out_copy(gb).start()
            return carry

        lax.fori_loop(0, nblk, block, 0)

    @pl.when(e == N_EXPERTS)
    def _():
        for b in range(EXPERT_OUT_BUFS, 0, -1):
            @pl.when(used >= b)
            def _():
                out_copy(used - b).wait()

        @pl.when(nblk > 0)
        def _():
            ybuf_s[0] = jnp.zeros(ybuf_s.shape[1:], F32)

            def zero_block(c, carry):
                gb = blk0 + c
                cp = pltpu.make_async_copy(
                    ybuf_s.at[0],
                    y_ref.at[pl.ds(pl.multiple_of(gb * rows * ny, rows * ny), rows * ny), :],
                    out_sem.at[0])
                cp.start()
                cp.wait()
                return carry

            lax.fori_loop(0, nblk, zero_block, 0)


def _experts(xs_slabs, blk0, nblk, w_e_gate, w_e_up, w_e_down):
    d, ff = w_e_gate.shape[-2:]
    nx = d // 2 // LANES
    ny = d // LANES
    n_slots = xs_slabs.shape[0] // nx
    rows = EXPERT_BLOCK
    w_idx = lambda e, b0, nb: (jnp.minimum(e, N_EXPERTS - 1), 0, 0)
    grid_spec = pltpu.PrefetchScalarGridSpec(
        num_scalar_prefetch=2,
        grid=(N_EXPERTS + 1,),
        in_specs=[pl.BlockSpec(memory_space=pl.ANY),
                  pl.BlockSpec((1, d, ff), w_idx),
                  pl.BlockSpec((1, d, ff), w_idx),
                  pl.BlockSpec((1, ff, d), w_idx)],
        out_specs=pl.BlockSpec(memory_space=pl.ANY),
        scratch_shapes=[pltpu.VMEM((EXPERT_IN_BUFS, rows * nx, LANES), jnp.uint32),
                        pltpu.VMEM((EXPERT_OUT_BUFS, rows * ny, LANES), F32),
                        pltpu.VMEM((d, ff), BF16), pltpu.VMEM((d, ff), BF16),
                        pltpu.VMEM((ff, d), BF16),
                        pltpu.SemaphoreType.DMA((EXPERT_IN_BUFS,)),
                        pltpu.SemaphoreType.DMA((EXPERT_OUT_BUFS,))],
    )
    return pl.pallas_call(
        functools.partial(_expert_kernel, nx=nx, ny=ny),
        grid_spec=grid_spec,
        out_shape=jax.ShapeDtypeStruct((n_slots * ny, LANES), F32),
        compiler_params=pltpu.CompilerParams(dimension_semantics=("arbitrary",),
                                             has_side_effects=True),
        name="experts",
    )(blk0, nblk, xs_slabs, w_e_gate, w_e_up, w_e_down)


def _combine_kernel(dest_ref, dest_next_ref, h_ref, mod_ref, w_ref, y_ref, wsg_ref, wsu_ref, wsd_ref,
                    ln2g_ref, ln2b_ref, o_ref, ybuf_s, sem, *, tile, n, n_steps):
    i = pl.program_id(0)
    slot = i % 2

    def issue_tile(dref, s):
        def issue(t, carry):
            for k in range(TOP_K):
                d = dref[t * TOP_K + k]
                pltpu.make_async_copy(_slab(y_ref, d, n), _slab(ybuf_s.at[s, k], t, n),
                                      sem.at[s]).start(priority=k % 2)
            return carry
        lax.fori_loop(0, tile, issue, 0)

    @pl.when(i == 0)
    def _():
        issue_tile(dest_ref, 0)

    @pl.when(i + 1 < n_steps)
    def _():
        issue_tile(dest_next_ref, 1 - slot)

    h = h_ref[...]
    mod = mod_ref[0]
    m2 = (_layer_norm(h) * (1.0 + mod[1:2]) + mod[0:1]).astype(BF16)
    acc = _dot((_silu(_dot(m2, wsg_ref[...])) * _dot(m2, wsu_ref[...])).astype(BF16), wsd_ref[...])

    for k in range(TOP_K):
        pltpu.make_async_copy(y_ref.at[pl.ds(0, tile * n), :], ybuf_s.at[slot, k],
                              sem.at[slot]).wait()
    w = w_ref[...]
    for k in range(TOP_K):
        acc = acc + _from_slabs(ybuf_s.at[slot, k], tile, n) * w[:, k:k + 1]
    o_ref[...] = _layer_norm(DEEPNORM_ALPHA * h + mod[2:3] * acc) * ln2g_ref[...] + ln2b_ref[...]


def _combine(h1, mod2, dest_flat, topw, y_slabs, w_sh_gate, w_sh_up, w_sh_down, ln2_g, ln2_b,
             tokens_per_batch):
    t, d = h1.shape
    n = d // LANES
    ff = w_sh_gate.shape[-1]
    tile = min(COMBINE_TILE, tokens_per_batch)
    per_b = tokens_per_batch // tile
    n_steps = t // tile
    kern = functools.partial(_combine_kernel, tile=tile, n=n, n_steps=n_steps)
    const = lambda shape: pl.BlockSpec(shape, lambda i: (0,) * len(shape))
    return pl.pallas_call(
        kern,
        grid=(n_steps,),
        in_specs=[pl.BlockSpec((tile * TOP_K,), lambda i: (i,), memory_space=pltpu.SMEM),
                  pl.BlockSpec((tile * TOP_K,), lambda i: (jnp.minimum(i + 1, n_steps - 1),),
                               memory_space=pltpu.SMEM),
                  pl.BlockSpec((tile, d), lambda i: (i, 0)),
                  pl.BlockSpec((1, SUBLANES, d), lambda i: (i // per_b, 0, 0)),
                  pl.BlockSpec((tile, TOP_K), lambda i: (i, 0)),
                  pl.BlockSpec(memory_space=pl.ANY),
                  const((d, ff)), const((d, ff)), const((ff, d)), const((1, d)), const((1, d))],
        out_specs=pl.BlockSpec((tile, d), lambda i: (i, 0)),
        out_shape=jax.ShapeDtypeStruct((t, d), F32),
        scratch_shapes=[pltpu.VMEM((2, TOP_K, tile * n, LANES), F32),
                        pltpu.SemaphoreType.DMA((2,))],
        compiler_params=pltpu.CompilerParams(dimension_semantics=("arbitrary",)),
        name="combine",
    )(dest_flat, dest_flat, h1, mod2, topw, y_slabs, w_sh_gate.astype(BF16), w_sh_up.astype(BF16),
      w_sh_down.astype(BF16), ln2_g.reshape(1, d), ln2_b.reshape(1, d))


def _moe(h1, mod2, w_router, router_bias, w_e_gate, w_e_up, w_e_down,
         w_sh_gate, w_sh_up, w_sh_down, ln2_g, ln2_b, tokens_per_batch):
    t, d = h1.shape
    topi_t, topw_t, rank_t, cnt = _route(h1, mod2, w_router, router_bias, tokens_per_batch)

    blk = EXPERT_BLOCK
    nb = -(-(t * TOP_K) // blk) + N_EXPERTS
    counts = cnt[:, 0].astype(jnp.int32)
    padded = (counts + blk - 1) // blk * blk
    pad_end = jnp.cumsum(padded)
    pad_start = pad_end - padded
    dest_flat = _slots(topi_t, rank_t, pad_start).T.reshape(t * TOP_K)
    used = pad_end[-1] // blk
    blk0 = jnp.concatenate([pad_start // blk, used[None]]).astype(jnp.int32)
    nblk = jnp.concatenate([padded // blk, (nb - used)[None]]).astype(jnp.int32)

    block_id = jnp.arange(nb, dtype=jnp.int32)
    is_last = jnp.any((block_id[:, None] == (pad_end // blk - 1)[None, :]) & (padded > 0)[None, :], axis=1)
    zero_flag = (is_last | (block_id >= used)).astype(jnp.int32)

    xs_slabs = _scatter(h1, mod2, dest_flat, zero_flag, tokens_per_batch)
    y_slabs = _experts(xs_slabs, blk0, nblk, w_e_gate, w_e_up, w_e_down)
    return _combine(h1, mod2, dest_flat, topw_t.T, y_slabs, w_sh_gate, w_sh_up, w_sh_down,
                    ln2_g, ln2_b, tokens_per_batch)


def _pad_rows(parts, d):
    rows = jnp.stack(parts, axis=1)
    return jnp.pad(rows, ((0, 0), (0, SUBLANES - rows.shape[1]), (0, 0)))


def kernel(x, c, w_ada, b_ada, w_in, conv_w, conv_b, dt_bias, a_log, d_skip, ssd_norm_w, gmlp_ln_g, gmlp_ln_b, gmlp_ws, gmlp_bs, w_proj_ssd, w_proj_gmlp, w_out, ln1_g, ln1_b, w_router, router_bias, w_e_gate, w_e_up, w_e_down, w_sh_gate, w_sh_up, w_sh_down, ln2_g, ln2_b):
    bsz, seq, d = x.shape
    h = x
    for i in range(w_ada.shape[0]):
        ada = _ada(c, w_ada[i], b_ada[i])
        sh1, sc1, g1, sh2, sc2, g2 = jnp.split(ada, 6, axis=-1)
        h = _mixer(h, _pad_rows([sh1, sc1, g1], d), w_in[i], conv_w[i], conv_b[i], dt_bias[i],
                   a_log[i], d_skip[i], ssd_norm_w[i], gmlp_ln_g[i], gmlp_ln_b[i], gmlp_ws[i],
                   gmlp_bs[i], w_proj_ssd[i], w_proj_gmlp[i], w_out[i], ln1_g[i], ln1_b[i])
        h = _moe(h.reshape(bsz * seq, d), _pad_rows([sh2, sc2, g2], d), w_router[i],
                 router_bias[i], w_e_gate[i], w_e_up[i], w_e_down[i], w_sh_gate[i], w_sh_up[i],
                 w_sh_down[i], ln2_g[i], ln2_b[i], seq).reshape(bsz, seq, d)
    return h
```

```python
import functools

import jax
import jax.numpy as jnp
from jax import lax
from jax.experimental import pallas as pl
from jax.experimental.pallas import tpu as pltpu

F32 = jnp.float32
BF16 = jnp.bfloat16

HEAD_DIM = 64
N_GROUPS = 8
HEADS_PER_GROUP = 4
N_HEADS = N_GROUPS * HEADS_PER_GROUP
D_STATE = 128
CONV_K = 4
CHUNK = 128
GMLP_GROUPS = 8
N_EXPERTS = 256
TOP_K = 8
N_EXPERT_GROUPS = 8
EXPERTS_PER_GROUP = N_EXPERTS // N_EXPERT_GROUPS
TOPK_GROUPS = 4
ROUTED_SCALE = 2.5
DEPTH = 1
DEEPNORM_ALPHA = (2 * DEPTH) ** 0.25
LN_EPS = 1e-5
RMS_EPS = 1e-5

LANES = 128
SUBLANES = 8
VMEM_BYTES_V7X = 64 * 1024 * 1024
MIXER_VMEM_LIMIT = VMEM_BYTES_V7X - 8 * 1024 * 1024

MIX_TILE = 256
ROUTE_TILE = 256
SCATTER_TILE = 256
EXPERT_BLOCK = 256
EXPERT_IN_BUFS = 4
EXPERT_OUT_BUFS = 4
COMBINE_TILE = 128
SLOTS_TILE = 512


def _dot(a, b):
    return jnp.dot(a, b, preferred_element_type=F32)


def _dot_nt(a, b):
    return lax.dot_general(a, b, (((1,), (1,)), ((), ())), preferred_element_type=F32)


def _dot_tn(a, b):
    return lax.dot_general(a, b, (((0,), (0,)), ((), ())), preferred_element_type=F32)


def _layer_norm(x):
    xc = x - jnp.mean(x, -1, keepdims=True)
    var = jnp.mean(xc * xc, -1, keepdims=True)
    return xc * lax.rsqrt(var + LN_EPS)


def _silu(x):
    return x * jax.nn.sigmoid(x)


def _softplus(x):
    return jnp.maximum(x, 0.0) + jnp.log1p(jnp.exp(-jnp.abs(x)))


def _ada_kernel(c_ref, w_ref, b_ref, o_ref):
    s = _silu(c_ref[...]).astype(BF16)
    o_ref[...] = _dot(s, w_ref[...].astype(BF16)) + b_ref[...]


def _ada(c, w_ada, b_ada):
    bsz, d = c.shape
    n = w_ada.shape[1]
    tn = 1024
    return pl.pallas_call(
        _ada_kernel,
        grid=(n // tn,),
        in_specs=[pl.BlockSpec((bsz, d), lambda j: (0, 0)),
                  pl.BlockSpec((d, tn), lambda j: (0, j)),
                  pl.BlockSpec((1, tn), lambda j: (0, j))],
        out_specs=pl.BlockSpec((bsz, tn), lambda j: (0, j)),
        out_shape=jax.ShapeDtypeStruct((bsz, n), F32),
        name="ada",
    )(c, w_ada, b_ada.reshape(1, n))


def _split3_bf16(a):
    hi = a.astype(BF16).astype(F32)
    r1 = a - hi
    mid = r1.astype(BF16).astype(F32)
    lo = r1 - mid
    pad = jnp.zeros((a.shape[0], LANES - 3 * N_HEADS), F32)
    return jnp.concatenate([hi, mid, lo, pad], axis=1).astype(BF16)


def _mixer_kernel(x_ref, mod_ref, w_ref, convw_ref, convb_ref, dtb_ref, alog_ref, dskip_ref,
                  normw_ref, glng_ref, glnb_ref, ws_ref, bsx_ref, e64_ref,
                  wpssd_ref, wpgmlp_ref, wout_ref, ln1g_ref, ln1b_ref,
                  o_ref,
                  m_s, xs_s, bm_s, cm_s, dt_s, y_s, state_s, tail_s,
                  *, tile, d_model, d_inner, offs):
    off_z, off_xbc, off_uv, off_ga, off_gb, off_dt = offs
    n_chunks = tile // CHUNK
    gw = HEADS_PER_GROUP * HEAD_DIM

    @pl.when(pl.program_id(1) == 0)
    def _():
        state_s[...] = jnp.zeros_like(state_s)
        tail_s[...] = jnp.zeros_like(tail_s)

    x = x_ref[0]
    mod = mod_ref[0]
    sh1, sc1, g1 = mod[0:1], mod[1:2], mod[2:3]
    m_s[...] = (_layer_norm(x) * (1.0 + sc1) + sh1).astype(BF16)

    row8 = lax.broadcasted_iota(jnp.int32, (SUBLANES, gw), 0)
    n_xblk = d_inner // gw
    n_bblk = N_GROUPS * D_STATE // gw
    for blk in range(n_xblk + 2 * n_bblk):
        c0 = blk * gw
        pre = _dot(m_s[...], w_ref[:, off_xbc + c0:off_xbc + c0 + gw])
        prev_tail = tail_s[:, c0:c0 + gw]
        tail_s[:, c0:c0 + gw] = pre[tile - SUBLANES:tile]
        cw = convw_ref[:, c0:c0 + gw]
        acc = pre * cw[CONV_K - 1:CONV_K] + convb_ref[:, c0:c0 + gw]
        for k in range(1, CONV_K):
            r = pltpu.roll(pre, k, axis=0)
            top = jnp.where(row8 < k, pltpu.roll(prev_tail, k, axis=0), r[0:SUBLANES])
            r = jnp.concatenate([top, r[SUBLANES:]], axis=0)
            acc = acc + r * cw[CONV_K - 1 - k:CONV_K - k]
        act = _silu(acc)
        if blk < n_xblk:
            xs_s[blk] = act
        else:
            dst = bm_s if blk < n_xblk + n_bblk else cm_s
            g0 = ((blk - n_xblk) % n_bblk) * (gw // D_STATE)
            for i in range(gw // D_STATE):
                dst[g0 + i] = act[:, i * D_STATE:(i + 1) * D_STATE].astype(BF16)

    dt_s[...] = _softplus(_dot(m_s[...], w_ref[:, off_dt:off_dt + LANES]) + dtb_ref[...])
    a_neg = -jnp.exp(alog_ref[...])

    rowc = lax.broadcasted_iota(jnp.int32, (CHUNK, LANES), 0)
    causal = (lax.broadcasted_iota(jnp.int32, (CHUNK, CHUNK), 0)
              >= lax.broadcasted_iota(jnp.int32, (CHUNK, CHUNK), 1))
    lane_g = lax.broadcasted_iota(jnp.int32, (CHUNK, gw), 1)
    lane_c = lax.broadcasted_iota(jnp.int32, (CHUNK, CHUNK), 1)

    def chunk_body(c, carry):
        r0 = pl.multiple_of(c * CHUNK, CHUNK)
        rows = pl.ds(r0, CHUNK)
        dt_c = dt_s[rows, :]
        acs = dt_c * a_neg
        sh = 1
        while sh < CHUNK:
            acs = acs + jnp.where(rowc >= sh, pltpu.roll(acs, sh, axis=0), 0.0)
            sh *= 2
        acs_x128 = jnp.concatenate(
            [jnp.broadcast_to(acs[:, h:h + 1], (CHUNK, CHUNK)) for h in range(N_HEADS)], axis=1)
        acs_x64 = jnp.concatenate(
            [jnp.where(lane_c < HEAD_DIM, acs_x128[:, (2 * i) * CHUNK:(2 * i + 1) * CHUNK],
                       acs_x128[:, (2 * i + 1) * CHUNK:(2 * i + 2) * CHUNK])
             for i in range(N_HEADS // 2)], axis=1)
        dt_x64 = _dot(_split3_bf16(dt_c[:, :N_HEADS]), e64_ref[...])
        acs_t = acs.T
        last = acs_x64[CHUNK - 1:CHUNK, :]
        grow = jnp.exp(acs_x64)
        to_end = jnp.exp(last - acs_x64)
        chunk_decay = jnp.exp(last)

        for g in range(N_GROUPS):
            cols = slice(g * gw, (g + 1) * gw)
            xs_g = xs_s[g, rows, :]
            bm_g = bm_s[g, rows, :]
            cm_g = cm_s[g, rows, :]
            xdt = xs_g * dt_x64[:, cols]
            cb = _dot_nt(cm_g, bm_g)
            ws, xr = [], []
            for r in range(HEADS_PER_GROUP):
                h = g * HEADS_PER_GROUP + r
                seg = acs_x128[:, h * CHUNK:(h + 1) * CHUNK] - acs_t[h:h + 1, :]
                ws.append((cb * jnp.exp(jnp.where(causal, seg, -jnp.inf))).astype(BF16))
                in_head = (lane_g >= r * HEAD_DIM) & (lane_g < (r + 1) * HEAD_DIM)
                xr.append(jnp.where(in_head, xdt, 0.0).astype(BF16))
            y_g = None
            for r in range(0, HEADS_PER_GROUP, 2):
                part = _dot(jnp.concatenate(ws[r:r + 2], axis=1), jnp.concatenate(xr[r:r + 2], axis=0))
                y_g = part if y_g is None else y_g + part
            st = state_s[g]
            y_g = y_g + _dot(cm_g, st.astype(BF16)) * grow[:, cols]
            state_s[g] = chunk_decay[:, cols] * st + _dot_tn(bm_g, (xdt * to_end[:, cols]).astype(BF16))
            y_s[rows, cols] = y_g + dskip_ref[:, cols] * xs_g
        return carry

    lax.fori_loop(0, n_chunks, chunk_body, 0)

    z = _dot(m_s[...], w_ref[:, off_z:off_z + d_inner])
    y = y_s[...] * _silu(z)
    y = y * lax.rsqrt(jnp.mean(y * y, -1, keepdims=True) + RMS_EPS) * normw_ref[...]
    y_a = _dot(y.astype(BF16), wpssd_ref[...])
    merged = jax.nn.sigmoid(_dot(m_s[...], w_ref[:, off_ga:off_ga + d_model])) * y_a

    uv = jax.nn.gelu(_dot(m_s[...], w_ref[:, off_uv:off_uv + 2 * d_model]))
    u = uv[:, :d_model]
    v = _layer_norm(uv[:, d_model:]) * glng_ref[...] + glnb_ref[...]
    v = v.astype(BF16)
    gd = d_model // GMLP_GROUPS
    ws_m = [jnp.where(causal, ws_ref[g], 0.0).astype(BF16) for g in range(GMLP_GROUPS)]
    v_rows = []
    for c in range(n_chunks):
        v_c = v[c * CHUNK:(c + 1) * CHUNK]
        v_rows.append(jnp.concatenate(
            [_dot(ws_m[g], v_c[:, g * gd:(g + 1) * gd]) for g in range(GMLP_GROUPS)], axis=1)
            + bsx_ref[...])
    v_mix = jnp.concatenate(v_rows, axis=0) if n_chunks > 1 else v_rows[0]
    y_b = _dot((u * v_mix).astype(BF16), wpgmlp_ref[...])
    merged = merged + jax.nn.sigmoid(_dot(m_s[...], w_ref[:, off_gb:off_gb + d_model])) * y_b

    mix = _dot(merged.astype(BF16), wout_ref[...])
    o_ref[0] = _layer_norm(DEEPNORM_ALPHA * x + g1 * mix) * ln1g_ref[...] + ln1b_ref[...]


def _const_spec(shape):
    nd = len(shape)
    return pl.BlockSpec(shape, lambda b, j: (0,) * nd, pipeline_mode=pl.Buffered(1))


def _mixer(x, mod1, w_in, conv_w, conv_b, dt_bias, a_log, d_skip, ssd_norm_w,
           gmlp_ln_g, gmlp_ln_b, gmlp_ws, gmlp_bs, w_proj_ssd, w_proj_gmlp, w_out, ln1_g, ln1_b):
    bsz, seq, d = x.shape
    d_inner = N_HEADS * HEAD_DIM
    gn = N_GROUPS * D_STATE
    d_xbc = d_inner + 2 * gn
    tile = min(MIX_TILE, seq)
    assert seq % tile == 0 and tile % CHUNK == 0 and d_inner == 2 * d

    s0, s1, s2, s3, s4 = (d_inner, d_inner + d_xbc, d_inner + d_xbc + N_HEADS,
                          d_inner + d_xbc + N_HEADS + 2 * d, d_inner + d_xbc + N_HEADS + 3 * d)
    w_dt = jnp.pad(w_in[:, s1:s2], ((0, 0), (0, LANES - N_HEADS)))
    w_all = jnp.concatenate([w_in[:, :s1], w_in[:, s2:], w_dt], axis=1).astype(BF16)
    off_z, off_xbc = 0, d_inner
    off_uv = s1
    off_ga = off_uv + 2 * d
    off_gb = off_ga + d
    off_dt = off_gb + d
    offs = (off_z, off_xbc, off_uv, off_ga, off_gb, off_dt)

    pad_h = lambda v: jnp.pad(v.reshape(1, N_HEADS), ((0, 0), (0, LANES - N_HEADS)))
    head_of_col64 = jnp.arange(d_inner) // HEAD_DIM
    piece_head = jnp.where(jnp.arange(LANES) < 3 * N_HEADS, jnp.arange(LANES) % N_HEADS, -1)
    e64 = (piece_head[:, None] == head_of_col64[None, :]).astype(BF16)
    dskip_x = jnp.broadcast_to(d_skip[:, None], (N_HEADS, HEAD_DIM)).reshape(1, d_inner)
    bs_x = jnp.broadcast_to(gmlp_bs.T[:, :, None], (CHUNK, GMLP_GROUPS, d // GMLP_GROUPS)).reshape(CHUNK, d)

    operands = [
        w_all, conv_w, conv_b.reshape(1, d_xbc), pad_h(dt_bias), pad_h(a_log), dskip_x,
        ssd_norm_w.reshape(1, d_inner), gmlp_ln_g.reshape(1, d), gmlp_ln_b.reshape(1, d),
        gmlp_ws, bs_x, e64,
        w_proj_ssd.astype(BF16), w_proj_gmlp.astype(BF16), w_out.astype(BF16),
        ln1_g.reshape(1, d), ln1_b.reshape(1, d),
    ]
    gw = HEADS_PER_GROUP * HEAD_DIM
    kern = functools.partial(_mixer_kernel, tile=tile, d_model=d, d_inner=d_inner, offs=offs)
    return pl.pallas_call(
        kern,
        grid=(bsz, seq // tile),
        in_specs=[pl.BlockSpec((1, tile, d), lambda b, j: (b, j, 0)),
                  pl.BlockSpec((1, SUBLANES, d), lambda b, j: (b, 0, 0))]
                 + [_const_spec(op.shape) for op in operands],
        out_specs=pl.BlockSpec((1, tile, d), lambda b, j: (b, j, 0)),
        out_shape=jax.ShapeDtypeStruct((bsz, seq, d), F32),
        scratch_shapes=[
            pltpu.VMEM((tile, d), BF16),
            pltpu.VMEM((N_GROUPS, tile, gw), F32),
            pltpu.VMEM((N_GROUPS, tile, D_STATE), BF16),
            pltpu.VMEM((N_GROUPS, tile, D_STATE), BF16),
            pltpu.VMEM((tile, LANES), F32),
            pltpu.VMEM((tile, d_inner), F32),
            pltpu.VMEM((N_GROUPS, D_STATE, gw), F32),
            pltpu.VMEM((SUBLANES, d_xbc), F32),
        ],
        compiler_params=pltpu.CompilerParams(
            dimension_semantics=("arbitrary", "arbitrary"),
            vmem_limit_bytes=MIXER_VMEM_LIMIT),
        name="mixer",
    )(x, mod1, *operands)


def _route_kernel(h_ref, mod_ref, wr_ref, bias_ref, topi_ref, topw_ref, rank_ref, cnt_ref, cnt_s,
                  *, tile):
    @pl.when(pl.program_id(0) == 0)
    def _():
        cnt_s[...] = jnp.zeros_like(cnt_s)

    mod = mod_ref[0]
    m2 = _layer_norm(h_ref[...]) * (1.0 + mod[1:2]) + mod[0:1]
    scores = jax.nn.sigmoid(_dot_nt(wr_ref[...], m2.astype(BF16)))
    choice = scores + bias_ref[...]

    neg = -jnp.inf
    c3 = choice.reshape(N_EXPERT_GROUPS, EXPERTS_PER_GROUP, tile)
    i3 = lax.broadcasted_iota(jnp.int32, c3.shape, 1).astype(F32)
    m1 = jnp.max(c3, axis=1, keepdims=True)
    i1 = jnp.min(jnp.where(c3 == m1, i3, float(EXPERTS_PER_GROUP)), axis=1, keepdims=True)
    second = jnp.max(jnp.where(i3 == i1, neg, c3), axis=1)
    gs = m1[:, 0, :] + second

    gi = lax.broadcasted_iota(jnp.int32, gs.shape, 0).astype(F32)
    gsel = jnp.zeros(gs.shape, F32)
    for _ in range(TOPK_GROUPS):
        mx = jnp.max(gs, axis=0, keepdims=True)
        ix = jnp.min(jnp.where(gs == mx, gi, float(N_EXPERT_GROUPS)), axis=0, keepdims=True)
        hit = gi == ix
        gsel = jnp.where(hit, 1.0, gsel)
        gs = jnp.where(hit, neg, gs)
    emask = jnp.broadcast_to(gsel[:, None, :], c3.shape).reshape(N_EXPERTS, tile) > 0.0
    masked = jnp.where(emask, choice, neg)

    ei = lax.broadcasted_iota(jnp.int32, masked.shape, 0).astype(F32)
    hits, idx_rows, w_rows = [], [], []
    for _ in range(TOP_K):
        mx = jnp.max(masked, axis=0, keepdims=True)
        ix = jnp.min(jnp.where(masked == mx, ei, float(N_EXPERTS)), axis=0, keepdims=True)
        hit = ei == ix
        hits.append(hit)
        idx_rows.append(ix)
        w_rows.append(jnp.sum(jnp.where(hit, scores, 0.0), axis=0, keepdims=True))
        masked = jnp.where(hit, neg, masked)
    w_all = jnp.concatenate(w_rows, axis=0)
    topw_ref[...] = w_all / (jnp.sum(w_all, axis=0, keepdims=True) + 1e-20) * ROUTED_SCALE
    topi_ref[...] = jnp.concatenate(idx_rows, axis=0).astype(jnp.int32)

    assign = jnp.zeros(masked.shape, F32)
    for hit in hits:
        assign = jnp.where(hit, 1.0, assign)
    assign_b = assign.astype(BF16)
    t0 = lax.broadcasted_iota(jnp.int32, (tile, tile), 0)
    t1 = lax.broadcasted_iota(jnp.int32, (tile, tile), 1)
    before = jnp.where(t0 < t1, 1.0, 0.0).astype(BF16)
    base = cnt_s[...]
    pos = _dot(assign_b, before) + jnp.concatenate([base] * (tile // LANES), axis=1)
    rank_ref[...] = jnp.concatenate(
        [jnp.sum(jnp.where(hit, pos, 0.0), axis=0, keepdims=True) for hit in hits],
        axis=0).astype(jnp.int32)
    cnt_s[...] = base + _dot(assign_b, jnp.ones((tile, LANES), BF16))
    cnt_ref[...] = cnt_s[...]


def _route(h1, mod2, w_router, router_bias, tokens_per_batch):
    t, d = h1.shape
    tile = min(ROUTE_TILE, tokens_per_batch)
    assert tokens_per_batch % tile == 0 and tile % LANES == 0
    per_b = tokens_per_batch // tile
    wr_t = w_router.T.astype(BF16)
    bias_x = jnp.broadcast_to(router_bias.reshape(N_EXPERTS, 1), (N_EXPERTS, tile)).astype(F32)
    kern = functools.partial(_route_kernel, tile=tile)
    row_spec = pl.BlockSpec((TOP_K, tile), lambda i: (0, i))
    return pl.pallas_call(
        kern,
        grid=(t // tile,),
        in_specs=[pl.BlockSpec((tile, d), lambda i: (i, 0)),
                  pl.BlockSpec((1, SUBLANES, d), lambda i: (i // per_b, 0, 0)),
                  pl.BlockSpec((N_EXPERTS, d), lambda i: (0, 0)),
                  pl.BlockSpec((N_EXPERTS, tile), lambda i: (0, 0))],
        out_specs=[row_spec, row_spec, row_spec,
                   pl.BlockSpec((N_EXPERTS, LANES), lambda i: (0, 0))],
        out_shape=[jax.ShapeDtypeStruct((TOP_K, t), jnp.int32),
                   jax.ShapeDtypeStruct((TOP_K, t), F32),
                   jax.ShapeDtypeStruct((TOP_K, t), jnp.int32),
                   jax.ShapeDtypeStruct((N_EXPERTS, LANES), F32)],
        scratch_shapes=[pltpu.VMEM((N_EXPERTS, LANES), F32)],
        compiler_params=pltpu.CompilerParams(dimension_semantics=("arbitrary",)),
        name="route",
    )(h1, mod2, wr_t, bias_x)


def _slots_kernel(topi_ref, rank_ref, start_ref, dest_ref, *, tile):
    ei = lax.broadcasted_iota(jnp.int32, (N_EXPERTS, tile), 0)
    start = jnp.concatenate([start_ref[...]] * (tile // LANES), axis=1)
    topi = topi_ref[...]
    rows = [jnp.sum(jnp.where(ei == topi[k:k + 1, :], start, 0.0), axis=0, keepdims=True)
            for k in range(TOP_K)]
    dest_ref[...] = jnp.concatenate(rows, axis=0).astype(jnp.int32) + rank_ref[...]


def _slots(topi_t, rank_t, pad_start):
    t = topi_t.shape[1]
    tile = min(SLOTS_TILE, t)
    start_x = jnp.broadcast_to(pad_start.astype(F32).reshape(N_EXPERTS, 1), (N_EXPERTS, LANES))
    row_spec = pl.BlockSpec((TOP_K, tile), lambda i: (0, i))
    return pl.pallas_call(
        functools.partial(_slots_kernel, tile=tile),
        grid=(t // tile,),
        in_specs=[row_spec, row_spec, pl.BlockSpec((N_EXPERTS, LANES), lambda i: (0, 0))],
        out_specs=row_spec,
        out_shape=jax.ShapeDtypeStruct((TOP_K, t), jnp.int32),
        name="slots",
    )(topi_t, rank_t, start_x)


def _to_slabs(slab_ref, x):
    rows, width = x.shape
    n = width // LANES
    for j in range(n):
        slab_ref[pl.ds(j, rows, stride=n), :] = x[:, j * LANES:(j + 1) * LANES]


def _from_slabs(slab_ref, rows, n):
    return jnp.concatenate([slab_ref[pl.ds(j, rows, stride=n), :] for j in range(n)], axis=1)


def _slab(ref, r, n):
    return ref.at[pl.ds(pl.multiple_of(r * n, n), n), :]


def _pack_bf16_pairs(x):
    n = x.shape[1] // 2
    bits = lax.bitcast_convert_type(x.astype(BF16).astype(F32), jnp.uint32)
    return bits[:, n:] | (bits[:, :n] >> 16)


def _unpack_bf16_pairs(p):
    lo = lax.bitcast_convert_type(p << 16, F32)
    hi = lax.bitcast_convert_type(p & jnp.uint32(0xFFFF0000), F32)
    return jnp.concatenate([lo, hi], axis=1).astype(BF16)


def _scatter_kernel(zflag_ref, dest_ref, h_ref, mod_ref, out_ref, slab_s, zero_s, sem, zsem,
                    *, tile, n, n_steps, n_blocks):
    i = pl.program_id(0)
    slot = i % 2

    @pl.when(i == 0)
    def _():
        zero_s[...] = jnp.zeros_like(zero_s)
        brows = EXPERT_BLOCK * n

        def zero_copy(b):
            return pltpu.make_async_copy(
                zero_s, out_ref.at[pl.ds(pl.multiple_of(b * brows, brows), brows), :], zsem)

        def start(b, carry):
            @pl.when(zflag_ref[b] > 0)
            def _():
                zero_copy(b).start()
            return carry

        def wait(b, carry):
            @pl.when(zflag_ref[b] > 0)
            def _():
                zero_copy(b).wait()
            return carry

        lax.fori_loop(0, n_blocks, start, 0)
        lax.fori_loop(0, n_blocks, wait, 0)

    mod = mod_ref[0]
    _to_slabs(slab_s.at[slot],
              _pack_bf16_pairs(_layer_norm(h_ref[...]) * (1.0 + mod[1:2]) + mod[0:1]))

    def issue(t, carry):
        src = _slab(slab_s.at[slot], t, n)
        for k in range(TOP_K):
            d = dest_ref[t * TOP_K + k]
            pltpu.make_async_copy(src, _slab(out_ref, d, n), sem.at[slot]).start(priority=k % 2)
        return carry

    lax.fori_loop(0, tile, issue, 0)

    def drain(s):
        for _ in range(TOP_K):
            pltpu.make_async_copy(slab_s.at[s], out_ref.at[pl.ds(0, tile * n), :], sem.at[s]).wait()

    @pl.when(i >= 1)
    def _():
        drain(1 - slot)

    @pl.when(i == n_steps - 1)
    def _():
        drain(slot)


def _scatter(h1, mod2, dest_flat, zero_flag, tokens_per_batch):
    t, d = h1.shape
    n = d // 2 // LANES
    n_blocks = zero_flag.shape[0]
    tile = min(SCATTER_TILE, tokens_per_batch)
    per_b = tokens_per_batch // tile
    kern = functools.partial(_scatter_kernel, tile=tile, n=n, n_steps=t // tile, n_blocks=n_blocks)
    grid_spec = pltpu.PrefetchScalarGridSpec(
        num_scalar_prefetch=1,
        grid=(t // tile,),
        in_specs=[pl.BlockSpec((tile * TOP_K,), lambda i, zf: (i,), memory_space=pltpu.SMEM),
                  pl.BlockSpec((tile, d), lambda i, zf: (i, 0)),
                  pl.BlockSpec((1, SUBLANES, d), lambda i, zf: (i // per_b, 0, 0))],
        out_specs=pl.BlockSpec(memory_space=pl.ANY),
        scratch_shapes=[pltpu.VMEM((2, tile * n, LANES), jnp.uint32),
                        pltpu.VMEM((EXPERT_BLOCK * n, LANES), jnp.uint32),
                        pltpu.SemaphoreType.DMA((2,)), pltpu.SemaphoreType.DMA(())],
    )
    return pl.pallas_call(
        kern,
        grid_spec=grid_spec,
        out_shape=jax.ShapeDtypeStruct((n_blocks * EXPERT_BLOCK * n, LANES), jnp.uint32),
        compiler_params=pltpu.CompilerParams(dimension_semantics=("arbitrary",),
                                             has_side_effects=True),
        name="scatter",
    )(zero_flag, dest_flat, h1, mod2)


def _expert_kernel(blk0_ref, nblk_ref, x_ref, wg_ref, wu_ref, wd_ref, y_ref,
                   xbuf_s, ybuf_s, wg_s, wu_s, wd_s, in_sem, out_sem, *, nx, ny):
    e = pl.program_id(0)
    blk0 = blk0_ref[e]
    nblk = nblk_ref[e]
    used = blk0_ref[N_EXPERTS]
    rows = EXPERT_BLOCK
    ahead = EXPERT_IN_BUFS - 1

    def in_copy(gb):
        slot = gb % EXPERT_IN_BUFS
        return pltpu.make_async_copy(x_ref.at[pl.ds(pl.multiple_of(gb * rows * nx, rows * nx), rows * nx), :],
                                     xbuf_s.at[slot], in_sem.at[slot])

    def out_copy(gb):
        slot = gb % EXPERT_OUT_BUFS
        return pltpu.make_async_copy(ybuf_s.at[slot],
                                     y_ref.at[pl.ds(pl.multiple_of(gb * rows * ny, rows * ny), rows * ny), :],
                                     out_sem.at[slot])

    @pl.when(e == 0)
    def _():
        for b in range(ahead):
            @pl.when(b < used)
            def _():
                in_copy(b).start()

    @pl.when((e < N_EXPERTS) & (nblk > 0))
    def _():
        wg_s[...] = wg_ref[0].astype(BF16)
        wu_s[...] = wu_ref[0].astype(BF16)
        wd_s[...] = wd_ref[0].astype(BF16)

        def block(c, carry):
            gb = blk0 + c
            in_copy(gb).wait()

            @pl.when(gb + ahead < used)
            def _():
                in_copy(gb + ahead).start()

            @pl.when(gb >= EXPERT_OUT_BUFS)
            def _():
                out_copy(gb - EXPERT_OUT_BUFS).wait()

            xb = _unpack_bf16_pairs(_from_slabs(xbuf_s.at[gb % EXPERT_IN_BUFS], rows, nx))
            hid = (_silu(_dot(xb, wg_s[...])) * _dot(xb, wu_s[...])).astype(BF16)
            _to_slabs(ybuf_s.at[gb % EXPERT_OUT_BUFS], _dot(hid, wd_s[...]))
            out_copy(gb).start()
            return carry

        lax.fori_loop(0, nblk, block, 0)

    @pl.when(e == N_EXPERTS)
    def _():
        for b in range(EXPERT_OUT_BUFS, 0, -1):
            @pl.when(used >= b)
            def _():
                out_copy(used - b).wait()

        @pl.when(nblk > 0)
        def _():
            ybuf_s[0] = jnp.zeros(ybuf_s.shape[1:], F32)

            def zero_block(c, carry):
                gb = blk0 + c
                cp = pltpu.make_async_copy(
                    ybuf_s.at[0],
                    y_ref.at[pl.ds(pl.multiple_of(gb * rows * ny, rows * ny), rows * ny), :],
                    out_sem.at[0])
                cp.start()
                cp.wait()
                return carry

            lax.fori_loop(0, nblk, zero_block, 0)


def _experts(xs_slabs, blk0, nblk, w_e_gate, w_e_up, w_e_down):
    d, ff = w_e_gate.shape[-2:]
    nx = d // 2 // LANES
    ny = d // LANES
    n_slots = xs_slabs.shape[0] // nx
    rows = EXPERT_BLOCK
    w_idx = lambda e, b0, nb: (jnp.minimum(e, N_EXPERTS - 1), 0, 0)
    grid_spec = pltpu.PrefetchScalarGridSpec(
        num_scalar_prefetch=2,
        grid=(N_EXPERTS + 1,),
        in_specs=[pl.BlockSpec(memory_space=pl.ANY),
                  pl.BlockSpec((1, d, ff), w_idx),
                  pl.BlockSpec((1, d, ff), w_idx),
                  pl.BlockSpec((1, ff, d), w_idx)],
        out_specs=pl.BlockSpec(memory_space=pl.ANY),
        scratch_shapes=[pltpu.VMEM((EXPERT_IN_BUFS, rows * nx, LANES), jnp.uint32),
                        pltpu.VMEM((EXPERT_OUT_BUFS, rows * ny, LANES), F32),
                        pltpu.VMEM((d, ff), BF16), pltpu.VMEM((d, ff), BF16),
                        pltpu.VMEM((ff, d), BF16),
                        pltpu.SemaphoreType.DMA((EXPERT_IN_BUFS,)),
                        pltpu.SemaphoreType.DMA((EXPERT_OUT_BUFS,))],
    )
    return pl.pallas_call(
        functools.partial(_expert_kernel, nx=nx, ny=ny),
        grid_spec=grid_spec,
        out_shape=jax.ShapeDtypeStruct((n_slots * ny, LANES), F32),
        compiler_params=pltpu.CompilerParams(dimension_semantics=("arbitrary",),
                                             has_side_effects=True),
        name="experts",
    )(blk0, nblk, xs_slabs, w_e_gate, w_e_up, w_e_down)


def _combine_kernel(dest_ref, dest_next_ref, h_ref, mod_ref, w_ref, y_ref, wsg_ref, wsu_ref, wsd_ref,
                    ln2g_ref, ln2b_ref, o_ref, ybuf_s, sem, *, tile, n, n_steps):
    i = pl.program_id(0)
    slot = i % 2

    def issue_tile(dref, s):
        def issue(t, carry):
            for k in range(TOP_K):
                d = dref[t * TOP_K + k]
                pltpu.make_async_copy(_slab(y_ref, d, n), _slab(ybuf_s.at[s, k], t, n),
                                      sem.at[s]).start(priority=k % 2)
            return carry
        lax.fori_loop(0, tile, issue, 0)

    @pl.when(i == 0)
    def _():
        issue_tile(dest_ref, 0)

    @pl.when(i + 1 < n_steps)
    def _():
        issue_tile(dest_next_ref, 1 - slot)

    h = h_ref[...]
    mod = mod_ref[0]
    m2 = (_layer_norm(h) * (1.0 + mod[1:2]) + mod[0:1]).astype(BF16)
    acc = _dot((_silu(_dot(m2, wsg_ref[...])) * _dot(m2, wsu_ref[...])).astype(BF16), wsd_ref[...])

    for k in range(TOP_K):
        pltpu.make_async_copy(y_ref.at[pl.ds(0, tile * n), :], ybuf_s.at[slot, k],
                              sem.at[slot]).wait()
    w = w_ref[...]
    for k in range(TOP_K):
        acc = acc + _from_slabs(ybuf_s.at[slot, k], tile, n) * w[:, k:k + 1]
    o_ref[...] = _layer_norm(DEEPNORM_ALPHA * h + mod[2:3] * acc) * ln2g_ref[...] + ln2b_ref[...]


def _combine(h1, mod2, dest_flat, topw, y_slabs, w_sh_gate, w_sh_up, w_sh_down, ln2_g, ln2_b,
             tokens_per_batch):
    t, d = h1.shape
    n = d // LANES
    ff = w_sh_gate.shape[-1]
    tile = min(COMBINE_TILE, tokens_per_batch)
    per_b = tokens_per_batch // tile
    n_steps = t // tile
    kern = functools.partial(_combine_kernel, tile=tile, n=n, n_steps=n_steps)
    const = lambda shape: pl.BlockSpec(shape, lambda i: (0,) * len(shape))
    return pl.pallas_call(
        kern,
        grid=(n_steps,),
        in_specs=[pl.BlockSpec((tile * TOP_K,), lambda i: (i,), memory_space=pltpu.SMEM),
                  pl.BlockSpec((tile * TOP_K,), lambda i: (jnp.minimum(i + 1, n_steps - 1),),
                               memory_space=pltpu.SMEM),
                  pl.BlockSpec((tile, d), lambda i: (i, 0)),
                  pl.BlockSpec((1, SUBLANES, d), lambda i: (i // per_b, 0, 0)),
                  pl.BlockSpec((tile, TOP_K), lambda i: (i, 0)),
                  pl.BlockSpec(memory_space=pl.ANY),
                  const((d, ff)), const((d, ff)), const((ff, d)), const((1, d)), const((1, d))],
        out_specs=pl.BlockSpec((tile, d), lambda i: (i, 0)),
        out_shape=jax.ShapeDtypeStruct((t, d), F32),
        scratch_shapes=[pltpu.VMEM((2, TOP_K, tile * n, LANES), F32),
                        pltpu.SemaphoreType.DMA((2,))],
        compiler_params=pltpu.CompilerParams(dimension_semantics=("arbitrary",)),
        name="combine",
    )(dest_flat, dest_flat, h1, mod2, topw, y_slabs, w_sh_gate.astype(BF16), w_sh_up.astype(BF16),
      w_sh_down.astype(BF16), ln2_g.reshape(1, d), ln2_b.reshape(1, d))


def _moe(h1, mod2, w_router, router_bias, w_e_gate, w_e_up, w_e_down,
         w_sh_gate, w_sh_up, w_sh_down, ln2_g, ln2_b, tokens_per_batch):
    t, d = h1.shape
    topi_t, topw_t, rank_t, cnt = _route(h1, mod2, w_router, router_bias, tokens_per_batch)

    blk = EXPERT_BLOCK
    nb = -(-(t * TOP_K) // blk) + N_EXPERTS
    counts = cnt[:, 0].astype(jnp.int32)
    padded = (counts + blk - 1) // blk * blk
    pad_end = jnp.cumsum(padded)
    pad_start = pad_end - padded
    dest_flat = _slots(topi_t, rank_t, pad_start).T.reshape(t * TOP_K)
    used = pad_end[-1] // blk
    blk0 = jnp.concatenate([pad_start // blk, used[None]]).astype(jnp.int32)
    nblk = jnp.concatenate([padded // blk, (nb - used)[None]]).astype(jnp.int32)

    block_id = jnp.arange(nb, dtype=jnp.int32)
    is_last = jnp.any((block_id[:, None] == (pad_end // blk - 1)[None, :]) & (padded > 0)[None, :], axis=1)
    zero_flag = (is_last | (block_id >= used)).astype(jnp.int32)

    xs_slabs = _scatter(h1, mod2, dest_flat, zero_flag, tokens_per_batch)
    y_slabs = _experts(xs_slabs, blk0, nblk, w_e_gate, w_e_up, w_e_down)
    return _combine(h1, mod2, dest_flat, topw_t.T, y_slabs, w_sh_gate, w_sh_up, w_sh_down,
                    ln2_g, ln2_b, tokens_per_batch)


def _pad_rows(parts, d):
    rows = jnp.stack(parts, axis=1)
    return jnp.pad(rows, ((0, 0), (0, SUBLANES - rows.shape[1]), (0, 0)))


def kernel(x, c, w_ada, b_ada, w_in, conv_w, conv_b, dt_bias, a_log, d_skip, ssd_norm_w, gmlp_ln_g, gmlp_ln_b, gmlp_ws, gmlp_bs, w_proj_ssd, w_proj_gmlp, w_out, ln1_g, ln1_b, w_router, router_bias, w_e_gate, w_e_up, w_e_down, w_sh_gate, w_sh_up, w_sh_down, ln2_g, ln2_b):
    bsz, seq, d = x.shape
    h = x
    for i in range(w_ada.shape[0]):
        ada = _ada(c, w_ada[i], b_ada[i])
        sh1, sc1, g1, sh2, sc2, g2 = jnp.split(ada, 6, axis=-1)
        h = _mixer(h, _pad_rows([sh1, sc1, g1], d), w_in[i], conv_w[i], conv_b[i], dt_bias[i],
                   a_log[i], d_skip[i], ssd_norm_w[i], gmlp_ln_g[i], gmlp_ln_b[i], gmlp_ws[i],
                   gmlp_bs[i], w_proj_ssd[i], w_proj_gmlp[i], w_out[i], ln1_g[i], ln1_b[i])
        h = _moe(h.reshape(bsz * seq, d), _pad_rows([sh2, sc2, g2], d), w_router[i],
                 router_bias[i], w_e_gate[i], w_e_up[i], w_e_down[i], w_sh_gate[i], w_sh_up[i],
                 w_sh_down[i], ln2_g[i], ln2_b[i], seq).reshape(bsz, seq, d)
    return h
```

```python
import functools

import jax
import jax.numpy as jnp
from jax import lax
from jax.experimental import pallas as pl
from jax.experimental.pallas import tpu as pltpu

F32 = jnp.float32
BF16 = jnp.bfloat16

HEAD_DIM = 64
N_GROUPS = 8
HEADS_PER_GROUP = 4
N_HEADS = N_GROUPS * HEADS_PER_GROUP
D_STATE = 128
CONV_K = 4
CHUNK = 128
GMLP_GROUPS = 8
N_EXPERTS = 256
TOP_K = 8
N_EXPERT_GROUPS = 8
EXPERTS_PER_GROUP = N_EXPERTS // N_EXPERT_GROUPS
TOPK_GROUPS = 4
ROUTED_SCALE = 2.5
DEPTH = 1
DEEPNORM_ALPHA = (2 * DEPTH) ** 0.25
LN_EPS = 1e-5
RMS_EPS = 1e-5

LANES = 128
SUBLANES = 8
VMEM_BYTES_V7X = 64 * 1024 * 1024
MIXER_VMEM_LIMIT = VMEM_BYTES_V7X - 8 * 1024 * 1024

MIX_TILE = 256
ROUTE_TILE = 256
SCATTER_TILE = 256
EXPERT_BLOCK = 256
EXPERT_IN_BUFS = 4
EXPERT_OUT_BUFS = 4
COMBINE_TILE = 128
SLOTS_TILE = 512


def _dot(a, b):
    return jnp.dot(a, b, preferred_element_type=F32)


def _dot_nt(a, b):
    return lax.dot_general(a, b, (((1,), (1,)), ((), ())), preferred_element_type=F32)


def _dot_tn(a, b):
    return lax.dot_general(a, b, (((0,), (0,)), ((), ())), preferred_element_type=F32)


def _layer_norm(x):
    xc = x - jnp.mean(x, -1, keepdims=True)
    var = jnp.mean(xc * xc, -1, keepdims=True)
    return xc * lax.rsqrt(var + LN_EPS)


def _silu(x):
    return x * jax.nn.sigmoid(x)


def _softplus(x):
    return jnp.maximum(x, 0.0) + jnp.log1p(jnp.exp(-jnp.abs(x)))


def _ada_kernel(c_ref, w_ref, b_ref, o_ref):
    s = _silu(c_ref[...]).astype(BF16)
    o_ref[...] = _dot(s, w_ref[...].astype(BF16)) + b_ref[...]


def _ada(c, w_ada, b_ada):
    bsz, d = c.shape
    n = w_ada.shape[1]
    tn = 1024
    return pl.pallas_call(
        _ada_kernel,
        grid=(n // tn,),
        in_specs=[pl.BlockSpec((bsz, d), lambda j: (0, 0)),
                  pl.BlockSpec((d, tn), lambda j: (0, j)),
                  pl.BlockSpec((1, tn), lambda j: (0, j))],
        out_specs=pl.BlockSpec((bsz, tn), lambda j: (0, j)),
        out_shape=jax.ShapeDtypeStruct((bsz, n), F32),
        name="ada",
    )(c, w_ada, b_ada.reshape(1, n))


def _split3_bf16(a):
    hi = a.astype(BF16).astype(F32)
    r1 = a - hi
    mid = r1.astype(BF16).astype(F32)
    lo = r1 - mid
    pad = jnp.zeros((a.shape[0], LANES - 3 * N_HEADS), F32)
    return jnp.concatenate([hi, mid, lo, pad], axis=1).astype(BF16)


def _mixer_kernel(x_ref, mod_ref, w_ref, convw_ref, convb_ref, dtb_ref, alog_ref, dskip_ref,
                  normw_ref, glng_ref, glnb_ref, ws_ref, bsx_ref, e64_ref,
                  wpssd_ref, wpgmlp_ref, wout_ref, ln1g_ref, ln1b_ref,
                  o_ref,
                  m_s, xs_s, bm_s, cm_s, dt_s, y_s, state_s, tail_s,
                  *, tile, d_model, d_inner, offs):
    off_z, off_xbc, off_uv, off_ga, off_gb, off_dt = offs
    n_chunks = tile // CHUNK
    gw = HEADS_PER_GROUP * HEAD_DIM

    @pl.when(pl.program_id(1) == 0)
    def _():
        state_s[...] = jnp.zeros_like(state_s)
        tail_s[...] = jnp.zeros_like(tail_s)

    x = x_ref[0]
    mod = mod_ref[0]
    sh1, sc1, g1 = mod[0:1], mod[1:2], mod[2:3]
    m_s[...] = (_layer_norm(x) * (1.0 + sc1) + sh1).astype(BF16)

    row8 = lax.broadcasted_iota(jnp.int32, (SUBLANES, gw), 0)
    n_xblk = d_inner // gw
    n_bblk = N_GROUPS * D_STATE // gw
    for blk in range(n_xblk + 2 * n_bblk):
        c0 = blk * gw
        pre = _dot(m_s[...], w_ref[:, off_xbc + c0:off_xbc + c0 + gw])
        prev_tail = tail_s[:, c0:c0 + gw]
        tail_s[:, c0:c0 + gw] = pre[tile - SUBLANES:tile]
        cw = convw_ref[:, c0:c0 + gw]
        acc = pre * cw[CONV_K - 1:CONV_K] + convb_ref[:, c0:c0 + gw]
        for k in range(1, CONV_K):
            r = pltpu.roll(pre, k, axis=0)
            top = jnp.where(row8 < k, pltpu.roll(prev_tail, k, axis=0), r[0:SUBLANES])
            r = jnp.concatenate([top, r[SUBLANES:]], axis=0)
            acc = acc + r * cw[CONV_K - 1 - k:CONV_K - k]
        act = _silu(acc)
        if blk < n_xblk:
            xs_s[blk] = act
        else:
            dst = bm_s if blk < n_xblk + n_bblk else cm_s
            g0 = ((blk - n_xblk) % n_bblk) * (gw // D_STATE)
            for i in range(gw // D_STATE):
                dst[g0 + i] = act[:, i * D_STATE:(i + 1) * D_STATE].astype(BF16)

    dt_s[...] = _softplus(_dot(m_s[...], w_ref[:, off_dt:off_dt + LANES]) + dtb_ref[...])
    a_neg = -jnp.exp(alog_ref[...])

    rowc = lax.broadcasted_iota(jnp.int32, (CHUNK, LANES), 0)
    causal = (lax.broadcasted_iota(jnp.int32, (CHUNK, CHUNK), 0)
              >= lax.broadcasted_iota(jnp.int32, (CHUNK, CHUNK), 1))
    lane_g = lax.broadcasted_iota(jnp.int32, (CHUNK, gw), 1)
    lane_c = lax.broadcasted_iota(jnp.int32, (CHUNK, CHUNK), 1)

    def chunk_body(c, carry):
        r0 = pl.multiple_of(c * CHUNK, CHUNK)
        rows = pl.ds(r0, CHUNK)
        dt_c = dt_s[rows, :]
        acs = dt_c * a_neg
        sh = 1
        while sh < CHUNK:
            acs = acs + jnp.where(rowc >= sh, pltpu.roll(acs, sh, axis=0), 0.0)
            sh *= 2
        acs_x128 = jnp.concatenate(
            [jnp.broadcast_to(acs[:, h:h + 1], (CHUNK, CHUNK)) for h in range(N_HEADS)], axis=1)
        acs_x64 = jnp.concatenate(
            [jnp.where(lane_c < HEAD_DIM, acs_x128[:, (2 * i) * CHUNK:(2 * i + 1) * CHUNK],
                       acs_x128[:, (2 * i + 1) * CHUNK:(2 * i + 2) * CHUNK])
             for i in range(N_HEADS // 2)], axis=1)
        dt_x64 = _dot(_split3_bf16(dt_c[:, :N_HEADS]), e64_ref[...])
        acs_t = acs.T
        last = acs_x64[CHUNK - 1:CHUNK, :]
        grow = jnp.exp(acs_x64)
        to_end = jnp.exp(last - acs_x64)
        chunk_decay = jnp.exp(last)

        for g in range(N_GROUPS):
            cols = slice(g * gw, (g + 1) * gw)
            xs_g = xs_s[g, rows, :]
            bm_g = bm_s[g, rows, :]
            cm_g = cm_s[g, rows, :]
            xdt = xs_g * dt_x64[:, cols]
            cb = _dot_nt(cm_g, bm_g)
            ws, xr = [], []
            for r in range(HEADS_PER_GROUP):
                h = g * HEADS_PER_GROUP + r
                seg = acs_x128[:, h * CHUNK:(h + 1) * CHUNK] - acs_t[h:h + 1, :]
                ws.append((cb * jnp.exp(jnp.where(causal, seg, -jnp.inf))).astype(BF16))
                in_head = (lane_g >= r * HEAD_DIM) & (lane_g < (r + 1) * HEAD_DIM)
                xr.append(jnp.where(in_head, xdt, 0.0).astype(BF16))
            y_g = None
            for r in range(0, HEADS_PER_GROUP, 2):
                part = _dot(jnp.concatenate(ws[r:r + 2], axis=1), jnp.concatenate(xr[r:r + 2], axis=0))
                y_g = part if y_g is None else y_g + part
            st = state_s[g]
            y_g = y_g + _dot(cm_g, st.astype(BF16)) * grow[:, cols]
            state_s[g] = chunk_decay[:, cols] * st + _dot_tn(bm_g, (xdt * to_end[:, cols]).astype(BF16))
            y_s[rows, cols] = y_g + dskip_ref[:, cols] * xs_g
        return carry

    lax.fori_loop(0, n_chunks, chunk_body, 0)

    z = _dot(m_s[...], w_ref[:, off_z:off_z + d_inner])
    y = y_s[...] * _silu(z)
    y = y * lax.rsqrt(jnp.mean(y * y, -1, keepdims=True) + RMS_EPS) * normw_ref[...]
    y_a = _dot(y.astype(BF16), wpssd_ref[...])
    merged = jax.nn.sigmoid(_dot(m_s[...], w_ref[:, off_ga:off_ga + d_model])) * y_a

    uv = jax.nn.gelu(_dot(m_s[...], w_ref[:, off_uv:off_uv + 2 * d_model]))
    u = uv[:, :d_model]
    v = _layer_norm(uv[:, d_model:]) * glng_ref[...] + glnb_ref[...]
    v = v.astype(BF16)
    gd = d_model // GMLP_GROUPS
    ws_m = [jnp.where(causal, ws_ref[g], 0.0).astype(BF16) for g in range(GMLP_GROUPS)]
    v_rows = []
    for c in range(n_chunks):
        v_c = v[c * CHUNK:(c + 1) * CHUNK]
        v_rows.append(jnp.concatenate(
            [_dot(ws_m[g], v_c[:, g * gd:(g + 1) * gd]) for g in range(GMLP_GROUPS)], axis=1)
            + bsx_ref[...])
    v_mix = jnp.concatenate(v_rows, axis=0) if n_chunks > 1 else v_rows[0]
    y_b = _dot((u * v_mix).astype(BF16), wpgmlp_ref[...])
    merged = merged + jax.nn.sigmoid(_dot(m_s[...], w_ref[:, off_gb:off_gb + d_model])) * y_b

    mix = _dot(merged.astype(BF16), wout_ref[...])
    o_ref[0] = _layer_norm(DEEPNORM_ALPHA * x + g1 * mix) * ln1g_ref[...] + ln1b_ref[...]


def _const_spec(shape):
    nd = len(shape)
    return pl.BlockSpec(shape, lambda b, j: (0,) * nd, pipeline_mode=pl.Buffered(1))


def _mixer(x, mod1, w_in, conv_w, conv_b, dt_bias, a_log, d_skip, ssd_norm_w,
           gmlp_ln_g, gmlp_ln_b, gmlp_ws, gmlp_bs, w_proj_ssd, w_proj_gmlp, w_out, ln1_g, ln1_b):
    bsz, seq, d = x.shape
    d_inner = N_HEADS * HEAD_DIM
    gn = N_GROUPS * D_STATE
    d_xbc = d_inner + 2 * gn
    tile = min(MIX_TILE, seq)
    assert seq % tile == 0 and tile % CHUNK == 0 and d_inner == 2 * d

    s0, s1, s2, s3, s4 = (d_inner, d_inner + d_xbc, d_inner + d_xbc + N_HEADS,
                          d_inner + d_xbc + N_HEADS + 2 * d, d_inner + d_xbc + N_HEADS + 3 * d)
    w_dt = jnp.pad(w_in[:, s1:s2], ((0, 0), (0, LANES - N_HEADS)))
    w_all = jnp.concatenate([w_in[:, :s1], w_in[:, s2:], w_dt], axis=1).astype(BF16)
    off_z, off_xbc = 0, d_inner
    off_uv = s1
    off_ga = off_uv + 2 * d
    off_gb = off_ga + d
    off_dt = off_gb + d
    offs = (off_z, off_xbc, off_uv, off_ga, off_gb, off_dt)

    pad_h = lambda v: jnp.pad(v.reshape(1, N_HEADS), ((0, 0), (0, LANES - N_HEADS)))
    head_of_col64 = jnp.arange(d_inner) // HEAD_DIM
    piece_head = jnp.where(jnp.arange(LANES) < 3 * N_HEADS, jnp.arange(LANES) % N_HEADS, -1)
    e64 = (piece_head[:, None] == head_of_col64[None, :]).astype(BF16)
    dskip_x = jnp.broadcast_to(d_skip[:, None], (N_HEADS, HEAD_DIM)).reshape(1, d_inner)
    bs_x = jnp.broadcast_to(gmlp_bs.T[:, :, None], (CHUNK, GMLP_GROUPS, d // GMLP_GROUPS)).reshape(CHUNK, d)

    operands = [
        w_all, conv_w, conv_b.reshape(1, d_xbc), pad_h(dt_bias), pad_h(a_log), dskip_x,
        ssd_norm_w.reshape(1, d_inner), gmlp_ln_g.reshape(1, d), gmlp_ln_b.reshape(1, d),
        gmlp_ws, bs_x, e64,
        w_proj_ssd.astype(BF16), w_proj_gmlp.astype(BF16), w_out.astype(BF16),
        ln1_g.reshape(1, d), ln1_b.reshape(1, d),
    ]
    gw = HEADS_PER_GROUP * HEAD_DIM
    kern = functools.partial(_mixer_kernel, tile=tile, d_model=d, d_inner=d_inner, offs=offs)
    return pl.pallas_call(
        kern,
        grid=(bsz, seq // tile),
        in_specs=[pl.BlockSpec((1, tile, d), lambda b, j: (b, j, 0)),
                  pl.BlockSpec((1, SUBLANES, d), lambda b, j: (b, 0, 0))]
                 + [_const_spec(op.shape) for op in operands],
        out_specs=pl.BlockSpec((1, tile, d), lambda b, j: (b, j, 0)),
        out_shape=jax.ShapeDtypeStruct((bsz, seq, d), F32),
        scratch_shapes=[
            pltpu.VMEM((tile, d), BF16),
            pltpu.VMEM((N_GROUPS, tile, gw), F32),
            pltpu.VMEM((N_GROUPS, tile, D_STATE), BF16),
            pltpu.VMEM((N_GROUPS, tile, D_STATE), BF16),
            pltpu.VMEM((tile, LANES), F32),
            pltpu.VMEM((tile, d_inner), F32),
            pltpu.VMEM((N_GROUPS, D_STATE, gw), F32),
            pltpu.VMEM((SUBLANES, d_xbc), F32),
        ],
        compiler_params=pltpu.CompilerParams(
            dimension_semantics=("arbitrary", "arbitrary"),
            vmem_limit_bytes=MIXER_VMEM_LIMIT),
        name="mixer",
    )(x, mod1, *operands)


def _route_kernel(h_ref, mod_ref, wr_ref, bias_ref, topi_ref, topw_ref, rank_ref, cnt_ref, cnt_s,
                  *, tile):
    @pl.when(pl.program_id(0) == 0)
    def _():
        cnt_s[...] = jnp.zeros_like(cnt_s)

    mod = mod_ref[0]
    m2 = _layer_norm(h_ref[...]) * (1.0 + mod[1:2]) + mod[0:1]
    scores = jax.nn.sigmoid(_dot_nt(wr_ref[...], m2.astype(BF16)))
    choice = scores + bias_ref[...]

    neg = -jnp.inf
    c3 = choice.reshape(N_EXPERT_GROUPS, EXPERTS_PER_GROUP, tile)
    i3 = lax.broadcasted_iota(jnp.int32, c3.shape, 1).astype(F32)
    m1 = jnp.max(c3, axis=1, keepdims=True)
    i1 = jnp.min(jnp.where(c3 == m1, i3, float(EXPERTS_PER_GROUP)), axis=1, keepdims=True)
    second = jnp.max(jnp.where(i3 == i1, neg, c3), axis=1)
    gs = m1[:, 0, :] + second

    gi = lax.broadcasted_iota(jnp.int32, gs.shape, 0).astype(F32)
    gsel = jnp.zeros(gs.shape, F32)
    for _ in range(TOPK_GROUPS):
        mx = jnp.max(gs, axis=0, keepdims=True)
        ix = jnp.min(jnp.where(gs == mx, gi, float(N_EXPERT_GROUPS)), axis=0, keepdims=True)
        hit = gi == ix
        gsel = jnp.where(hit, 1.0, gsel)
        gs = jnp.where(hit, neg, gs)
    emask = jnp.broadcast_to(gsel[:, None, :], c3.shape).reshape(N_EXPERTS, tile) > 0.0
    masked = jnp.where(emask, choice, neg)

    ei = lax.broadcasted_iota(jnp.int32, masked.shape, 0).astype(F32)
    hits, idx_rows, w_rows = [], [], []
    for _ in range(TOP_K):
        mx = jnp.max(masked, axis=0, keepdims=True)
        ix = jnp.min(jnp.where(masked == mx, ei, float(N_EXPERTS)), axis=0, keepdims=True)
        hit = ei == ix
        hits.append(hit)
        idx_rows.append(ix)
        w_rows.append(jnp.sum(jnp.where(hit, scores, 0.0), axis=0, keepdims=True))
        masked = jnp.where(hit, neg, masked)
    w_all = jnp.concatenate(w_rows, axis=0)
    topw_ref[...] = w_all / (jnp.sum(w_all, axis=0, keepdims=True) + 1e-20) * ROUTED_SCALE
    topi_ref[...] = jnp.concatenate(idx_rows, axis=0).astype(jnp.int32)

    assign = jnp.zeros(masked.shape, F32)
    for hit in hits:
        assign = jnp.where(hit, 1.0, assign)
    assign_b = assign.astype(BF16)
    t0 = lax.broadcasted_iota(jnp.int32, (tile, tile), 0)
    t1 = lax.broadcasted_iota(jnp.int32, (tile, tile), 1)
    before = jnp.where(t0 < t1, 1.0, 0.0).astype(BF16)
    base = cnt_s[...]
    pos = _dot(assign_b, before) + jnp.concatenate([base] * (tile // LANES), axis=1)
    rank_ref[...] = jnp.concatenate(
        [jnp.sum(jnp.where(hit, pos, 0.0), axis=0, keepdims=True) for hit in hits],
        axis=0).astype(jnp.int32)
    cnt_s[...] = base + _dot(assign_b, jnp.ones((tile, LANES), BF16))
    cnt_ref[...] = cnt_s[...]


def _route(h1, mod2, w_router, router_bias, tokens_per_batch):
    t, d = h1.shape
    tile = min(ROUTE_TILE, tokens_per_batch)
    assert tokens_per_batch % tile == 0 and tile % LANES == 0
    per_b = tokens_per_batch // tile
    wr_t = w_router.T.astype(BF16)
    bias_x = jnp.broadcast_to(router_bias.reshape(N_EXPERTS, 1), (N_EXPERTS, tile)).astype(F32)
    kern = functools.partial(_route_kernel, tile=tile)
    row_spec = pl.BlockSpec((TOP_K, tile), lambda i: (0, i))
    return pl.pallas_call(
        kern,
        grid=(t // tile,),
        in_specs=[pl.BlockSpec((tile, d), lambda i: (i, 0)),
                  pl.BlockSpec((1, SUBLANES, d), lambda i: (i // per_b, 0, 0)),
                  pl.BlockSpec((N_EXPERTS, d), lambda i: (0, 0)),
                  pl.BlockSpec((N_EXPERTS, tile), lambda i: (0, 0))],
        out_specs=[row_spec, row_spec, row_spec,
                   pl.BlockSpec((N_EXPERTS, LANES), lambda i: (0, 0))],
        out_shape=[jax.ShapeDtypeStruct((TOP_K, t), jnp.int32),
                   jax.ShapeDtypeStruct((TOP_K, t), F32),
                   jax.ShapeDtypeStruct((TOP_K, t), jnp.int32),
                   jax.ShapeDtypeStruct((N_EXPERTS, LANES), F32)],
        scratch_shapes=[pltpu.VMEM((N_EXPERTS, LANES), F32)],
        compiler_params=pltpu.CompilerParams(dimension_semantics=("arbitrary",)),
        name="route",
    )(h1, mod2, wr_t, bias_x)


def _slots_kernel(topi_ref, rank_ref, start_ref, dest_ref, *, tile):
    ei = lax.broadcasted_iota(jnp.int32, (N_EXPERTS, tile), 0)
    start = jnp.concatenate([start_ref[...]] * (tile // LANES), axis=1)
    topi = topi_ref[...]
    rows = [jnp.sum(jnp.where(ei == topi[k:k + 1, :], start, 0.0), axis=0, keepdims=True)
            for k in range(TOP_K)]
    dest_ref[...] = jnp.concatenate(rows, axis=0).astype(jnp.int32) + rank_ref[...]


def _slots(topi_t, rank_t, pad_start):
    t = topi_t.shape[1]
    tile = min(SLOTS_TILE, t)
    start_x = jnp.broadcast_to(pad_start.astype(F32).reshape(N_EXPERTS, 1), (N_EXPERTS, LANES))
    row_spec = pl.BlockSpec((TOP_K, tile), lambda i: (0, i))
    return pl.pallas_call(
        functools.partial(_slots_kernel, tile=tile),
        grid=(t // tile,),
        in_specs=[row_spec, row_spec, pl.BlockSpec((N_EXPERTS, LANES), lambda i: (0, 0))],
        out_specs=row_spec,
        out_shape=jax.ShapeDtypeStruct((TOP_K, t), jnp.int32),
        name="slots",
    )(topi_t, rank_t, start_x)


def _to_slabs(slab_ref, x):
    rows, width = x.shape
    n = width // LANES
    for j in range(n):
        slab_ref[pl.ds(j, rows, stride=n), :] = x[:, j * LANES:(j + 1) * LANES]


def _from_slabs(slab_ref, rows, n):
    return jnp.concatenate([slab_ref[pl.ds(j, rows, stride=n), :] for j in range(n)], axis=1)


def _slab(ref, r, n):
    return ref.at[pl.ds(pl.multiple_of(r * n, n), n), :]


def _pack_bf16_pairs(x):
    n = x.shape[1] // 2
    bits = lax.bitcast_convert_type(x.astype(BF16).astype(F32), jnp.uint32)
    return bits[:, n:] | (bits[:, :n] >> 16)


def _unpack_pairs_f32(p):
    lo = lax.bitcast_convert_type(p << 16, F32)
    hi = lax.bitcast_convert_type(p & jnp.uint32(0xFFFF0000), F32)
    return jnp.concatenate([lo, hi], axis=1)


def _unpack_bf16_pairs(p):
    return _unpack_pairs_f32(p).astype(BF16)


def _scatter_kernel(zflag_ref, dest_ref, h_ref, mod_ref, out_ref, slab_s, zero_s, sem, zsem,
                    *, tile, n, n_steps, n_blocks):
    i = pl.program_id(0)
    slot = i % 2

    @pl.when(i == 0)
    def _():
        zero_s[...] = jnp.zeros_like(zero_s)
        brows = EXPERT_BLOCK * n

        def zero_copy(b):
            return pltpu.make_async_copy(
                zero_s, out_ref.at[pl.ds(pl.multiple_of(b * brows, brows), brows), :], zsem)

        def start(b, carry):
            @pl.when(zflag_ref[b] > 0)
            def _():
                zero_copy(b).start()
            return carry

        def wait(b, carry):
            @pl.when(zflag_ref[b] > 0)
            def _():
                zero_copy(b).wait()
            return carry

        lax.fori_loop(0, n_blocks, start, 0)
        lax.fori_loop(0, n_blocks, wait, 0)

    mod = mod_ref[0]
    _to_slabs(slab_s.at[slot],
              _pack_bf16_pairs(_layer_norm(h_ref[...]) * (1.0 + mod[1:2]) + mod[0:1]))

    def issue(t, carry):
        src = _slab(slab_s.at[slot], t, n)
        for k in range(TOP_K):
            d = dest_ref[t * TOP_K + k]
            pltpu.make_async_copy(src, _slab(out_ref, d, n), sem.at[slot]).start(priority=k % 2)
        return carry

    lax.fori_loop(0, tile, issue, 0)

    def drain(s):
        for _ in range(TOP_K):
            pltpu.make_async_copy(slab_s.at[s], out_ref.at[pl.ds(0, tile * n), :], sem.at[s]).wait()

    @pl.when(i >= 1)
    def _():
        drain(1 - slot)

    @pl.when(i == n_steps - 1)
    def _():
        drain(slot)


def _scatter(h1, mod2, dest_flat, zero_flag, tokens_per_batch):
    t, d = h1.shape
    n = d // 2 // LANES
    n_blocks = zero_flag.shape[0]
    tile = min(SCATTER_TILE, tokens_per_batch)
    per_b = tokens_per_batch // tile
    kern = functools.partial(_scatter_kernel, tile=tile, n=n, n_steps=t // tile, n_blocks=n_blocks)
    grid_spec = pltpu.PrefetchScalarGridSpec(
        num_scalar_prefetch=1,
        grid=(t // tile,),
        in_specs=[pl.BlockSpec((tile * TOP_K,), lambda i, zf: (i,), memory_space=pltpu.SMEM),
                  pl.BlockSpec((tile, d), lambda i, zf: (i, 0)),
                  pl.BlockSpec((1, SUBLANES, d), lambda i, zf: (i // per_b, 0, 0))],
        out_specs=pl.BlockSpec(memory_space=pl.ANY),
        scratch_shapes=[pltpu.VMEM((2, tile * n, LANES), jnp.uint32),
                        pltpu.VMEM((EXPERT_BLOCK * n, LANES), jnp.uint32),
                        pltpu.SemaphoreType.DMA((2,)), pltpu.SemaphoreType.DMA(())],
    )
    return pl.pallas_call(
        kern,
        grid_spec=grid_spec,
        out_shape=jax.ShapeDtypeStruct((n_blocks * EXPERT_BLOCK * n, LANES), jnp.uint32),
        compiler_params=pltpu.CompilerParams(dimension_semantics=("arbitrary",),
                                             has_side_effects=True),
        name="scatter",
    )(zero_flag, dest_flat, h1, mod2)


def _expert_kernel(blk0_ref, nblk_ref, x_ref, wg_ref, wu_ref, wd_ref, y_ref,
                   xbuf_s, ybuf_s, wg_s, wu_s, wd_s, in_sem, out_sem, *, nx, ny):
    e = pl.program_id(0)
    blk0 = blk0_ref[e]
    nblk = nblk_ref[e]
    used = blk0_ref[N_EXPERTS]
    rows = EXPERT_BLOCK
    ahead = EXPERT_IN_BUFS - 1

    def in_copy(gb):
        slot = gb % EXPERT_IN_BUFS
        return pltpu.make_async_copy(x_ref.at[pl.ds(pl.multiple_of(gb * rows * nx, rows * nx), rows * nx), :],
                                     xbuf_s.at[slot], in_sem.at[slot])

    def out_copy(gb):
        slot = gb % EXPERT_OUT_BUFS
        return pltpu.make_async_copy(ybuf_s.at[slot],
                                     y_ref.at[pl.ds(pl.multiple_of(gb * rows * ny, rows * ny), rows * ny), :],
                                     out_sem.at[slot])

    @pl.when(e == 0)
    def _():
        for b in range(ahead):
            @pl.when(b < used)
            def _():
                in_copy(b).start()

    @pl.when((e < N_EXPERTS) & (nblk > 0))
    def _():
        wg_s[...] = wg_ref[0].astype(BF16)
        wu_s[...] = wu_ref[0].astype(BF16)
        wd_s[...] = wd_ref[0].astype(BF16)

        def block(c, carry):
            gb = blk0 + c
            in_copy(gb).wait()

            @pl.when(gb + ahead < used)
            def _():
                in_copy(gb + ahead).start()

            @pl.when(gb >= EXPERT_OUT_BUFS)
            def _():
                out_copy(gb - EXPERT_OUT_BUFS).wait()

            xb = _unpack_bf16_pairs(_from_slabs(xbuf_s.at[gb % EXPERT_IN_BUFS], rows, nx))
            hid = (_silu(_dot(xb, wg_s[...])) * _dot(xb, wu_s[...])).astype(BF16)
            _to_slabs(ybuf_s.at[gb % EXPERT_OUT_BUFS], _pack_bf16_pairs(_dot(hid, wd_s[...])))
            out_copy(gb).start()
            return carry

        lax.fori_loop(0, nblk, block, 0)

    @pl.when(e == N_EXPERTS)
    def _():
        for b in range(EXPERT_OUT_BUFS, 0, -1):
            @pl.when(used >= b)
            def _():
                out_copy(used - b).wait()

        @pl.when(nblk > 0)
        def _():
            ybuf_s[0] = jnp.zeros(ybuf_s.shape[1:], jnp.uint32)

            def zero_block(c, carry):
                gb = blk0 + c
                cp = pltpu.make_async_copy(
                    ybuf_s.at[0],
                    y_ref.at[pl.ds(pl.multiple_of(gb * rows * ny, rows * ny), rows * ny), :],
                    out_sem.at[0])
                cp.start()
                cp.wait()
                return carry

            lax.fori_loop(0, nblk, zero_block, 0)


def _experts(xs_slabs, blk0, nblk, w_e_gate, w_e_up, w_e_down):
    d, ff = w_e_gate.shape[-2:]
    nx = d // 2 // LANES
    ny = d // 2 // LANES
    n_slots = xs_slabs.shape[0] // nx
    rows = EXPERT_BLOCK
    w_idx = lambda e, b0, nb: (jnp.minimum(e, N_EXPERTS - 1), 0, 0)
    grid_spec = pltpu.PrefetchScalarGridSpec(
        num_scalar_prefetch=2,
        grid=(N_EXPERTS + 1,),
        in_specs=[pl.BlockSpec(memory_space=pl.ANY),
                  pl.BlockSpec((1, d, ff), w_idx),
                  pl.BlockSpec((1, d, ff), w_idx),
                  pl.BlockSpec((1, ff, d), w_idx)],
        out_specs=pl.BlockSpec(memory_space=pl.ANY),
        scratch_shapes=[pltpu.VMEM((EXPERT_IN_BUFS, rows * nx, LANES), jnp.uint32),
                        pltpu.VMEM((EXPERT_OUT_BUFS, rows * ny, LANES), jnp.uint32),
                        pltpu.VMEM((d, ff), BF16), pltpu.VMEM((d, ff), BF16),
                        pltpu.VMEM((ff, d), BF16),
                        pltpu.SemaphoreType.DMA((EXPERT_IN_BUFS,)),
                        pltpu.SemaphoreType.DMA((EXPERT_OUT_BUFS,))],
    )
    return pl.pallas_call(
        functools.partial(_expert_kernel, nx=nx, ny=ny),
        grid_spec=grid_spec,
        out_shape=jax.ShapeDtypeStruct((n_slots * ny, LANES), jnp.uint32),
        compiler_params=pltpu.CompilerParams(dimension_semantics=("arbitrary",),
                                             has_side_effects=True),
        name="experts",
    )(blk0, nblk, xs_slabs, w_e_gate, w_e_up, w_e_down)


def _combine_kernel(dest_ref, dest_next_ref, h_ref, mod_ref, w_ref, y_ref, wsg_ref, wsu_ref, wsd_ref,
                    ln2g_ref, ln2b_ref, o_ref, ybuf_s, sem, *, tile, n, n_steps):
    i = pl.program_id(0)
    slot = i % 2

    def issue_tile(dref, s):
        def issue(t, carry):
            for k in range(TOP_K):
                d = dref[t * TOP_K + k]
                pltpu.make_async_copy(_slab(y_ref, d, n), _slab(ybuf_s.at[s, k], t, n),
                                      sem.at[s]).start(priority=k % 2)
            return carry
        lax.fori_loop(0, tile, issue, 0)

    @pl.when(i == 0)
    def _():
        issue_tile(dest_ref, 0)

    @pl.when(i + 1 < n_steps)
    def _():
        issue_tile(dest_next_ref, 1 - slot)

    h = h_ref[...]
    mod = mod_ref[0]
    m2 = (_layer_norm(h) * (1.0 + mod[1:2]) + mod[0:1]).astype(BF16)
    acc = _dot((_silu(_dot(m2, wsg_ref[...])) * _dot(m2, wsu_ref[...])).astype(BF16), wsd_ref[...])

    for k in range(TOP_K):
        pltpu.make_async_copy(y_ref.at[pl.ds(0, tile * n), :], ybuf_s.at[slot, k],
                              sem.at[slot]).wait()
    w = w_ref[...]
    for k in range(TOP_K):
        acc = acc + _unpack_pairs_f32(_from_slabs(ybuf_s.at[slot, k], tile, n)) * w[:, k:k + 1]
    o_ref[...] = _layer_norm(DEEPNORM_ALPHA * h + mod[2:3] * acc) * ln2g_ref[...] + ln2b_ref[...]


def _combine(h1, mod2, dest_flat, topw, y_slabs, w_sh_gate, w_sh_up, w_sh_down, ln2_g, ln2_b,
             tokens_per_batch):
    t, d = h1.shape
    n = d // 2 // LANES
    ff = w_sh_gate.shape[-1]
    tile = min(COMBINE_TILE, tokens_per_batch)
    per_b = tokens_per_batch // tile
    n_steps = t // tile
    kern = functools.partial(_combine_kernel, tile=tile, n=n, n_steps=n_steps)
    const = lambda shape: pl.BlockSpec(shape, lambda i: (0,) * len(shape))
    return pl.pallas_call(
        kern,
        grid=(n_steps,),
        in_specs=[pl.BlockSpec((tile * TOP_K,), lambda i: (i,), memory_space=pltpu.SMEM),
                  pl.BlockSpec((tile * TOP_K,), lambda i: (jnp.minimum(i + 1, n_steps - 1),),
                               memory_space=pltpu.SMEM),
                  pl.BlockSpec((tile, d), lambda i: (i, 0)),
                  pl.BlockSpec((1, SUBLANES, d), lambda i: (i // per_b, 0, 0)),
                  pl.BlockSpec((tile, TOP_K), lambda i: (i, 0)),
                  pl.BlockSpec(memory_space=pl.ANY),
                  const((d, ff)), const((d, ff)), const((ff, d)), const((1, d)), const((1, d))],
        out_specs=pl.BlockSpec((tile, d), lambda i: (i, 0)),
        out_shape=jax.ShapeDtypeStruct((t, d), F32),
        scratch_shapes=[pltpu.VMEM((2, TOP_K, tile * n, LANES), jnp.uint32),
                        pltpu.SemaphoreType.DMA((2,))],
        compiler_params=pltpu.CompilerParams(dimension_semantics=("arbitrary",)),
        name="combine",
    )(dest_flat, dest_flat, h1, mod2, topw, y_slabs, w_sh_gate.astype(BF16), w_sh_up.astype(BF16),
      w_sh_down.astype(BF16), ln2_g.reshape(1, d), ln2_b.reshape(1, d))


def _moe(h1, mod2, w_router, router_bias, w_e_gate, w_e_up, w_e_down,
         w_sh_gate, w_sh_up, w_sh_down, ln2_g, ln2_b, tokens_per_batch):
    t, d = h1.shape
    topi_t, topw_t, rank_t, cnt = _route(h1, mod2, w_router, router_bias, tokens_per_batch)

    blk = EXPERT_BLOCK
    nb = -(-(t * TOP_K) // blk) + N_EXPERTS
    counts = cnt[:, 0].astype(jnp.int32)
    padded = (counts + blk - 1) // blk * blk
    pad_end = jnp.cumsum(padded)
    pad_start = pad_end - padded
    dest_flat = _slots(topi_t, rank_t, pad_start).T.reshape(t * TOP_K)
    used = pad_end[-1] // blk
    blk0 = jnp.concatenate([pad_start // blk, used[None]]).astype(jnp.int32)
    nblk = jnp.concatenate([padded // blk, (nb - used)[None]]).astype(jnp.int32)

    block_id = jnp.arange(nb, dtype=jnp.int32)
    is_last = jnp.any((block_id[:, None] == (pad_end // blk - 1)[None, :]) & (padded > 0)[None, :], axis=1)
    zero_flag = (is_last | (block_id >= used)).astype(jnp.int32)

    xs_slabs = _scatter(h1, mod2, dest_flat, zero_flag, tokens_per_batch)
    y_slabs = _experts(xs_slabs, blk0, nblk, w_e_gate, w_e_up, w_e_down)
    return _combine(h1, mod2, dest_flat, topw_t.T, y_slabs, w_sh_gate, w_sh_up, w_sh_down,
                    ln2_g, ln2_b, tokens_per_batch)


def _pad_rows(parts, d):
    rows = jnp.stack(parts, axis=1)
    return jnp.pad(rows, ((0, 0), (0, SUBLANES - rows.shape[1]), (0, 0)))


def kernel(x, c, w_ada, b_ada, w_in, conv_w, conv_b, dt_bias, a_log, d_skip, ssd_norm_w, gmlp_ln_g, gmlp_ln_b, gmlp_ws, gmlp_bs, w_proj_ssd, w_proj_gmlp, w_out, ln1_g, ln1_b, w_router, router_bias, w_e_gate, w_e_up, w_e_down, w_sh_gate, w_sh_up, w_sh_down, ln2_g, ln2_b):
    bsz, seq, d = x.shape
    h = x
    for i in range(w_ada.shape[0]):
        ada = _ada(c, w_ada[i], b_ada[i])
        sh1, sc1, g1, sh2, sc2, g2 = jnp.split(ada, 6, axis=-1)
        h = _mixer(h, _pad_rows([sh1, sc1, g1], d), w_in[i], conv_w[i], conv_b[i], dt_bias[i],
                   a_log[i], d_skip[i], ssd_norm_w[i], gmlp_ln_g[i], gmlp_ln_b[i], gmlp_ws[i],
                   gmlp_bs[i], w_proj_ssd[i], w_proj_gmlp[i], w_out[i], ln1_g[i], ln1_b[i])
        h = _moe(h.reshape(bsz * seq, d), _pad_rows([sh2, sc2, g2], d), w_router[i],
                 router_bias[i], w_e_gate[i], w_e_up[i], w_e_down[i], w_sh_gate[i], w_sh_up[i],
                 w_sh_down[i], ln2_g[i], ln2_b[i], seq).reshape(bsz, seq, d)
    return h
```

```python
import functools

import jax
import jax.numpy as jnp
from jax import lax
from jax.experimental import pallas as pl
from jax.experimental.pallas import tpu as pltpu

F32 = jnp.float32
BF16 = jnp.bfloat16

HEAD_DIM = 64
N_GROUPS = 8
HEADS_PER_GROUP = 4
N_HEADS = N_GROUPS * HEADS_PER_GROUP
D_STATE = 128
CONV_K = 4
CHUNK = 128
GMLP_GROUPS = 8
N_EXPERTS = 256
TOP_K = 8
N_EXPERT_GROUPS = 8
EXPERTS_PER_GROUP = N_EXPERTS // N_EXPERT_GROUPS
TOPK_GROUPS = 4
ROUTED_SCALE = 2.5
DEPTH = 1
DEEPNORM_ALPHA = (2 * DEPTH) ** 0.25
LN_EPS = 1e-5
RMS_EPS = 1e-5

LANES = 128
SUBLANES = 8
VMEM_BYTES_V7X = 64 * 1024 * 1024
MIXER_VMEM_LIMIT = VMEM_BYTES_V7X - 8 * 1024 * 1024

MIX_TILE = 256
SCATTER_TILE = 256
EXPERT_BLOCK = 256
EXPERT_IN_BUFS = 4
EXPERT_OUT_BUFS = 4
COMBINE_TILE = 128
SLOTS_TILE = 512


def _dot(a, b):
    return jnp.dot(a, b, preferred_element_type=F32)


def _dot_nt(a, b):
    return lax.dot_general(a, b, (((1,), (1,)), ((), ())), preferred_element_type=F32)


def _dot_tn(a, b):
    return lax.dot_general(a, b, (((0,), (0,)), ((), ())), preferred_element_type=F32)


def _layer_norm(x):
    xc = x - jnp.mean(x, -1, keepdims=True)
    var = jnp.mean(xc * xc, -1, keepdims=True)
    return xc * lax.rsqrt(var + LN_EPS)


def _silu(x):
    return x * jax.nn.sigmoid(x)


def _softplus(x):
    return jnp.maximum(x, 0.0) + jnp.log1p(jnp.exp(-jnp.abs(x)))


def _ada_kernel(c_ref, w_ref, b_ref, o_ref):
    s = _silu(c_ref[...]).astype(BF16)
    o_ref[...] = _dot(s, w_ref[...].astype(BF16)) + b_ref[...]


def _ada(c, w_ada, b_ada):
    bsz, d = c.shape
    n = w_ada.shape[1]
    tn = 1024
    return pl.pallas_call(
        _ada_kernel,
        grid=(n // tn,),
        in_specs=[pl.BlockSpec((bsz, d), lambda j: (0, 0)),
                  pl.BlockSpec((d, tn), lambda j: (0, j)),
                  pl.BlockSpec((1, tn), lambda j: (0, j))],
        out_specs=pl.BlockSpec((bsz, tn), lambda j: (0, j)),
        out_shape=jax.ShapeDtypeStruct((bsz, n), F32),
        name="ada",
    )(c, w_ada, b_ada.reshape(1, n))


def _split3_bf16(a):
    hi = a.astype(BF16).astype(F32)
    r1 = a - hi
    mid = r1.astype(BF16).astype(F32)
    lo = r1 - mid
    pad = jnp.zeros((a.shape[0], LANES - 3 * N_HEADS), F32)
    return jnp.concatenate([hi, mid, lo, pad], axis=1).astype(BF16)


def _mixer_kernel(x_ref, mod_ref, mod2_ref, w_ref, convw_ref, convb_ref, dtb_ref, alog_ref, dskip_ref,
                  normw_ref, glng_ref, glnb_ref, ws_ref, bsx_ref, e64_ref,
                  wpssd_ref, wpgmlp_ref, wout_ref, ln1g_ref, ln1b_ref, wr_ref, rbias_ref,
                  o_ref, topi_ref, topw_ref, rank_ref, cnt_ref,
                  m_s, xs_s, bm_s, cm_s, dt_s, y_s, state_s, tail_s, cnt_s,
                  *, tile, d_model, d_inner, offs):
    off_z, off_xbc, off_uv, off_ga, off_gb, off_dt = offs
    n_chunks = tile // CHUNK
    gw = HEADS_PER_GROUP * HEAD_DIM

    @pl.when(pl.program_id(1) == 0)
    def _():
        state_s[...] = jnp.zeros_like(state_s)
        tail_s[...] = jnp.zeros_like(tail_s)

    @pl.when((pl.program_id(0) == 0) & (pl.program_id(1) == 0))
    def _():
        cnt_s[...] = jnp.zeros_like(cnt_s)

    x = x_ref[0]
    mod = mod_ref[0]
    sh1, sc1, g1 = mod[0:1], mod[1:2], mod[2:3]
    m_s[...] = (_layer_norm(x) * (1.0 + sc1) + sh1).astype(BF16)

    row8 = lax.broadcasted_iota(jnp.int32, (SUBLANES, gw), 0)
    n_xblk = d_inner // gw
    n_bblk = N_GROUPS * D_STATE // gw
    for blk in range(n_xblk + 2 * n_bblk):
        c0 = blk * gw
        pre = _dot(m_s[...], w_ref[:, off_xbc + c0:off_xbc + c0 + gw])
        prev_tail = tail_s[:, c0:c0 + gw]
        tail_s[:, c0:c0 + gw] = pre[tile - SUBLANES:tile]
        cw = convw_ref[:, c0:c0 + gw]
        acc = pre * cw[CONV_K - 1:CONV_K] + convb_ref[:, c0:c0 + gw]
        for k in range(1, CONV_K):
            r = pltpu.roll(pre, k, axis=0)
            top = jnp.where(row8 < k, pltpu.roll(prev_tail, k, axis=0), r[0:SUBLANES])
            r = jnp.concatenate([top, r[SUBLANES:]], axis=0)
            acc = acc + r * cw[CONV_K - 1 - k:CONV_K - k]
        act = _silu(acc)
        if blk < n_xblk:
            xs_s[blk] = act
        else:
            dst = bm_s if blk < n_xblk + n_bblk else cm_s
            g0 = ((blk - n_xblk) % n_bblk) * (gw // D_STATE)
            for i in range(gw // D_STATE):
                dst[g0 + i] = act[:, i * D_STATE:(i + 1) * D_STATE].astype(BF16)

    dt_s[...] = _softplus(_dot(m_s[...], w_ref[:, off_dt:off_dt + LANES]) + dtb_ref[...])
    a_neg = -jnp.exp(alog_ref[...])

    rowc = lax.broadcasted_iota(jnp.int32, (CHUNK, LANES), 0)
    causal = (lax.broadcasted_iota(jnp.int32, (CHUNK, CHUNK), 0)
              >= lax.broadcasted_iota(jnp.int32, (CHUNK, CHUNK), 1))
    lane_g = lax.broadcasted_iota(jnp.int32, (CHUNK, gw), 1)
    lane_c = lax.broadcasted_iota(jnp.int32, (CHUNK, CHUNK), 1)

    def chunk_body(c, carry):
        r0 = c * CHUNK
        rows = pl.ds(r0, CHUNK)
        dt_c = dt_s[rows, :]
        acs = dt_c * a_neg
        sh = 1
        while sh < CHUNK:
            acs = acs + jnp.where(rowc >= sh, pltpu.roll(acs, sh, axis=0), 0.0)
            sh *= 2
        acs_x128 = jnp.concatenate(
            [jnp.broadcast_to(acs[:, h:h + 1], (CHUNK, CHUNK)) for h in range(N_HEADS)], axis=1)
        acs_x64 = jnp.concatenate(
            [jnp.where(lane_c < HEAD_DIM, acs_x128[:, (2 * i) * CHUNK:(2 * i + 1) * CHUNK],
                       acs_x128[:, (2 * i + 1) * CHUNK:(2 * i + 2) * CHUNK])
             for i in range(N_HEADS // 2)], axis=1)
        dt_x64 = _dot(_split3_bf16(dt_c[:, :N_HEADS]), e64_ref[...])
        acs_t = acs.T
        last = acs_x64[CHUNK - 1:CHUNK, :]
        grow = jnp.exp(acs_x64)
        to_end = jnp.exp(last - acs_x64)
        chunk_decay = jnp.exp(last)

        for g in range(N_GROUPS):
            cols = slice(g * gw, (g + 1) * gw)
            xs_g = xs_s[g, rows, :]
            bm_g = bm_s[g, rows, :]
            cm_g = cm_s[g, rows, :]
            xdt = xs_g * dt_x64[:, cols]
            cb = _dot_nt(cm_g, bm_g)
            ws, xr = [], []
            for r in range(HEADS_PER_GROUP):
                h = g * HEADS_PER_GROUP + r
                seg = acs_x128[:, h * CHUNK:(h + 1) * CHUNK] - acs_t[h:h + 1, :]
                ws.append((cb * jnp.exp(jnp.where(causal, seg, -jnp.inf))).astype(BF16))
                in_head = (lane_g >= r * HEAD_DIM) & (lane_g < (r + 1) * HEAD_DIM)
                xr.append(jnp.where(in_head, xdt, 0.0).astype(BF16))
            y_g = None
            for r in range(0, HEADS_PER_GROUP, 2):
                part = _dot(jnp.concatenate(ws[r:r + 2], axis=1), jnp.concatenate(xr[r:r + 2], axis=0))
                y_g = part if y_g is None else y_g + part
            st = state_s[g]
            y_g = y_g + _dot(cm_g, st.astype(BF16)) * grow[:, cols]
            state_s[g] = chunk_decay[:, cols] * st + _dot_tn(bm_g, (xdt * to_end[:, cols]).astype(BF16))
            y_s[rows, cols] = y_g + dskip_ref[:, cols] * xs_g
        return carry

    for c in range(n_chunks):
        chunk_body(c, 0)

    z = _dot(m_s[...], w_ref[:, off_z:off_z + d_inner])
    y = y_s[...] * _silu(z)
    y = y * lax.rsqrt(jnp.mean(y * y, -1, keepdims=True) + RMS_EPS) * normw_ref[...]
    y_a = _dot(y.astype(BF16), wpssd_ref[...])
    merged = jax.nn.sigmoid(_dot(m_s[...], w_ref[:, off_ga:off_ga + d_model])) * y_a

    uv = jax.nn.gelu(_dot(m_s[...], w_ref[:, off_uv:off_uv + 2 * d_model]))
    u = uv[:, :d_model]
    v = _layer_norm(uv[:, d_model:]) * glng_ref[...] + glnb_ref[...]
    v = v.astype(BF16)
    gd = d_model // GMLP_GROUPS
    ws_m = [jnp.where(causal, ws_ref[g], 0.0).astype(BF16) for g in range(GMLP_GROUPS)]
    v_rows = []
    for c in range(n_chunks):
        v_c = v[c * CHUNK:(c + 1) * CHUNK]
        v_rows.append(jnp.concatenate(
            [_dot(ws_m[g], v_c[:, g * gd:(g + 1) * gd]) for g in range(GMLP_GROUPS)], axis=1)
            + bsx_ref[...])
    v_mix = jnp.concatenate(v_rows, axis=0) if n_chunks > 1 else v_rows[0]
    y_b = _dot((u * v_mix).astype(BF16), wpgmlp_ref[...])
    merged = merged + jax.nn.sigmoid(_dot(m_s[...], w_ref[:, off_gb:off_gb + d_model])) * y_b

    mix = _dot(merged.astype(BF16), wout_ref[...])
    h1 = _layer_norm(DEEPNORM_ALPHA * x + g1 * mix) * ln1g_ref[...] + ln1b_ref[...]
    o_ref[0] = h1

    _route_tile(h1, mod2_ref[0], wr_ref, rbias_ref, topi_ref, topw_ref, rank_ref, cnt_ref, cnt_s, tile)


def _const_spec(shape):
    nd = len(shape)
    return pl.BlockSpec(shape, lambda b, j: (0,) * nd, pipeline_mode=pl.Buffered(1))


def _mixer(x, mod1, mod2, w_in, conv_w, conv_b, dt_bias, a_log, d_skip, ssd_norm_w,
           gmlp_ln_g, gmlp_ln_b, gmlp_ws, gmlp_bs, w_proj_ssd, w_proj_gmlp, w_out, ln1_g, ln1_b,
           w_router, router_bias):
    bsz, seq, d = x.shape
    d_inner = N_HEADS * HEAD_DIM
    gn = N_GROUPS * D_STATE
    d_xbc = d_inner + 2 * gn
    tile = min(MIX_TILE, seq)
    assert seq % tile == 0 and tile % CHUNK == 0 and d_inner == 2 * d

    s0, s1, s2, s3, s4 = (d_inner, d_inner + d_xbc, d_inner + d_xbc + N_HEADS,
                          d_inner + d_xbc + N_HEADS + 2 * d, d_inner + d_xbc + N_HEADS + 3 * d)
    w_dt = jnp.pad(w_in[:, s1:s2], ((0, 0), (0, LANES - N_HEADS)))
    w_all = jnp.concatenate([w_in[:, :s1], w_in[:, s2:], w_dt], axis=1).astype(BF16)
    off_z, off_xbc = 0, d_inner
    off_uv = s1
    off_ga = off_uv + 2 * d
    off_gb = off_ga + d
    off_dt = off_gb + d
    offs = (off_z, off_xbc, off_uv, off_ga, off_gb, off_dt)

    pad_h = lambda v: jnp.pad(v.reshape(1, N_HEADS), ((0, 0), (0, LANES - N_HEADS)))
    head_of_col64 = jnp.arange(d_inner) // HEAD_DIM
    piece_head = jnp.where(jnp.arange(LANES) < 3 * N_HEADS, jnp.arange(LANES) % N_HEADS, -1)
    e64 = (piece_head[:, None] == head_of_col64[None, :]).astype(BF16)
    dskip_x = jnp.broadcast_to(d_skip[:, None], (N_HEADS, HEAD_DIM)).reshape(1, d_inner)
    bs_x = jnp.broadcast_to(gmlp_bs.T[:, :, None], (CHUNK, GMLP_GROUPS, d // GMLP_GROUPS)).reshape(CHUNK, d)

    operands = [
        w_all, conv_w, conv_b.reshape(1, d_xbc), pad_h(dt_bias), pad_h(a_log), dskip_x,
        ssd_norm_w.reshape(1, d_inner), gmlp_ln_g.reshape(1, d), gmlp_ln_b.reshape(1, d),
        gmlp_ws, bs_x, e64,
        w_proj_ssd.astype(BF16), w_proj_gmlp.astype(BF16), w_out.astype(BF16),
        ln1_g.reshape(1, d), ln1_b.reshape(1, d),
        w_router.T.astype(BF16),
        jnp.broadcast_to(router_bias.reshape(N_EXPERTS, 1), (N_EXPERTS, tile)).astype(F32),
    ]
    gw = HEADS_PER_GROUP * HEAD_DIM
    n_j = seq // tile
    t = bsz * seq
    kern = functools.partial(_mixer_kernel, tile=tile, d_model=d, d_inner=d_inner, offs=offs)
    row_spec = pl.BlockSpec((TOP_K, tile), lambda b, j: (0, b * n_j + j))
    return pl.pallas_call(
        kern,
        grid=(bsz, n_j),
        in_specs=[pl.BlockSpec((1, tile, d), lambda b, j: (b, j, 0)),
                  pl.BlockSpec((1, SUBLANES, d), lambda b, j: (b, 0, 0)),
                  pl.BlockSpec((1, SUBLANES, d), lambda b, j: (b, 0, 0))]
                 + [_const_spec(op.shape) for op in operands],
        out_specs=[pl.BlockSpec((1, tile, d), lambda b, j: (b, j, 0)),
                   row_spec, row_spec, row_spec,
                   pl.BlockSpec((N_EXPERTS, LANES), lambda b, j: (0, 0))],
        out_shape=[jax.ShapeDtypeStruct((bsz, seq, d), F32),
                   jax.ShapeDtypeStruct((TOP_K, t), jnp.int32),
                   jax.ShapeDtypeStruct((TOP_K, t), F32),
                   jax.ShapeDtypeStruct((TOP_K, t), jnp.int32),
                   jax.ShapeDtypeStruct((N_EXPERTS, LANES), F32)],
        scratch_shapes=[
            pltpu.VMEM((tile, d), BF16),
            pltpu.VMEM((N_GROUPS, tile, gw), F32),
            pltpu.VMEM((N_GROUPS, tile, D_STATE), BF16),
            pltpu.VMEM((N_GROUPS, tile, D_STATE), BF16),
            pltpu.VMEM((tile, LANES), F32),
            pltpu.VMEM((tile, d_inner), F32),
            pltpu.VMEM((N_GROUPS, D_STATE, gw), F32),
            pltpu.VMEM((SUBLANES, d_xbc), F32),
            pltpu.VMEM((N_EXPERTS, LANES), F32),
        ],
        compiler_params=pltpu.CompilerParams(
            dimension_semantics=("arbitrary", "arbitrary"),
            vmem_limit_bytes=MIXER_VMEM_LIMIT),
        name="mixer",
    )(x, mod1, mod2, *operands)


def _route_tile(h, mod, wr_ref, bias_ref, topi_ref, topw_ref, rank_ref, cnt_ref, cnt_s, tile):
    m2 = _layer_norm(h) * (1.0 + mod[1:2]) + mod[0:1]
    scores = jax.nn.sigmoid(_dot_nt(wr_ref[...], m2.astype(BF16)))
    choice = scores + bias_ref[...]

    neg = -jnp.inf
    c3 = choice.reshape(N_EXPERT_GROUPS, EXPERTS_PER_GROUP, tile)
    i3 = lax.broadcasted_iota(jnp.int32, c3.shape, 1).astype(F32)
    m1 = jnp.max(c3, axis=1, keepdims=True)
    i1 = jnp.min(jnp.where(c3 == m1, i3, float(EXPERTS_PER_GROUP)), axis=1, keepdims=True)
    second = jnp.max(jnp.where(i3 == i1, neg, c3), axis=1)
    gs = m1[:, 0, :] + second

    gi = lax.broadcasted_iota(jnp.int32, gs.shape, 0).astype(F32)
    gsel = jnp.zeros(gs.shape, F32)
    for _ in range(TOPK_GROUPS):
        mx = jnp.max(gs, axis=0, keepdims=True)
        ix = jnp.min(jnp.where(gs == mx, gi, float(N_EXPERT_GROUPS)), axis=0, keepdims=True)
        hit = gi == ix
        gsel = jnp.where(hit, 1.0, gsel)
        gs = jnp.where(hit, neg, gs)
    emask = jnp.broadcast_to(gsel[:, None, :], c3.shape).reshape(N_EXPERTS, tile) > 0.0
    masked = jnp.where(emask, choice, neg)

    ei = lax.broadcasted_iota(jnp.int32, masked.shape, 0).astype(F32)
    hits, idx_rows, w_rows = [], [], []
    for _ in range(TOP_K):
        mx = jnp.max(masked, axis=0, keepdims=True)
        ix = jnp.min(jnp.where(masked == mx, ei, float(N_EXPERTS)), axis=0, keepdims=True)
        hit = ei == ix
        hits.append(hit)
        idx_rows.append(ix)
        w_rows.append(jnp.sum(jnp.where(hit, scores, 0.0), axis=0, keepdims=True))
        masked = jnp.where(hit, neg, masked)
    w_all = jnp.concatenate(w_rows, axis=0)
    topw_ref[...] = w_all / (jnp.sum(w_all, axis=0, keepdims=True) + 1e-20) * ROUTED_SCALE
    topi_ref[...] = jnp.concatenate(idx_rows, axis=0).astype(jnp.int32)

    assign = jnp.zeros(masked.shape, F32)
    for hit in hits:
        assign = jnp.where(hit, 1.0, assign)
    assign_b = assign.astype(BF16)
    t0 = lax.broadcasted_iota(jnp.int32, (tile, tile), 0)
    t1 = lax.broadcasted_iota(jnp.int32, (tile, tile), 1)
    before = jnp.where(t0 < t1, 1.0, 0.0).astype(BF16)
    base = cnt_s[...]
    pos = _dot(assign_b, before) + jnp.concatenate([base] * (tile // LANES), axis=1)
    rank_ref[...] = jnp.concatenate(
        [jnp.sum(jnp.where(hit, pos, 0.0), axis=0, keepdims=True) for hit in hits],
        axis=0).astype(jnp.int32)
    cnt_s[...] = base + _dot(assign_b, jnp.ones((tile, LANES), BF16))
    cnt_ref[...] = cnt_s[...]


def _slots_kernel(topi_ref, rank_ref, start_ref, dest_ref, *, tile):
    ei = lax.broadcasted_iota(jnp.int32, (N_EXPERTS, tile), 0)
    start = jnp.concatenate([start_ref[...]] * (tile // LANES), axis=1)
    topi = topi_ref[...]
    rows = [jnp.sum(jnp.where(ei == topi[k:k + 1, :], start, 0.0), axis=0, keepdims=True)
            for k in range(TOP_K)]
    dest_ref[...] = jnp.concatenate(rows, axis=0).astype(jnp.int32) + rank_ref[...]


def _slots(topi_t, rank_t, pad_start):
    t = topi_t.shape[1]
    tile = min(SLOTS_TILE, t)
    start_x = jnp.broadcast_to(pad_start.astype(F32).reshape(N_EXPERTS, 1), (N_EXPERTS, LANES))
    row_spec = pl.BlockSpec((TOP_K, tile), lambda i: (0, i))
    return pl.pallas_call(
        functools.partial(_slots_kernel, tile=tile),
        grid=(t // tile,),
        in_specs=[row_spec, row_spec, pl.BlockSpec((N_EXPERTS, LANES), lambda i: (0, 0))],
        out_specs=row_spec,
        out_shape=jax.ShapeDtypeStruct((TOP_K, t), jnp.int32),
        name="slots",
    )(topi_t, rank_t, start_x)


def _to_slabs(slab_ref, x):
    rows, width = x.shape
    n = width // LANES
    for j in range(n):
        slab_ref[pl.ds(j, rows, stride=n), :] = x[:, j * LANES:(j + 1) * LANES]


def _from_slabs(slab_ref, rows, n):
    return jnp.concatenate([slab_ref[pl.ds(j, rows, stride=n), :] for j in range(n)], axis=1)


def _slab(ref, r, n):
    return ref.at[pl.ds(pl.multiple_of(r * n, n), n), :]


def _pack_bf16_pairs(x):
    n = x.shape[1] // 2
    bits = lax.bitcast_convert_type(x.astype(BF16).astype(F32), jnp.uint32)
    return bits[:, n:] | (bits[:, :n] >> 16)


def _unpack_pairs_f32(p):
    lo = lax.bitcast_convert_type(p << 16, F32)
    hi = lax.bitcast_convert_type(p & jnp.uint32(0xFFFF0000), F32)
    return jnp.concatenate([lo, hi], axis=1)


def _unpack_bf16_pairs(p):
    return _unpack_pairs_f32(p).astype(BF16)


def _scatter_kernel(zflag_ref, dest_ref, h_ref, mod_ref, out_ref, slab_s, zero_s, sem, zsem,
                    *, tile, n, n_steps, n_blocks):
    i = pl.program_id(0)
    slot = i % 2

    @pl.when(i == 0)
    def _():
        zero_s[...] = jnp.zeros_like(zero_s)
        brows = EXPERT_BLOCK * n

        def zero_copy(b):
            return pltpu.make_async_copy(
                zero_s, out_ref.at[pl.ds(pl.multiple_of(b * brows, brows), brows), :], zsem)

        def start(b, carry):
            @pl.when(zflag_ref[b] > 0)
            def _():
                zero_copy(b).start()
            return carry

        def wait(b, carry):
            @pl.when(zflag_ref[b] > 0)
            def _():
                zero_copy(b).wait()
            return carry

        lax.fori_loop(0, n_blocks, start, 0)
        lax.fori_loop(0, n_blocks, wait, 0)

    mod = mod_ref[0]
    _to_slabs(slab_s.at[slot],
              _pack_bf16_pairs(_layer_norm(h_ref[...]) * (1.0 + mod[1:2]) + mod[0:1]))

    def issue(t, carry):
        src = _slab(slab_s.at[slot], t, n)
        for k in range(TOP_K):
            d = dest_ref[t * TOP_K + k]
            pltpu.make_async_copy(src, _slab(out_ref, d, n), sem.at[slot]).start(priority=k % 2)
        return carry

    lax.fori_loop(0, tile, issue, 0)

    def drain(s):
        for _ in range(TOP_K):
            pltpu.make_async_copy(slab_s.at[s], out_ref.at[pl.ds(0, tile * n), :], sem.at[s]).wait()

    @pl.when(i >= 1)
    def _():
        drain(1 - slot)

    @pl.when(i == n_steps - 1)
    def _():
        drain(slot)


def _scatter(h1, mod2, dest_flat, zero_flag, tokens_per_batch):
    t, d = h1.shape
    n = d // 2 // LANES
    n_blocks = zero_flag.shape[0]
    tile = min(SCATTER_TILE, tokens_per_batch)
    per_b = tokens_per_batch // tile
    kern = functools.partial(_scatter_kernel, tile=tile, n=n, n_steps=t // tile, n_blocks=n_blocks)
    grid_spec = pltpu.PrefetchScalarGridSpec(
        num_scalar_prefetch=1,
        grid=(t // tile,),
        in_specs=[pl.BlockSpec((tile * TOP_K,), lambda i, zf: (i,), memory_space=pltpu.SMEM),
                  pl.BlockSpec((tile, d), lambda i, zf: (i, 0)),
                  pl.BlockSpec((1, SUBLANES, d), lambda i, zf: (i // per_b, 0, 0))],
        out_specs=pl.BlockSpec(memory_space=pl.ANY),
        scratch_shapes=[pltpu.VMEM((2, tile * n, LANES), jnp.uint32),
                        pltpu.VMEM((EXPERT_BLOCK * n, LANES), jnp.uint32),
                        pltpu.SemaphoreType.DMA((2,)), pltpu.SemaphoreType.DMA(())],
    )
    return pl.pallas_call(
        kern,
        grid_spec=grid_spec,
        out_shape=jax.ShapeDtypeStruct((n_blocks * EXPERT_BLOCK * n, LANES), jnp.uint32),
        compiler_params=pltpu.CompilerParams(dimension_semantics=("arbitrary",),
                                             has_side_effects=True),
        name="scatter",
    )(zero_flag, dest_flat, h1, mod2)


def _expert_kernel(blk0_ref, nblk_ref, x_ref, wg_ref, wu_ref, wd_ref, y_ref,
                   xbuf_s, ybuf_s, wg_s, wu_s, wd_s, in_sem, out_sem, *, nx, ny):
    e = pl.program_id(0)
    blk0 = blk0_ref[e]
    nblk = nblk_ref[e]
    used = blk0_ref[N_EXPERTS]
    rows = EXPERT_BLOCK
    ahead = EXPERT_IN_BUFS - 1

    def in_copy(gb):
        slot = gb % EXPERT_IN_BUFS
        return pltpu.make_async_copy(x_ref.at[pl.ds(pl.multiple_of(gb * rows * nx, rows * nx), rows * nx), :],
                                     xbuf_s.at[slot], in_sem.at[slot])

    def out_copy(gb):
        slot = gb % EXPERT_OUT_BUFS
        return pltpu.make_async_copy(ybuf_s.at[slot],
                                     y_ref.at[pl.ds(pl.multiple_of(gb * rows * ny, rows * ny), rows * ny), :],
                                     out_sem.at[slot])

    @pl.when(e == 0)
    def _():
        for b in range(ahead):
            @pl.when(b < used)
            def _():
                in_copy(b).start()

    @pl.when((e < N_EXPERTS) & (nblk > 0))
    def _():
        wg_s[...] = wg_ref[0].astype(BF16)
        wu_s[...] = wu_ref[0].astype(BF16)
        wd_s[...] = wd_ref[0].astype(BF16)

        def block(c, carry):
            gb = blk0 + c
            in_copy(gb).wait()

            @pl.when(gb + ahead < used)
            def _():
                in_copy(gb + ahead).start()

            @pl.when(gb >= EXPERT_OUT_BUFS)
            def _():
                out_copy(gb - EXPERT_OUT_BUFS).wait()

            xb = _unpack_bf16_pairs(_from_slabs(xbuf_s.at[gb % EXPERT_IN_BUFS], rows, nx))
            hid = (_silu(_dot(xb, wg_s[...])) * _dot(xb, wu_s[...])).astype(BF16)
            _to_slabs(ybuf_s.at[gb % EXPERT_OUT_BUFS], _pack_bf16_pairs(_dot(hid, wd_s[...])))
            out_copy(gb).start()
            return carry

        lax.fori_loop(0, nblk, block, 0)

    @pl.when(e == N_EXPERTS)
    def _():
        for b in range(EXPERT_OUT_BUFS, 0, -1):
            @pl.when(used >= b)
            def _():
                out_copy(used - b).wait()

        @pl.when(nblk > 0)
        def _():
            ybuf_s[0] = jnp.zeros(ybuf_s.shape[1:], jnp.uint32)

            def zero_block(c, carry):
                gb = blk0 + c
                cp = pltpu.make_async_copy(
                    ybuf_s.at[0],
                    y_ref.at[pl.ds(pl.multiple_of(gb * rows * ny, rows * ny), rows * ny), :],
                    out_sem.at[0])
                cp.start()
                cp.wait()
                return carry

            lax.fori_loop(0, nblk, zero_block, 0)


def _experts(xs_slabs, blk0, nblk, w_e_gate, w_e_up, w_e_down):
    d, ff = w_e_gate.shape[-2:]
    nx = d // 2 // LANES
    ny = d // 2 // LANES
    n_slots = xs_slabs.shape[0] // nx
    rows = EXPERT_BLOCK
    w_idx = lambda e, b0, nb: (jnp.minimum(e, N_EXPERTS - 1), 0, 0)
    grid_spec = pltpu.PrefetchScalarGridSpec(
        num_scalar_prefetch=2,
        grid=(N_EXPERTS + 1,),
        in_specs=[pl.BlockSpec(memory_space=pl.ANY),
                  pl.BlockSpec((1, d, ff), w_idx),
                  pl.BlockSpec((1, d, ff), w_idx),
                  pl.BlockSpec((1, ff, d), w_idx)],
        out_specs=pl.BlockSpec(memory_space=pl.ANY),
        scratch_shapes=[pltpu.VMEM((EXPERT_IN_BUFS, rows * nx, LANES), jnp.uint32),
                        pltpu.VMEM((EXPERT_OUT_BUFS, rows * ny, LANES), jnp.uint32),
                        pltpu.VMEM((d, ff), BF16), pltpu.VMEM((d, ff), BF16),
                        pltpu.VMEM((ff, d), BF16),
                        pltpu.SemaphoreType.DMA((EXPERT_IN_BUFS,)),
                        pltpu.SemaphoreType.DMA((EXPERT_OUT_BUFS,))],
    )
    return pl.pallas_call(
        functools.partial(_expert_kernel, nx=nx, ny=ny),
        grid_spec=grid_spec,
        out_shape=jax.ShapeDtypeStruct((n_slots * ny, LANES), jnp.uint32),
        compiler_params=pltpu.CompilerParams(dimension_semantics=("arbitrary",),
                                             has_side_effects=True),
        name="experts",
    )(blk0, nblk, xs_slabs, w_e_gate, w_e_up, w_e_down)


def _combine_kernel(dest_ref, dest_next_ref, h_ref, mod_ref, w_ref, y_ref, wsg_ref, wsu_ref, wsd_ref,
                    ln2g_ref, ln2b_ref, o_ref, ybuf_s, sem, *, tile, n, n_steps):
    i = pl.program_id(0)
    slot = i % 2

    def issue_tile(dref, s):
        def issue(t, carry):
            for k in range(TOP_K):
                d = dref[t * TOP_K + k]
                pltpu.make_async_copy(_slab(y_ref, d, n), _slab(ybuf_s.at[s, k], t, n),
                                      sem.at[s]).start(priority=k % 2)
            return carry
        lax.fori_loop(0, tile, issue, 0)

    @pl.when(i == 0)
    def _():
        issue_tile(dest_ref, 0)

    @pl.when(i + 1 < n_steps)
    def _():
        issue_tile(dest_next_ref, 1 - slot)

    h = h_ref[...]
    mod = mod_ref[0]
    m2 = (_layer_norm(h) * (1.0 + mod[1:2]) + mod[0:1]).astype(BF16)
    acc = _dot((_silu(_dot(m2, wsg_ref[...])) * _dot(m2, wsu_ref[...])).astype(BF16), wsd_ref[...])

    for k in range(TOP_K):
        pltpu.make_async_copy(y_ref.at[pl.ds(0, tile * n), :], ybuf_s.at[slot, k],
                              sem.at[slot]).wait()
    w = w_ref[...]
    for k in range(TOP_K):
        acc = acc + _unpack_pairs_f32(_from_slabs(ybuf_s.at[slot, k], tile, n)) * w[:, k:k + 1]
    o_ref[...] = _layer_norm(DEEPNORM_ALPHA * h + mod[2:3] * acc) * ln2g_ref[...] + ln2b_ref[...]


def _combine(h1, mod2, dest_flat, topw, y_slabs, w_sh_gate, w_sh_up, w_sh_down, ln2_g, ln2_b,
             tokens_per_batch):
    t, d = h1.shape
    n = d // 2 // LANES
    ff = w_sh_gate.shape[-1]
    tile = min(COMBINE_TILE, tokens_per_batch)
    per_b = tokens_per_batch // tile
    n_steps = t // tile
    kern = functools.partial(_combine_kernel, tile=tile, n=n, n_steps=n_steps)
    const = lambda shape: pl.BlockSpec(shape, lambda i: (0,) * len(shape))
    return pl.pallas_call(
        kern,
        grid=(n_steps,),
        in_specs=[pl.BlockSpec((tile * TOP_K,), lambda i: (i,), memory_space=pltpu.SMEM),
                  pl.BlockSpec((tile * TOP_K,), lambda i: (jnp.minimum(i + 1, n_steps - 1),),
                               memory_space=pltpu.SMEM),
                  pl.BlockSpec((tile, d), lambda i: (i, 0)),
                  pl.BlockSpec((1, SUBLANES, d), lambda i: (i // per_b, 0, 0)),
                  pl.BlockSpec((tile, TOP_K), lambda i: (i, 0)),
                  pl.BlockSpec(memory_space=pl.ANY),
                  const((d, ff)), const((d, ff)), const((ff, d)), const((1, d)), const((1, d))],
        out_specs=pl.BlockSpec((tile, d), lambda i: (i, 0)),
        out_shape=jax.ShapeDtypeStruct((t, d), F32),
        scratch_shapes=[pltpu.VMEM((2, TOP_K, tile * n, LANES), jnp.uint32),
                        pltpu.SemaphoreType.DMA((2,))],
        compiler_params=pltpu.CompilerParams(dimension_semantics=("arbitrary",)),
        name="combine",
    )(dest_flat, dest_flat, h1, mod2, topw, y_slabs, w_sh_gate.astype(BF16), w_sh_up.astype(BF16),
      w_sh_down.astype(BF16), ln2_g.reshape(1, d), ln2_b.reshape(1, d))


def _moe(h1, routing, mod2, w_e_gate, w_e_up, w_e_down,
         w_sh_gate, w_sh_up, w_sh_down, ln2_g, ln2_b, tokens_per_batch):
    t, d = h1.shape
    topi_t, topw_t, rank_t, cnt = routing

    blk = EXPERT_BLOCK
    nb = -(-(t * TOP_K) // blk) + N_EXPERTS
    counts = cnt[:, 0].astype(jnp.int32)
    padded = (counts + blk - 1) // blk * blk
    pad_end = jnp.cumsum(padded)
    pad_start = pad_end - padded
    dest_flat = _slots(topi_t, rank_t, pad_start).T.reshape(t * TOP_K)
    used = pad_end[-1] // blk
    blk0 = jnp.concatenate([pad_start // blk, used[None]]).astype(jnp.int32)
    nblk = jnp.concatenate([padded // blk, (nb - used)[None]]).astype(jnp.int32)

    block_id = jnp.arange(nb, dtype=jnp.int32)
    is_last = jnp.any((block_id[:, None] == (pad_end // blk - 1)[None, :]) & (padded > 0)[None, :], axis=1)
    zero_flag = (is_last | (block_id >= used)).astype(jnp.int32)

    xs_slabs = _scatter(h1, mod2, dest_flat, zero_flag, tokens_per_batch)
    y_slabs = _experts(xs_slabs, blk0, nblk, w_e_gate, w_e_up, w_e_down)
    return _combine(h1, mod2, dest_flat, topw_t.T, y_slabs, w_sh_gate, w_sh_up, w_sh_down,
                    ln2_g, ln2_b, tokens_per_batch)


def _pad_rows(parts, d):
    rows = jnp.stack(parts, axis=1)
    return jnp.pad(rows, ((0, 0), (0, SUBLANES - rows.shape[1]), (0, 0)))


def kernel(x, c, w_ada, b_ada, w_in, conv_w, conv_b, dt_bias, a_log, d_skip, ssd_norm_w, gmlp_ln_g, gmlp_ln_b, gmlp_ws, gmlp_bs, w_proj_ssd, w_proj_gmlp, w_out, ln1_g, ln1_b, w_router, router_bias, w_e_gate, w_e_up, w_e_down, w_sh_gate, w_sh_up, w_sh_down, ln2_g, ln2_b):
    bsz, seq, d = x.shape
    depth = w_ada.shape[0]
    at = lambda w, i: w.reshape(w.shape[1:]) if depth == 1 else w[i]
    h = x
    for i in range(depth):
        ada = _ada(c, at(w_ada, i), at(b_ada, i))
        sh1, sc1, g1, sh2, sc2, g2 = jnp.split(ada, 6, axis=-1)
        mod2 = _pad_rows([sh2, sc2, g2], d)
        h, *routing = _mixer(
            h, _pad_rows([sh1, sc1, g1], d), mod2,
            *[at(w, i) for w in (w_in, conv_w, conv_b, dt_bias, a_log, d_skip, ssd_norm_w, gmlp_ln_g,
                                 gmlp_ln_b, gmlp_ws, gmlp_bs, w_proj_ssd, w_proj_gmlp, w_out, ln1_g,
                                 ln1_b, w_router, router_bias)])
        h = _moe(h.reshape(bsz * seq, d), routing, mod2,
                 *[at(w, i) for w in (w_e_gate, w_e_up, w_e_down, w_sh_gate, w_sh_up, w_sh_down,
                                      ln2_g, ln2_b)], seq).reshape(bsz, seq, d)
    return h
```

```python
import functools

import jax
import jax.numpy as jnp
from jax import lax
from jax.experimental import pallas as pl
from jax.experimental.pallas import tpu as pltpu

F32 = jnp.float32
BF16 = jnp.bfloat16

HEAD_DIM = 64
N_GROUPS = 8
HEADS_PER_GROUP = 4
N_HEADS = N_GROUPS * HEADS_PER_GROUP
D_STATE = 128
CONV_K = 4
CHUNK = 128
GMLP_GROUPS = 8
N_EXPERTS = 256
TOP_K = 8
N_EXPERT_GROUPS = 8
EXPERTS_PER_GROUP = N_EXPERTS // N_EXPERT_GROUPS
TOPK_GROUPS = 4
ROUTED_SCALE = 2.5
DEPTH = 1
DEEPNORM_ALPHA = (2 * DEPTH) ** 0.25
LN_EPS = 1e-5
RMS_EPS = 1e-5

LANES = 128
SUBLANES = 8
VMEM_BYTES_V7X = 64 * 1024 * 1024
MIXER_VMEM_LIMIT = VMEM_BYTES_V7X - 8 * 1024 * 1024

MIX_TILE = 256
SCATTER_TILE = 512
EXPERT_BLOCK = 256
EXPERT_IN_BUFS = 4
EXPERT_OUT_BUFS = 4
COMBINE_TILE = 256
SLOTS_TILE = 512


def _dot(a, b):
    return jnp.dot(a, b, preferred_element_type=F32)


def _dot_nt(a, b):
    return lax.dot_general(a, b, (((1,), (1,)), ((), ())), preferred_element_type=F32)


def _dot_tn(a, b):
    return lax.dot_general(a, b, (((0,), (0,)), ((), ())), preferred_element_type=F32)


def _layer_norm(x):
    xc = x - jnp.mean(x, -1, keepdims=True)
    var = jnp.mean(xc * xc, -1, keepdims=True)
    return xc * lax.rsqrt(var + LN_EPS)


def _silu(x):
    return x * jax.nn.sigmoid(x)


def _softplus(x):
    return jnp.maximum(x, 0.0) + jnp.log1p(jnp.exp(-jnp.abs(x)))


def _ada_kernel(c_ref, w_ref, b_ref, o_ref):
    s = _silu(c_ref[...]).astype(BF16)
    o_ref[...] = _dot(s, w_ref[...].astype(BF16)) + b_ref[...]


def _ada(c, w_ada, b_ada):
    bsz, d = c.shape
    n = w_ada.shape[1]
    tn = 1024
    return pl.pallas_call(
        _ada_kernel,
        grid=(n // tn,),
        in_specs=[pl.BlockSpec((bsz, d), lambda j: (0, 0)),
                  pl.BlockSpec((d, tn), lambda j: (0, j)),
                  pl.BlockSpec((1, tn), lambda j: (0, j))],
        out_specs=pl.BlockSpec((bsz, tn), lambda j: (0, j)),
        out_shape=jax.ShapeDtypeStruct((bsz, n), F32),
        name="ada",
    )(c, w_ada, b_ada.reshape(1, n))


def _split3_bf16(a):
    hi = a.astype(BF16).astype(F32)
    r1 = a - hi
    mid = r1.astype(BF16).astype(F32)
    lo = r1 - mid
    pad = jnp.zeros((a.shape[0], LANES - 3 * N_HEADS), F32)
    return jnp.concatenate([hi, mid, lo, pad], axis=1).astype(BF16)


def _mixer_kernel(x_ref, mod_ref, mod2_ref, w_ref, convw_ref, convb_ref, dtb_ref, alog_ref, dskip_ref,
                  normw_ref, glng_ref, glnb_ref, ws_ref, bsx_ref, e64_ref,
                  wpssd_ref, wpgmlp_ref, wout_ref, ln1g_ref, ln1b_ref, wr_ref, rbias_ref,
                  o_ref, topi_ref, topw_ref, rank_ref, cnt_ref,
                  m_s, xs_s, bm_s, cm_s, dt_s, y_s, state_s, tail_s, cnt_s,
                  *, tile, d_model, d_inner, offs):
    off_z, off_xbc, off_uv, off_ga, off_gb, off_dt = offs
    n_chunks = tile // CHUNK
    gw = HEADS_PER_GROUP * HEAD_DIM

    @pl.when(pl.program_id(1) == 0)
    def _():
        state_s[...] = jnp.zeros_like(state_s)
        tail_s[...] = jnp.zeros_like(tail_s)

    @pl.when((pl.program_id(0) == 0) & (pl.program_id(1) == 0))
    def _():
        cnt_s[...] = jnp.zeros_like(cnt_s)

    x = x_ref[0]
    mod = mod_ref[0]
    sh1, sc1, g1 = mod[0:1], mod[1:2], mod[2:3]
    m_s[...] = (_layer_norm(x) * (1.0 + sc1) + sh1).astype(BF16)

    row8 = lax.broadcasted_iota(jnp.int32, (SUBLANES, gw), 0)
    n_xblk = d_inner // gw
    n_bblk = N_GROUPS * D_STATE // gw
    for blk in range(n_xblk + 2 * n_bblk):
        c0 = blk * gw
        pre = _dot(m_s[...], w_ref[:, off_xbc + c0:off_xbc + c0 + gw])
        prev_tail = tail_s[:, c0:c0 + gw]
        tail_s[:, c0:c0 + gw] = pre[tile - SUBLANES:tile]
        cw = convw_ref[:, c0:c0 + gw]
        acc = pre * cw[CONV_K - 1:CONV_K] + convb_ref[:, c0:c0 + gw]
        for k in range(1, CONV_K):
            r = pltpu.roll(pre, k, axis=0)
            top = jnp.where(row8 < k, pltpu.roll(prev_tail, k, axis=0), r[0:SUBLANES])
            r = jnp.concatenate([top, r[SUBLANES:]], axis=0)
            acc = acc + r * cw[CONV_K - 1 - k:CONV_K - k]
        act = _silu(acc)
        if blk < n_xblk:
            xs_s[blk] = act
        else:
            dst = bm_s if blk < n_xblk + n_bblk else cm_s
            g0 = ((blk - n_xblk) % n_bblk) * (gw // D_STATE)
            for i in range(gw // D_STATE):
                dst[g0 + i] = act[:, i * D_STATE:(i + 1) * D_STATE].astype(BF16)

    dt_s[...] = _softplus(_dot(m_s[...], w_ref[:, off_dt:off_dt + LANES]) + dtb_ref[...])
    a_neg = -jnp.exp(alog_ref[...])

    rowc = lax.broadcasted_iota(jnp.int32, (CHUNK, LANES), 0)
    causal = (lax.broadcasted_iota(jnp.int32, (CHUNK, CHUNK), 0)
              >= lax.broadcasted_iota(jnp.int32, (CHUNK, CHUNK), 1))
    lane_g = lax.broadcasted_iota(jnp.int32, (CHUNK, gw), 1)
    lane_c = lax.broadcasted_iota(jnp.int32, (CHUNK, CHUNK), 1)

    def chunk_body(c, carry):
        r0 = c * CHUNK
        rows = pl.ds(r0, CHUNK)
        dt_c = dt_s[rows, :]
        acs = dt_c * a_neg
        sh = 1
        while sh < CHUNK:
            acs = acs + jnp.where(rowc >= sh, pltpu.roll(acs, sh, axis=0), 0.0)
            sh *= 2
        acs_x128 = jnp.concatenate(
            [jnp.broadcast_to(acs[:, h:h + 1], (CHUNK, CHUNK)) for h in range(N_HEADS)], axis=1)
        acs_x64 = jnp.concatenate(
            [jnp.where(lane_c < HEAD_DIM, acs_x128[:, (2 * i) * CHUNK:(2 * i + 1) * CHUNK],
                       acs_x128[:, (2 * i + 1) * CHUNK:(2 * i + 2) * CHUNK])
             for i in range(N_HEADS // 2)], axis=1)
        dt_x64 = _dot(_split3_bf16(dt_c[:, :N_HEADS]), e64_ref[...])
        acs_t = acs.T
        last = acs_x64[CHUNK - 1:CHUNK, :]
        grow = jnp.exp(acs_x64)
        to_end = jnp.exp(last - acs_x64)
        chunk_decay = jnp.exp(last)

        for g in range(N_GROUPS):
            cols = slice(g * gw, (g + 1) * gw)
            xs_g = xs_s[g, rows, :]
            bm_g = bm_s[g, rows, :]
            cm_g = cm_s[g, rows, :]
            xdt = xs_g * dt_x64[:, cols]
            cb = _dot_nt(cm_g, bm_g)
            ws, xr = [], []
            for r in range(HEADS_PER_GROUP):
                h = g * HEADS_PER_GROUP + r
                seg = acs_x128[:, h * CHUNK:(h + 1) * CHUNK] - acs_t[h:h + 1, :]
                ws.append((cb * jnp.exp(jnp.where(causal, seg, -jnp.inf))).astype(BF16))
                in_head = (lane_g >= r * HEAD_DIM) & (lane_g < (r + 1) * HEAD_DIM)
                xr.append(jnp.where(in_head, xdt, 0.0).astype(BF16))
            y_g = None
            for r in range(0, HEADS_PER_GROUP, 2):
                part = _dot(jnp.concatenate(ws[r:r + 2], axis=1), jnp.concatenate(xr[r:r + 2], axis=0))
                y_g = part if y_g is None else y_g + part
            st = state_s[g]
            y_g = y_g + _dot(cm_g, st.astype(BF16)) * grow[:, cols]
            state_s[g] = chunk_decay[:, cols] * st + _dot_tn(bm_g, (xdt * to_end[:, cols]).astype(BF16))
            y_s[rows, cols] = y_g + dskip_ref[:, cols] * xs_g
        return carry

    for c in range(n_chunks):
        chunk_body(c, 0)

    z = _dot(m_s[...], w_ref[:, off_z:off_z + d_inner])
    y = y_s[...] * _silu(z)
    y = y * lax.rsqrt(jnp.mean(y * y, -1, keepdims=True) + RMS_EPS) * normw_ref[...]
    y_a = _dot(y.astype(BF16), wpssd_ref[...])
    merged = jax.nn.sigmoid(_dot(m_s[...], w_ref[:, off_ga:off_ga + d_model])) * y_a

    uv = jax.nn.gelu(_dot(m_s[...], w_ref[:, off_uv:off_uv + 2 * d_model]))
    u = uv[:, :d_model]
    v = _layer_norm(uv[:, d_model:]) * glng_ref[...] + glnb_ref[...]
    v = v.astype(BF16)
    gd = d_model // GMLP_GROUPS
    ws_m = [jnp.where(causal, ws_ref[g], 0.0).astype(BF16) for g in range(GMLP_GROUPS)]
    v_rows = []
    for c in range(n_chunks):
        v_c = v[c * CHUNK:(c + 1) * CHUNK]
        v_rows.append(jnp.concatenate(
            [_dot(ws_m[g], v_c[:, g * gd:(g + 1) * gd]) for g in range(GMLP_GROUPS)], axis=1)
            + bsx_ref[...])
    v_mix = jnp.concatenate(v_rows, axis=0) if n_chunks > 1 else v_rows[0]
    y_b = _dot((u * v_mix).astype(BF16), wpgmlp_ref[...])
    merged = merged + jax.nn.sigmoid(_dot(m_s[...], w_ref[:, off_gb:off_gb + d_model])) * y_b

    mix = _dot(merged.astype(BF16), wout_ref[...])
    h1 = _layer_norm(DEEPNORM_ALPHA * x + g1 * mix) * ln1g_ref[...] + ln1b_ref[...]
    o_ref[0] = h1

    _route_tile(h1, mod2_ref[0], wr_ref, rbias_ref, topi_ref, topw_ref, rank_ref, cnt_ref, cnt_s, tile)


def _const_spec(shape):
    nd = len(shape)
    return pl.BlockSpec(shape, lambda b, j: (0,) * nd, pipeline_mode=pl.Buffered(1))


def _mixer(x, mod1, mod2, w_in, conv_w, conv_b, dt_bias, a_log, d_skip, ssd_norm_w,
           gmlp_ln_g, gmlp_ln_b, gmlp_ws, gmlp_bs, w_proj_ssd, w_proj_gmlp, w_out, ln1_g, ln1_b,
           w_router, router_bias):
    bsz, seq, d = x.shape
    d_inner = N_HEADS * HEAD_DIM
    gn = N_GROUPS * D_STATE
    d_xbc = d_inner + 2 * gn
    tile = min(MIX_TILE, seq)
    assert seq % tile == 0 and tile % CHUNK == 0 and d_inner == 2 * d

    s0, s1, s2, s3, s4 = (d_inner, d_inner + d_xbc, d_inner + d_xbc + N_HEADS,
                          d_inner + d_xbc + N_HEADS + 2 * d, d_inner + d_xbc + N_HEADS + 3 * d)
    w_b = w_in.astype(BF16)
    w_dt = jnp.pad(w_b[:, s1:s2], ((0, 0), (0, LANES - N_HEADS)))
    w_all = jnp.concatenate([w_b[:, :s1], w_b[:, s2:], w_dt], axis=1)
    off_z, off_xbc = 0, d_inner
    off_uv = s1
    off_ga = off_uv + 2 * d
    off_gb = off_ga + d
    off_dt = off_gb + d
    offs = (off_z, off_xbc, off_uv, off_ga, off_gb, off_dt)

    pad_h = lambda v: jnp.pad(v.reshape(1, N_HEADS), ((0, 0), (0, LANES - N_HEADS)))
    head_of_col64 = jnp.arange(d_inner) // HEAD_DIM
    piece_head = jnp.where(jnp.arange(LANES) < 3 * N_HEADS, jnp.arange(LANES) % N_HEADS, -1)
    e64 = (piece_head[:, None] == head_of_col64[None, :]).astype(BF16)
    dskip_x = jnp.broadcast_to(d_skip[:, None], (N_HEADS, HEAD_DIM)).reshape(1, d_inner)
    bs_x = jnp.broadcast_to(gmlp_bs.T[:, :, None], (CHUNK, GMLP_GROUPS, d // GMLP_GROUPS)).reshape(CHUNK, d)

    operands = [
        w_all, conv_w, conv_b.reshape(1, d_xbc), pad_h(dt_bias), pad_h(a_log), dskip_x,
        ssd_norm_w.reshape(1, d_inner), gmlp_ln_g.reshape(1, d), gmlp_ln_b.reshape(1, d),
        gmlp_ws, bs_x, e64,
        w_proj_ssd.astype(BF16), w_proj_gmlp.astype(BF16), w_out.astype(BF16),
        ln1_g.reshape(1, d), ln1_b.reshape(1, d),
        w_router.T.astype(BF16),
        jnp.broadcast_to(router_bias.reshape(N_EXPERTS, 1), (N_EXPERTS, tile)).astype(F32),
    ]
    gw = HEADS_PER_GROUP * HEAD_DIM
    n_j = seq // tile
    t = bsz * seq
    kern = functools.partial(_mixer_kernel, tile=tile, d_model=d, d_inner=d_inner, offs=offs)
    row_spec = pl.BlockSpec((TOP_K, tile), lambda b, j: (0, b * n_j + j))
    return pl.pallas_call(
        kern,
        grid=(bsz, n_j),
        in_specs=[pl.BlockSpec((1, tile, d), lambda b, j: (b, j, 0)),
                  pl.BlockSpec((1, SUBLANES, d), lambda b, j: (b, 0, 0)),
                  pl.BlockSpec((1, SUBLANES, d), lambda b, j: (b, 0, 0))]
                 + [_const_spec(op.shape) for op in operands],
        out_specs=[pl.BlockSpec((1, tile, d), lambda b, j: (b, j, 0)),
                   row_spec, row_spec, row_spec,
                   pl.BlockSpec((N_EXPERTS, LANES), lambda b, j: (0, 0))],
        out_shape=[jax.ShapeDtypeStruct((bsz, seq, d), F32),
                   jax.ShapeDtypeStruct((TOP_K, t), jnp.int32),
                   jax.ShapeDtypeStruct((TOP_K, t), F32),
                   jax.ShapeDtypeStruct((TOP_K, t), jnp.int32),
                   jax.ShapeDtypeStruct((N_EXPERTS, LANES), F32)],
        scratch_shapes=[
            pltpu.VMEM((tile, d), BF16),
            pltpu.VMEM((N_GROUPS, tile, gw), F32),
            pltpu.VMEM((N_GROUPS, tile, D_STATE), BF16),
            pltpu.VMEM((N_GROUPS, tile, D_STATE), BF16),
            pltpu.VMEM((tile, LANES), F32),
            pltpu.VMEM((tile, d_inner), F32),
            pltpu.VMEM((N_GROUPS, D_STATE, gw), F32),
            pltpu.VMEM((SUBLANES, d_xbc), F32),
            pltpu.VMEM((N_EXPERTS, LANES), F32),
        ],
        compiler_params=pltpu.CompilerParams(
            dimension_semantics=("arbitrary", "arbitrary"),
            vmem_limit_bytes=MIXER_VMEM_LIMIT),
        name="mixer",
    )(x, mod1, mod2, *operands)


def _route_tile(h, mod, wr_ref, bias_ref, topi_ref, topw_ref, rank_ref, cnt_ref, cnt_s, tile):
    m2 = _layer_norm(h) * (1.0 + mod[1:2]) + mod[0:1]
    scores = jax.nn.sigmoid(_dot_nt(wr_ref[...], m2.astype(BF16)))
    choice = scores + bias_ref[...]

    neg = -jnp.inf
    c3 = choice.reshape(N_EXPERT_GROUPS, EXPERTS_PER_GROUP, tile)
    i3 = lax.broadcasted_iota(jnp.int32, c3.shape, 1).astype(F32)
    m1 = jnp.max(c3, axis=1, keepdims=True)
    i1 = jnp.min(jnp.where(c3 == m1, i3, float(EXPERTS_PER_GROUP)), axis=1, keepdims=True)
    second = jnp.max(jnp.where(i3 == i1, neg, c3), axis=1)
    gs = m1[:, 0, :] + second

    gi = lax.broadcasted_iota(jnp.int32, gs.shape, 0).astype(F32)
    gsel = jnp.zeros(gs.shape, F32)
    for _ in range(TOPK_GROUPS):
        mx = jnp.max(gs, axis=0, keepdims=True)
        ix = jnp.min(jnp.where(gs == mx, gi, float(N_EXPERT_GROUPS)), axis=0, keepdims=True)
        hit = gi == ix
        gsel = jnp.where(hit, 1.0, gsel)
        gs = jnp.where(hit, neg, gs)
    emask = jnp.broadcast_to(gsel[:, None, :], c3.shape).reshape(N_EXPERTS, tile) > 0.0
    masked = jnp.where(emask, choice, neg)

    ei = lax.broadcasted_iota(jnp.int32, masked.shape, 0).astype(F32)
    hits, idx_rows, w_rows = [], [], []
    for _ in range(TOP_K):
        mx = jnp.max(masked, axis=0, keepdims=True)
        ix = jnp.min(jnp.where(masked == mx, ei, float(N_EXPERTS)), axis=0, keepdims=True)
        hit = ei == ix
        hits.append(hit)
        idx_rows.append(ix)
        w_rows.append(jnp.sum(jnp.where(hit, scores, 0.0), axis=0, keepdims=True))
        masked = jnp.where(hit, neg, masked)
    w_all = jnp.concatenate(w_rows, axis=0)
    topw_ref[...] = w_all / (jnp.sum(w_all, axis=0, keepdims=True) + 1e-20) * ROUTED_SCALE
    topi_ref[...] = jnp.concatenate(idx_rows, axis=0).astype(jnp.int32)

    assign = jnp.zeros(masked.shape, F32)
    for hit in hits:
        assign = jnp.where(hit, 1.0, assign)
    assign_b = assign.astype(BF16)
    t0 = lax.broadcasted_iota(jnp.int32, (tile, tile), 0)
    t1 = lax.broadcasted_iota(jnp.int32, (tile, tile), 1)
    before = jnp.where(t0 < t1, 1.0, 0.0).astype(BF16)
    base = cnt_s[...]
    pos = _dot(assign_b, before) + jnp.concatenate([base] * (tile // LANES), axis=1)
    rank_ref[...] = jnp.concatenate(
        [jnp.sum(jnp.where(hit, pos, 0.0), axis=0, keepdims=True) for hit in hits],
        axis=0).astype(jnp.int32)
    cnt_s[...] = base + _dot(assign_b, jnp.ones((tile, LANES), BF16))
    cnt_ref[...] = cnt_s[...]


def _slots_kernel(topi_ref, rank_ref, start_ref, dest_ref, *, tile):
    ei = lax.broadcasted_iota(jnp.int32, (N_EXPERTS, tile), 0)
    start = jnp.concatenate([start_ref[...]] * (tile // LANES), axis=1)
    topi = topi_ref[...]
    rows = [jnp.sum(jnp.where(ei == topi[k:k + 1, :], start, 0.0), axis=0, keepdims=True)
            for k in range(TOP_K)]
    dest_ref[...] = jnp.concatenate(rows, axis=0).astype(jnp.int32) + rank_ref[...]


def _slots(topi_t, rank_t, pad_start):
    t = topi_t.shape[1]
    tile = min(SLOTS_TILE, t)
    start_x = jnp.broadcast_to(pad_start.astype(F32).reshape(N_EXPERTS, 1), (N_EXPERTS, LANES))
    row_spec = pl.BlockSpec((TOP_K, tile), lambda i: (0, i))
    return pl.pallas_call(
        functools.partial(_slots_kernel, tile=tile),
        grid=(t // tile,),
        in_specs=[row_spec, row_spec, pl.BlockSpec((N_EXPERTS, LANES), lambda i: (0, 0))],
        out_specs=row_spec,
        out_shape=jax.ShapeDtypeStruct((TOP_K, t), jnp.int32),
        name="slots",
    )(topi_t, rank_t, start_x)


def _to_slabs(slab_ref, x):
    rows, width = x.shape
    n = width // LANES
    for j in range(n):
        slab_ref[pl.ds(j, rows, stride=n), :] = x[:, j * LANES:(j + 1) * LANES]


def _from_slabs(slab_ref, rows, n):
    return jnp.concatenate([slab_ref[pl.ds(j, rows, stride=n), :] for j in range(n)], axis=1)


def _slab(ref, r, n):
    return ref.at[pl.ds(pl.multiple_of(r * n, n), n), :]


def _pack_bf16_pairs(x):
    n = x.shape[1] // 2
    bits = lax.bitcast_convert_type(x.astype(BF16).astype(F32), jnp.uint32)
    return bits[:, n:] | (bits[:, :n] >> 16)


def _unpack_pairs_f32(p):
    lo = lax.bitcast_convert_type(p << 16, F32)
    hi = lax.bitcast_convert_type(p & jnp.uint32(0xFFFF0000), F32)
    return jnp.concatenate([lo, hi], axis=1)


def _unpack_bf16_pairs(p):
    return _unpack_pairs_f32(p).astype(BF16)


def _scatter_kernel(zflag_ref, dest_ref, h_ref, mod_ref, out_ref, slab_s, zero_s, sem, zsem,
                    *, tile, n, n_steps, n_blocks):
    i = pl.program_id(0)
    slot = i % 2

    @pl.when(i == 0)
    def _():
        zero_s[...] = jnp.zeros_like(zero_s)
        brows = EXPERT_BLOCK * n

        def zero_copy(b):
            return pltpu.make_async_copy(
                zero_s, out_ref.at[pl.ds(pl.multiple_of(b * brows, brows), brows), :], zsem)

        def start(b, carry):
            @pl.when(zflag_ref[b] > 0)
            def _():
                zero_copy(b).start()
            return carry

        def wait(b, carry):
            @pl.when(zflag_ref[b] > 0)
            def _():
                zero_copy(b).wait()
            return carry

        lax.fori_loop(0, n_blocks, start, 0)
        lax.fori_loop(0, n_blocks, wait, 0)

    mod = mod_ref[0]
    _to_slabs(slab_s.at[slot],
              _pack_bf16_pairs(_layer_norm(h_ref[...]) * (1.0 + mod[1:2]) + mod[0:1]))

    def issue(t, carry):
        src = _slab(slab_s.at[slot], t, n)
        for k in range(TOP_K):
            d = dest_ref[t * TOP_K + k]
            pltpu.make_async_copy(src, _slab(out_ref, d, n), sem.at[slot]).start(priority=k % 2)
        return carry

    lax.fori_loop(0, tile, issue, 0)

    def drain(s):
        for _ in range(TOP_K):
            pltpu.make_async_copy(slab_s.at[s], out_ref.at[pl.ds(0, tile * n), :], sem.at[s]).wait()

    @pl.when(i >= 1)
    def _():
        drain(1 - slot)

    @pl.when(i == n_steps - 1)
    def _():
        drain(slot)


def _scatter(h1, mod2, dest_flat, zero_flag, tokens_per_batch):
    t, d = h1.shape
    n = d // 2 // LANES
    n_blocks = zero_flag.shape[0]
    tile = min(SCATTER_TILE, tokens_per_batch)
    per_b = tokens_per_batch // tile
    kern = functools.partial(_scatter_kernel, tile=tile, n=n, n_steps=t // tile, n_blocks=n_blocks)
    grid_spec = pltpu.PrefetchScalarGridSpec(
        num_scalar_prefetch=1,
        grid=(t // tile,),
        in_specs=[pl.BlockSpec((tile * TOP_K,), lambda i, zf: (i,), memory_space=pltpu.SMEM),
                  pl.BlockSpec((tile, d), lambda i, zf: (i, 0)),
                  pl.BlockSpec((1, SUBLANES, d), lambda i, zf: (i // per_b, 0, 0))],
        out_specs=pl.BlockSpec(memory_space=pl.ANY),
        scratch_shapes=[pltpu.VMEM((2, tile * n, LANES), jnp.uint32),
                        pltpu.VMEM((EXPERT_BLOCK * n, LANES), jnp.uint32),
                        pltpu.SemaphoreType.DMA((2,)), pltpu.SemaphoreType.DMA(())],
    )
    return pl.pallas_call(
        kern,
        grid_spec=grid_spec,
        out_shape=jax.ShapeDtypeStruct((n_blocks * EXPERT_BLOCK * n, LANES), jnp.uint32),
        compiler_params=pltpu.CompilerParams(dimension_semantics=("arbitrary",),
                                             has_side_effects=True),
        name="scatter",
    )(zero_flag, dest_flat, h1, mod2)


def _expert_kernel(blk0_ref, nblk_ref, x_ref, wg_ref, wu_ref, wd_ref, y_ref,
                   xbuf_s, ybuf_s, wg_s, wu_s, wd_s, in_sem, out_sem, *, nx, ny):
    e = pl.program_id(0)
    blk0 = blk0_ref[e]
    nblk = nblk_ref[e]
    used = blk0_ref[N_EXPERTS]
    rows = EXPERT_BLOCK
    ahead = EXPERT_IN_BUFS - 1

    def in_copy(gb):
        slot = gb % EXPERT_IN_BUFS
        return pltpu.make_async_copy(x_ref.at[pl.ds(pl.multiple_of(gb * rows * nx, rows * nx), rows * nx), :],
                                     xbuf_s.at[slot], in_sem.at[slot])

    def out_copy(gb):
        slot = gb % EXPERT_OUT_BUFS
        return pltpu.make_async_copy(ybuf_s.at[slot],
                                     y_ref.at[pl.ds(pl.multiple_of(gb * rows * ny, rows * ny), rows * ny), :],
                                     out_sem.at[slot])

    @pl.when(e == 0)
    def _():
        for b in range(ahead):
            @pl.when(b < used)
            def _():
                in_copy(b).start()

    @pl.when((e < N_EXPERTS) & (nblk > 0))
    def _():
        wg_s[...] = wg_ref[0].astype(BF16)
        wu_s[...] = wu_ref[0].astype(BF16)
        wd_s[...] = wd_ref[0].astype(BF16)

        def block(c, carry):
            gb = blk0 + c
            in_copy(gb).wait()

            @pl.when(gb + ahead < used)
            def _():
                in_copy(gb + ahead).start()

            @pl.when(gb >= EXPERT_OUT_BUFS)
            def _():
                out_copy(gb - EXPERT_OUT_BUFS).wait()

            xb = _unpack_bf16_pairs(_from_slabs(xbuf_s.at[gb % EXPERT_IN_BUFS], rows, nx))
            hid = (_silu(_dot(xb, wg_s[...])) * _dot(xb, wu_s[...])).astype(BF16)
            _to_slabs(ybuf_s.at[gb % EXPERT_OUT_BUFS], _pack_bf16_pairs(_dot(hid, wd_s[...])))
            out_copy(gb).start()
            return carry

        lax.fori_loop(0, nblk, block, 0)

    @pl.when(e == N_EXPERTS)
    def _():
        for b in range(EXPERT_OUT_BUFS, 0, -1):
            @pl.when(used >= b)
            def _():
                out_copy(used - b).wait()

        @pl.when(nblk > 0)
        def _():
            ybuf_s[0] = jnp.zeros(ybuf_s.shape[1:], jnp.uint32)

            def zero_block(c, carry):
                gb = blk0 + c
                cp = pltpu.make_async_copy(
                    ybuf_s.at[0],
                    y_ref.at[pl.ds(pl.multiple_of(gb * rows * ny, rows * ny), rows * ny), :],
                    out_sem.at[0])
                cp.start()
                cp.wait()
                return carry

            lax.fori_loop(0, nblk, zero_block, 0)


def _experts(xs_slabs, blk0, nblk, w_e_gate, w_e_up, w_e_down):
    d, ff = w_e_gate.shape[-2:]
    nx = d // 2 // LANES
    ny = d // 2 // LANES
    n_slots = xs_slabs.shape[0] // nx
    rows = EXPERT_BLOCK
    w_idx = lambda e, b0, nb: (jnp.minimum(e, N_EXPERTS - 1), 0, 0)
    grid_spec = pltpu.PrefetchScalarGridSpec(
        num_scalar_prefetch=2,
        grid=(N_EXPERTS + 1,),
        in_specs=[pl.BlockSpec(memory_space=pl.ANY),
                  pl.BlockSpec((1, d, ff), w_idx),
                  pl.BlockSpec((1, d, ff), w_idx),
                  pl.BlockSpec((1, ff, d), w_idx)],
        out_specs=pl.BlockSpec(memory_space=pl.ANY),
        scratch_shapes=[pltpu.VMEM((EXPERT_IN_BUFS, rows * nx, LANES), jnp.uint32),
                        pltpu.VMEM((EXPERT_OUT_BUFS, rows * ny, LANES), jnp.uint32),
                        pltpu.VMEM((d, ff), BF16), pltpu.VMEM((d, ff), BF16),
                        pltpu.VMEM((ff, d), BF16),
                        pltpu.SemaphoreType.DMA((EXPERT_IN_BUFS,)),
                        pltpu.SemaphoreType.DMA((EXPERT_OUT_BUFS,))],
    )
    return pl.pallas_call(
        functools.partial(_expert_kernel, nx=nx, ny=ny),
        grid_spec=grid_spec,
        out_shape=jax.ShapeDtypeStruct((n_slots * ny, LANES), jnp.uint32),
        compiler_params=pltpu.CompilerParams(dimension_semantics=("arbitrary",),
                                             has_side_effects=True),
        name="experts",
    )(blk0, nblk, xs_slabs, w_e_gate, w_e_up, w_e_down)


def _combine_kernel(dest_ref, dest_next_ref, h_ref, mod_ref, w_ref, y_ref, wsg_ref, wsu_ref, wsd_ref,
                    ln2g_ref, ln2b_ref, o_ref, ybuf_s, sem, *, tile, n, n_steps):
    i = pl.program_id(0)
    slot = i % 2

    def issue_tile(dref, s):
        def issue(t, carry):
            for k in range(TOP_K):
                d = dref[t * TOP_K + k]
                pltpu.make_async_copy(_slab(y_ref, d, n), _slab(ybuf_s.at[s, k], t, n),
                                      sem.at[s]).start(priority=k % 2)
            return carry
        lax.fori_loop(0, tile, issue, 0)

    @pl.when(i == 0)
    def _():
        issue_tile(dest_ref, 0)

    @pl.when(i + 1 < n_steps)
    def _():
        issue_tile(dest_next_ref, 1 - slot)

    h = h_ref[...]
    mod = mod_ref[0]
    m2 = (_layer_norm(h) * (1.0 + mod[1:2]) + mod[0:1]).astype(BF16)
    acc = _dot((_silu(_dot(m2, wsg_ref[...])) * _dot(m2, wsu_ref[...])).astype(BF16), wsd_ref[...])

    for k in range(TOP_K):
        pltpu.make_async_copy(y_ref.at[pl.ds(0, tile * n), :], ybuf_s.at[slot, k],
                              sem.at[slot]).wait()
    w = w_ref[...]
    for k in range(TOP_K):
        acc = acc + _unpack_pairs_f32(_from_slabs(ybuf_s.at[slot, k], tile, n)) * w[:, k:k + 1]
    o_ref[...] = _layer_norm(DEEPNORM_ALPHA * h + mod[2:3] * acc) * ln2g_ref[...] + ln2b_ref[...]


def _combine(h1, mod2, dest_flat, topw, y_slabs, w_sh_gate, w_sh_up, w_sh_down, ln2_g, ln2_b,
             tokens_per_batch):
    t, d = h1.shape
    n = d // 2 // LANES
    ff = w_sh_gate.shape[-1]
    tile = min(COMBINE_TILE, tokens_per_batch)
    per_b = tokens_per_batch // tile
    n_steps = t // tile
    kern = functools.partial(_combine_kernel, tile=tile, n=n, n_steps=n_steps)
    const = lambda shape: pl.BlockSpec(shape, lambda i: (0,) * len(shape))
    return pl.pallas_call(
        kern,
        grid=(n_steps,),
        in_specs=[pl.BlockSpec((tile * TOP_K,), lambda i: (i,), memory_space=pltpu.SMEM),
                  pl.BlockSpec((tile * TOP_K,), lambda i: (jnp.minimum(i + 1, n_steps - 1),),
                               memory_space=pltpu.SMEM),
                  pl.BlockSpec((tile, d), lambda i: (i, 0)),
                  pl.BlockSpec((1, SUBLANES, d), lambda i: (i // per_b, 0, 0)),
                  pl.BlockSpec((tile, TOP_K), lambda i: (i, 0)),
                  pl.BlockSpec(memory_space=pl.ANY),
                  const((d, ff)), const((d, ff)), const((ff, d)), const((1, d)), const((1, d))],
        out_specs=pl.BlockSpec((tile, d), lambda i: (i, 0)),
        out_shape=jax.ShapeDtypeStruct((t, d), F32),
        scratch_shapes=[pltpu.VMEM((2, TOP_K, tile * n, LANES), jnp.uint32),
                        pltpu.SemaphoreType.DMA((2,))],
        compiler_params=pltpu.CompilerParams(dimension_semantics=("arbitrary",)),
        name="combine",
    )(dest_flat, dest_flat, h1, mod2, topw, y_slabs, w_sh_gate.astype(BF16), w_sh_up.astype(BF16),
      w_sh_down.astype(BF16), ln2_g.reshape(1, d), ln2_b.reshape(1, d))


def _moe(h1, routing, mod2, w_e_gate, w_e_up, w_e_down,
         w_sh_gate, w_sh_up, w_sh_down, ln2_g, ln2_b, tokens_per_batch):
    t, d = h1.shape
    topi_t, topw_t, rank_t, cnt = routing

    blk = EXPERT_BLOCK
    nb = -(-(t * TOP_K) // blk) + N_EXPERTS
    counts = cnt[:, 0].astype(jnp.int32)
    padded = (counts + blk - 1) // blk * blk
    pad_end = jnp.cumsum(padded)
    pad_start = pad_end - padded
    dest_flat = _slots(topi_t, rank_t, pad_start).T.reshape(t * TOP_K)
    used = pad_end[-1] // blk
    blk0 = jnp.concatenate([pad_start // blk, used[None]]).astype(jnp.int32)
    nblk = jnp.concatenate([padded // blk, (nb - used)[None]]).astype(jnp.int32)

    block_id = jnp.arange(nb, dtype=jnp.int32)
    is_last = jnp.any((block_id[:, None] == (pad_end // blk - 1)[None, :]) & (padded > 0)[None, :], axis=1)
    zero_flag = (is_last | (block_id >= used)).astype(jnp.int32)

    xs_slabs = _scatter(h1, mod2, dest_flat, zero_flag, tokens_per_batch)
    y_slabs = _experts(xs_slabs, blk0, nblk, w_e_gate, w_e_up, w_e_down)
    return _combine(h1, mod2, dest_flat, topw_t.T, y_slabs, w_sh_gate, w_sh_up, w_sh_down,
                    ln2_g, ln2_b, tokens_per_batch)


def _pad_rows(parts, d):
    rows = jnp.stack(parts, axis=1)
    return jnp.pad(rows, ((0, 0), (0, SUBLANES - rows.shape[1]), (0, 0)))


def kernel(x, c, w_ada, b_ada, w_in, conv_w, conv_b, dt_bias, a_log, d_skip, ssd_norm_w, gmlp_ln_g, gmlp_ln_b, gmlp_ws, gmlp_bs, w_proj_ssd, w_proj_gmlp, w_out, ln1_g, ln1_b, w_router, router_bias, w_e_gate, w_e_up, w_e_down, w_sh_gate, w_sh_up, w_sh_down, ln2_g, ln2_b):
    bsz, seq, d = x.shape
    depth = w_ada.shape[0]
    at = lambda w, i: w.reshape(w.shape[1:]) if depth == 1 else w[i]
    h = x
    for i in range(depth):
        ada = _ada(c, at(w_ada, i), at(b_ada, i))
        sh1, sc1, g1, sh2, sc2, g2 = jnp.split(ada, 6, axis=-1)
        mod2 = _pad_rows([sh2, sc2, g2], d)
        h, *routing = _mixer(
            h, _pad_rows([sh1, sc1, g1], d), mod2,
            *[at(w, i) for w in (w_in, conv_w, conv_b, dt_bias, a_log, d_skip, ssd_norm_w, gmlp_ln_g,
                                 gmlp_ln_b, gmlp_ws, gmlp_bs, w_proj_ssd, w_proj_gmlp, w_out, ln1_g,
                                 ln1_b, w_router, router_bias)])
        h = _moe(h.reshape(bsz * seq, d), routing, mod2,
                 *[at(w, i) for w in (w_e_gate, w_e_up, w_e_down, w_sh_gate, w_sh_up, w_sh_down,
                                      ln2_g, ln2_b)], seq).reshape(bsz, seq, d)
    return h
```

```python
import functools

import jax
import jax.numpy as jnp
from jax import lax
from jax.experimental import pallas as pl
from jax.experimental.pallas import tpu as pltpu

F32 = jnp.float32
BF16 = jnp.bfloat16

HEAD_DIM = 64
N_GROUPS = 8
HEADS_PER_GROUP = 4
N_HEADS = N_GROUPS * HEADS_PER_GROUP
D_STATE = 128
CONV_K = 4
CHUNK = 128
GMLP_GROUPS = 8
N_EXPERTS = 256
TOP_K = 8
N_EXPERT_GROUPS = 8
EXPERTS_PER_GROUP = N_EXPERTS // N_EXPERT_GROUPS
TOPK_GROUPS = 4
ROUTED_SCALE = 2.5
DEPTH = 1
DEEPNORM_ALPHA = (2 * DEPTH) ** 0.25
LN_EPS = 1e-5
RMS_EPS = 1e-5

LANES = 128
SUBLANES = 8
VMEM_BYTES_V7X = 64 * 1024 * 1024
MIXER_VMEM_LIMIT = VMEM_BYTES_V7X - 8 * 1024 * 1024

MIX_TILE = 256
SCATTER_TILE = 512
EXPERT_BLOCK = 256
EXPERT_IN_BUFS = 4
EXPERT_OUT_BUFS = 4
COMBINE_TILE = 512
SLOTS_TILE = 512


def _dot(a, b):
    return jnp.dot(a, b, preferred_element_type=F32)


def _dot_nt(a, b):
    return lax.dot_general(a, b, (((1,), (1,)), ((), ())), preferred_element_type=F32)


def _dot_tn(a, b):
    return lax.dot_general(a, b, (((0,), (0,)), ((), ())), preferred_element_type=F32)


def _layer_norm(x):
    xc = x - jnp.mean(x, -1, keepdims=True)
    var = jnp.mean(xc * xc, -1, keepdims=True)
    return xc * lax.rsqrt(var + LN_EPS)


def _silu(x):
    return x * jax.nn.sigmoid(x)


def _softplus(x):
    return jnp.maximum(x, 0.0) + jnp.log1p(jnp.exp(-jnp.abs(x)))


def _ada_kernel(c_ref, w_ref, b_ref, o_ref):
    s = _silu(c_ref[...]).astype(BF16)
    o_ref[...] = _dot(s, w_ref[...].astype(BF16)) + b_ref[...]


def _ada(c, w_ada, b_ada):
    bsz, d = c.shape
    n = w_ada.shape[1]
    tn = 1024
    return pl.pallas_call(
        _ada_kernel,
        grid=(n // tn,),
        in_specs=[pl.BlockSpec((bsz, d), lambda j: (0, 0)),
                  pl.BlockSpec((d, tn), lambda j: (0, j)),
                  pl.BlockSpec((1, tn), lambda j: (0, j))],
        out_specs=pl.BlockSpec((bsz, tn), lambda j: (0, j)),
        out_shape=jax.ShapeDtypeStruct((bsz, n), F32),
        name="ada",
    )(c, w_ada, b_ada.reshape(1, n))


def _split3_bf16(a):
    hi = a.astype(BF16).astype(F32)
    r1 = a - hi
    mid = r1.astype(BF16).astype(F32)
    lo = r1 - mid
    pad = jnp.zeros((a.shape[0], LANES - 3 * N_HEADS), F32)
    return jnp.concatenate([hi, mid, lo, pad], axis=1).astype(BF16)


def _mixer_kernel(x_ref, mod_ref, mod2_ref, w_ref, convw_ref, convb_ref, dtb_ref, alog_ref, dskip_ref,
                  normw_ref, glng_ref, glnb_ref, ws_ref, bsx_ref, e64_ref,
                  wpssd_ref, wpgmlp_ref, wout_ref, ln1g_ref, ln1b_ref, wr_ref, rbias_ref,
                  o_ref, topi_ref, topw_ref, rank_ref, cnt_ref,
                  m_s, xs_s, bm_s, cm_s, dt_s, y_s, state_s, tail_s, cnt_s,
                  *, tile, d_model, d_inner, offs):
    off_z, off_xbc, off_uv, off_ga, off_gb, off_dt = offs
    n_chunks = tile // CHUNK
    gw = HEADS_PER_GROUP * HEAD_DIM

    @pl.when(pl.program_id(1) == 0)
    def _():
        state_s[...] = jnp.zeros_like(state_s)
        tail_s[...] = jnp.zeros_like(tail_s)

    @pl.when((pl.program_id(0) == 0) & (pl.program_id(1) == 0))
    def _():
        cnt_s[...] = jnp.zeros_like(cnt_s)

    x = x_ref[0]
    mod = mod_ref[0]
    sh1, sc1, g1 = mod[0:1], mod[1:2], mod[2:3]
    m_s[...] = (_layer_norm(x) * (1.0 + sc1) + sh1).astype(BF16)

    row8 = lax.broadcasted_iota(jnp.int32, (SUBLANES, gw), 0)
    n_xblk = d_inner // gw
    n_bblk = N_GROUPS * D_STATE // gw
    for blk in range(n_xblk + 2 * n_bblk):
        c0 = blk * gw
        pre = _dot(m_s[...], w_ref[:, off_xbc + c0:off_xbc + c0 + gw])
        prev_tail = tail_s[:, c0:c0 + gw]
        tail_s[:, c0:c0 + gw] = pre[tile - SUBLANES:tile]
        cw = convw_ref[:, c0:c0 + gw]
        acc = pre * cw[CONV_K - 1:CONV_K] + convb_ref[:, c0:c0 + gw]
        for k in range(1, CONV_K):
            r = pltpu.roll(pre, k, axis=0)
            top = jnp.where(row8 < k, pltpu.roll(prev_tail, k, axis=0), r[0:SUBLANES])
            r = jnp.concatenate([top, r[SUBLANES:]], axis=0)
            acc = acc + r * cw[CONV_K - 1 - k:CONV_K - k]
        act = _silu(acc)
        if blk < n_xblk:
            xs_s[blk] = act
        else:
            dst = bm_s if blk < n_xblk + n_bblk else cm_s
            g0 = ((blk - n_xblk) % n_bblk) * (gw // D_STATE)
            for i in range(gw // D_STATE):
                dst[g0 + i] = act[:, i * D_STATE:(i + 1) * D_STATE].astype(BF16)

    dt_s[...] = _softplus(_dot(m_s[...], w_ref[:, off_dt:off_dt + LANES]) + dtb_ref[...])
    a_neg = -jnp.exp(alog_ref[...])

    rowc = lax.broadcasted_iota(jnp.int32, (CHUNK, LANES), 0)
    causal = (lax.broadcasted_iota(jnp.int32, (CHUNK, CHUNK), 0)
              >= lax.broadcasted_iota(jnp.int32, (CHUNK, CHUNK), 1))
    lane_g = lax.broadcasted_iota(jnp.int32, (CHUNK, gw), 1)
    lane_c = lax.broadcasted_iota(jnp.int32, (CHUNK, CHUNK), 1)

    def chunk_body(c, carry):
        r0 = c * CHUNK
        rows = pl.ds(r0, CHUNK)
        dt_c = dt_s[rows, :]
        acs = dt_c * a_neg
        sh = 1
        while sh < CHUNK:
            acs = acs + jnp.where(rowc >= sh, pltpu.roll(acs, sh, axis=0), 0.0)
            sh *= 2
        acs_x128 = jnp.concatenate(
            [jnp.broadcast_to(acs[:, h:h + 1], (CHUNK, CHUNK)) for h in range(N_HEADS)], axis=1)
        acs_x64 = jnp.concatenate(
            [jnp.where(lane_c < HEAD_DIM, acs_x128[:, (2 * i) * CHUNK:(2 * i + 1) * CHUNK],
                       acs_x128[:, (2 * i + 1) * CHUNK:(2 * i + 2) * CHUNK])
             for i in range(N_HEADS // 2)], axis=1)
        dt_x64 = _dot(_split3_bf16(dt_c[:, :N_HEADS]), e64_ref[...])
        acs_t = acs.T
        last = acs_x64[CHUNK - 1:CHUNK, :]
        grow = jnp.exp(acs_x64)
        to_end = jnp.exp(last - acs_x64)
        chunk_decay = jnp.exp(last)

        for g in range(N_GROUPS):
            cols = slice(g * gw, (g + 1) * gw)
            xs_g = xs_s[g, rows, :]
            bm_g = bm_s[g, rows, :]
            cm_g = cm_s[g, rows, :]
            xdt = xs_g * dt_x64[:, cols]
            cb = _dot_nt(cm_g, bm_g)
            ws, xr = [], []
            for r in range(HEADS_PER_GROUP):
                h = g * HEADS_PER_GROUP + r
                seg = acs_x128[:, h * CHUNK:(h + 1) * CHUNK] - acs_t[h:h + 1, :]
                ws.append((cb * jnp.exp(jnp.where(causal, seg, -jnp.inf))).astype(BF16))
                in_head = (lane_g >= r * HEAD_DIM) & (lane_g < (r + 1) * HEAD_DIM)
                xr.append(jnp.where(in_head, xdt, 0.0).astype(BF16))
            y_g = None
            for r in range(0, HEADS_PER_GROUP, 2):
                part = _dot(jnp.concatenate(ws[r:r + 2], axis=1), jnp.concatenate(xr[r:r + 2], axis=0))
                y_g = part if y_g is None else y_g + part
            st = state_s[g]
            y_g = y_g + _dot(cm_g, st.astype(BF16)) * grow[:, cols]
            state_s[g] = chunk_decay[:, cols] * st + _dot_tn(bm_g, (xdt * to_end[:, cols]).astype(BF16))
            y_s[rows, cols] = y_g + dskip_ref[:, cols] * xs_g
        return carry

    for c in range(n_chunks):
        chunk_body(c, 0)

    z = _dot(m_s[...], w_ref[:, off_z:off_z + d_inner])
    y = y_s[...] * _silu(z)
    y = y * lax.rsqrt(jnp.mean(y * y, -1, keepdims=True) + RMS_EPS) * normw_ref[...]
    y_a = _dot(y.astype(BF16), wpssd_ref[...])
    merged = jax.nn.sigmoid(_dot(m_s[...], w_ref[:, off_ga:off_ga + d_model])) * y_a

    uv = jax.nn.gelu(_dot(m_s[...], w_ref[:, off_uv:off_uv + 2 * d_model]))
    u = uv[:, :d_model]
    v = _layer_norm(uv[:, d_model:]) * glng_ref[...] + glnb_ref[...]
    v = v.astype(BF16)
    gd = d_model // GMLP_GROUPS
    ws_m = [jnp.where(causal, ws_ref[g], 0.0).astype(BF16) for g in range(GMLP_GROUPS)]
    v_rows = []
    for c in range(n_chunks):
        v_c = v[c * CHUNK:(c + 1) * CHUNK]
        v_rows.append(jnp.concatenate(
            [_dot(ws_m[g], v_c[:, g * gd:(g + 1) * gd]) for g in range(GMLP_GROUPS)], axis=1)
            + bsx_ref[...])
    v_mix = jnp.concatenate(v_rows, axis=0) if n_chunks > 1 else v_rows[0]
    y_b = _dot((u * v_mix).astype(BF16), wpgmlp_ref[...])
    merged = merged + jax.nn.sigmoid(_dot(m_s[...], w_ref[:, off_gb:off_gb + d_model])) * y_b

    mix = _dot(merged.astype(BF16), wout_ref[...])
    h1 = _layer_norm(DEEPNORM_ALPHA * x + g1 * mix) * ln1g_ref[...] + ln1b_ref[...]
    o_ref[0] = h1

    _route_tile(h1, mod2_ref[0], wr_ref, rbias_ref, topi_ref, topw_ref, rank_ref, cnt_ref, cnt_s, tile)


def _const_spec(shape):
    nd = len(shape)
    return pl.BlockSpec(shape, lambda b, j: (0,) * nd, pipeline_mode=pl.Buffered(1))


def _mixer(x, mod1, mod2, w_in, conv_w, conv_b, dt_bias, a_log, d_skip, ssd_norm_w,
           gmlp_ln_g, gmlp_ln_b, gmlp_ws, gmlp_bs, w_proj_ssd, w_proj_gmlp, w_out, ln1_g, ln1_b,
           w_router, router_bias):
    bsz, seq, d = x.shape
    d_inner = N_HEADS * HEAD_DIM
    gn = N_GROUPS * D_STATE
    d_xbc = d_inner + 2 * gn
    tile = min(MIX_TILE, seq)
    assert seq % tile == 0 and tile % CHUNK == 0 and d_inner == 2 * d

    s0, s1, s2, s3, s4 = (d_inner, d_inner + d_xbc, d_inner + d_xbc + N_HEADS,
                          d_inner + d_xbc + N_HEADS + 2 * d, d_inner + d_xbc + N_HEADS + 3 * d)
    w_b = w_in.astype(BF16)
    w_dt = jnp.pad(w_b[:, s1:s2], ((0, 0), (0, LANES - N_HEADS)))
    w_all = jnp.concatenate([w_b[:, :s1], w_b[:, s2:], w_dt], axis=1)
    off_z, off_xbc = 0, d_inner
    off_uv = s1
    off_ga = off_uv + 2 * d
    off_gb = off_ga + d
    off_dt = off_gb + d
    offs = (off_z, off_xbc, off_uv, off_ga, off_gb, off_dt)

    pad_h = lambda v: jnp.pad(v.reshape(1, N_HEADS), ((0, 0), (0, LANES - N_HEADS)))
    head_of_col64 = jnp.arange(d_inner) // HEAD_DIM
    piece_head = jnp.where(jnp.arange(LANES) < 3 * N_HEADS, jnp.arange(LANES) % N_HEADS, -1)
    e64 = (piece_head[:, None] == head_of_col64[None, :]).astype(BF16)
    dskip_x = jnp.broadcast_to(d_skip[:, None], (N_HEADS, HEAD_DIM)).reshape(1, d_inner)
    bs_x = jnp.broadcast_to(gmlp_bs.T[:, :, None], (CHUNK, GMLP_GROUPS, d // GMLP_GROUPS)).reshape(CHUNK, d)

    operands = [
        w_all, conv_w, conv_b.reshape(1, d_xbc), pad_h(dt_bias), pad_h(a_log), dskip_x,
        ssd_norm_w.reshape(1, d_inner), gmlp_ln_g.reshape(1, d), gmlp_ln_b.reshape(1, d),
        gmlp_ws, bs_x, e64,
        w_proj_ssd.astype(BF16), w_proj_gmlp.astype(BF16), w_out.astype(BF16),
        ln1_g.reshape(1, d), ln1_b.reshape(1, d),
        w_router.T.astype(BF16),
        jnp.broadcast_to(router_bias.reshape(N_EXPERTS, 1), (N_EXPERTS, tile)).astype(F32),
    ]
    gw = HEADS_PER_GROUP * HEAD_DIM
    n_j = seq // tile
    t = bsz * seq
    kern = functools.partial(_mixer_kernel, tile=tile, d_model=d, d_inner=d_inner, offs=offs)
    row_spec = pl.BlockSpec((TOP_K, tile), lambda b, j: (0, b * n_j + j))
    return pl.pallas_call(
        kern,
        grid=(bsz, n_j),
        in_specs=[pl.BlockSpec((1, tile, d), lambda b, j: (b, j, 0)),
                  pl.BlockSpec((1, SUBLANES, d), lambda b, j: (b, 0, 0)),
                  pl.BlockSpec((1, SUBLANES, d), lambda b, j: (b, 0, 0))]
                 + [_const_spec(op.shape) for op in operands],
        out_specs=[pl.BlockSpec((1, tile, d), lambda b, j: (b, j, 0)),
                   row_spec, row_spec, row_spec,
                   pl.BlockSpec((N_EXPERTS, LANES), lambda b, j: (0, 0))],
        out_shape=[jax.ShapeDtypeStruct((bsz, seq, d), F32),
                   jax.ShapeDtypeStruct((TOP_K, t), jnp.int32),
                   jax.ShapeDtypeStruct((TOP_K, t), F32),
                   jax.ShapeDtypeStruct((TOP_K, t), jnp.int32),
                   jax.ShapeDtypeStruct((N_EXPERTS, LANES), F32)],
        scratch_shapes=[
            pltpu.VMEM((tile, d), BF16),
            pltpu.VMEM((N_GROUPS, tile, gw), F32),
            pltpu.VMEM((N_GROUPS, tile, D_STATE), BF16),
            pltpu.VMEM((N_GROUPS, tile, D_STATE), BF16),
            pltpu.VMEM((tile, LANES), F32),
            pltpu.VMEM((tile, d_inner), F32),
            pltpu.VMEM((N_GROUPS, D_STATE, gw), F32),
            pltpu.VMEM((SUBLANES, d_xbc), F32),
            pltpu.VMEM((N_EXPERTS, LANES), F32),
        ],
        compiler_params=pltpu.CompilerParams(
            dimension_semantics=("arbitrary", "arbitrary"),
            vmem_limit_bytes=MIXER_VMEM_LIMIT),
        name="mixer",
    )(x, mod1, mod2, *operands)


def _route_tile(h, mod, wr_ref, bias_ref, topi_ref, topw_ref, rank_ref, cnt_ref, cnt_s, tile):
    m2 = _layer_norm(h) * (1.0 + mod[1:2]) + mod[0:1]
    scores = jax.nn.sigmoid(_dot_nt(wr_ref[...], m2.astype(BF16)))
    choice = scores + bias_ref[...]

    neg = -jnp.inf
    c3 = choice.reshape(N_EXPERT_GROUPS, EXPERTS_PER_GROUP, tile)
    i3 = lax.broadcasted_iota(jnp.int32, c3.shape, 1).astype(F32)
    m1 = jnp.max(c3, axis=1, keepdims=True)
    i1 = jnp.min(jnp.where(c3 == m1, i3, float(EXPERTS_PER_GROUP)), axis=1, keepdims=True)
    second = jnp.max(jnp.where(i3 == i1, neg, c3), axis=1)
    gs = m1[:, 0, :] + second

    gi = lax.broadcasted_iota(jnp.int32, gs.shape, 0).astype(F32)
    gsel = jnp.zeros(gs.shape, F32)
    for _ in range(TOPK_GROUPS):
        mx = jnp.max(gs, axis=0, keepdims=True)
        ix = jnp.min(jnp.where(gs == mx, gi, float(N_EXPERT_GROUPS)), axis=0, keepdims=True)
        hit = gi == ix
        gsel = jnp.where(hit, 1.0, gsel)
        gs = jnp.where(hit, neg, gs)
    emask = jnp.broadcast_to(gsel[:, None, :], c3.shape).reshape(N_EXPERTS, tile) > 0.0
    masked = jnp.where(emask, choice, neg)

    ei = lax.broadcasted_iota(jnp.int32, masked.shape, 0).astype(F32)
    hits, idx_rows, w_rows = [], [], []
    for _ in range(TOP_K):
        mx = jnp.max(masked, axis=0, keepdims=True)
        ix = jnp.min(jnp.where(masked == mx, ei, float(N_EXPERTS)), axis=0, keepdims=True)
        hit = ei == ix
        hits.append(hit)
        idx_rows.append(ix)
        w_rows.append(jnp.sum(jnp.where(hit, scores, 0.0), axis=0, keepdims=True))
        masked = jnp.where(hit, neg, masked)
    w_all = jnp.concatenate(w_rows, axis=0)
    topw_ref[...] = w_all / (jnp.sum(w_all, axis=0, keepdims=True) + 1e-20) * ROUTED_SCALE
    topi_ref[...] = jnp.concatenate(idx_rows, axis=0).astype(jnp.int32)

    assign = jnp.zeros(masked.shape, F32)
    for hit in hits:
        assign = jnp.where(hit, 1.0, assign)
    assign_b = assign.astype(BF16)
    t0 = lax.broadcasted_iota(jnp.int32, (tile, tile), 0)
    t1 = lax.broadcasted_iota(jnp.int32, (tile, tile), 1)
    before = jnp.where(t0 < t1, 1.0, 0.0).astype(BF16)
    base = cnt_s[...]
    pos = _dot(assign_b, before) + jnp.concatenate([base] * (tile // LANES), axis=1)
    rank_ref[...] = jnp.concatenate(
        [jnp.sum(jnp.where(hit, pos, 0.0), axis=0, keepdims=True) for hit in hits],
        axis=0).astype(jnp.int32)
    cnt_s[...] = base + _dot(assign_b, jnp.ones((tile, LANES), BF16))
    cnt_ref[...] = cnt_s[...]


def _slots_kernel(topi_ref, rank_ref, start_ref, dest_ref, *, tile):
    ei = lax.broadcasted_iota(jnp.int32, (N_EXPERTS, tile), 0)
    start = jnp.concatenate([start_ref[...]] * (tile // LANES), axis=1)
    topi = topi_ref[...]
    rows = [jnp.sum(jnp.where(ei == topi[k:k + 1, :], start, 0.0), axis=0, keepdims=True)
            for k in range(TOP_K)]
    dest_ref[...] = jnp.concatenate(rows, axis=0).astype(jnp.int32) + rank_ref[...]


def _slots(topi_t, rank_t, pad_start):
    t = topi_t.shape[1]
    tile = min(SLOTS_TILE, t)
    start_x = jnp.broadcast_to(pad_start.astype(F32).reshape(N_EXPERTS, 1), (N_EXPERTS, LANES))
    row_spec = pl.BlockSpec((TOP_K, tile), lambda i: (0, i))
    return pl.pallas_call(
        functools.partial(_slots_kernel, tile=tile),
        grid=(t // tile,),
        in_specs=[row_spec, row_spec, pl.BlockSpec((N_EXPERTS, LANES), lambda i: (0, 0))],
        out_specs=row_spec,
        out_shape=jax.ShapeDtypeStruct((TOP_K, t), jnp.int32),
        name="slots",
    )(topi_t, rank_t, start_x)


def _to_slabs(slab_ref, x):
    rows, width = x.shape
    n = width // LANES
    for j in range(n):
        slab_ref[pl.ds(j, rows, stride=n), :] = x[:, j * LANES:(j + 1) * LANES]


def _from_slabs(slab_ref, rows, n):
    return jnp.concatenate([slab_ref[pl.ds(j, rows, stride=n), :] for j in range(n)], axis=1)


def _slab(ref, r, n):
    return ref.at[pl.ds(pl.multiple_of(r * n, n), n), :]


def _pack_bf16_pairs(x):
    n = x.shape[1] // 2
    bits = lax.bitcast_convert_type(x.astype(BF16).astype(F32), jnp.uint32)
    return bits[:, n:] | (bits[:, :n] >> 16)


def _unpack_pairs_f32(p):
    lo = lax.bitcast_convert_type(p << 16, F32)
    hi = lax.bitcast_convert_type(p & jnp.uint32(0xFFFF0000), F32)
    return jnp.concatenate([lo, hi], axis=1)


def _unpack_bf16_pairs(p):
    return _unpack_pairs_f32(p).astype(BF16)


def _scatter_kernel(zflag_ref, dest_ref, h_ref, mod_ref, out_ref, slab_s, zero_s, sem, zsem,
                    *, tile, n, n_steps, n_blocks):
    i = pl.program_id(0)
    slot = i % 2

    @pl.when(i == 0)
    def _():
        zero_s[...] = jnp.zeros_like(zero_s)
        brows = EXPERT_BLOCK * n

        def zero_copy(b):
            return pltpu.make_async_copy(
                zero_s, out_ref.at[pl.ds(pl.multiple_of(b * brows, brows), brows), :], zsem)

        def start(b, carry):
            @pl.when(zflag_ref[b] > 0)
            def _():
                zero_copy(b).start()
            return carry

        def wait(b, carry):
            @pl.when(zflag_ref[b] > 0)
            def _():
                zero_copy(b).wait()
            return carry

        lax.fori_loop(0, n_blocks, start, 0)
        lax.fori_loop(0, n_blocks, wait, 0)

    mod = mod_ref[0]
    _to_slabs(slab_s.at[slot],
              _pack_bf16_pairs(_layer_norm(h_ref[...]) * (1.0 + mod[1:2]) + mod[0:1]))

    def issue(t, carry):
        src = _slab(slab_s.at[slot], t, n)
        for k in range(TOP_K):
            d = dest_ref[t * TOP_K + k]
            pltpu.make_async_copy(src, _slab(out_ref, d, n), sem.at[slot]).start(priority=k % 2)
        return carry

    lax.fori_loop(0, tile, issue, 0)

    def drain(s):
        for _ in range(TOP_K):
            pltpu.make_async_copy(slab_s.at[s], out_ref.at[pl.ds(0, tile * n), :], sem.at[s]).wait()

    @pl.when(i >= 1)
    def _():
        drain(1 - slot)

    @pl.when(i == n_steps - 1)
    def _():
        drain(slot)


def _scatter(h1, mod2, dest_flat, zero_flag, tokens_per_batch):
    t, d = h1.shape
    n = d // 2 // LANES
    n_blocks = zero_flag.shape[0]
    tile = min(SCATTER_TILE, tokens_per_batch)
    per_b = tokens_per_batch // tile
    kern = functools.partial(_scatter_kernel, tile=tile, n=n, n_steps=t // tile, n_blocks=n_blocks)
    grid_spec = pltpu.PrefetchScalarGridSpec(
        num_scalar_prefetch=1,
        grid=(t // tile,),
        in_specs=[pl.BlockSpec((tile * TOP_K,), lambda i, zf: (i,), memory_space=pltpu.SMEM),
                  pl.BlockSpec((tile, d), lambda i, zf: (i, 0)),
                  pl.BlockSpec((1, SUBLANES, d), lambda i, zf: (i // per_b, 0, 0))],
        out_specs=pl.BlockSpec(memory_space=pl.ANY),
        scratch_shapes=[pltpu.VMEM((2, tile * n, LANES), jnp.uint32),
                        pltpu.VMEM((EXPERT_BLOCK * n, LANES), jnp.uint32),
                        pltpu.SemaphoreType.DMA((2,)), pltpu.SemaphoreType.DMA(())],
    )
    return pl.pallas_call(
        kern,
        grid_spec=grid_spec,
        out_shape=jax.ShapeDtypeStruct((n_blocks * EXPERT_BLOCK * n, LANES), jnp.uint32),
        compiler_params=pltpu.CompilerParams(dimension_semantics=("arbitrary",),
                                             has_side_effects=True),
        name="scatter",
    )(zero_flag, dest_flat, h1, mod2)


def _expert_kernel(blk0_ref, nblk_ref, x_ref, wg_ref, wu_ref, wd_ref, y_ref,
                   xbuf_s, ybuf_s, wg_s, wu_s, wd_s, in_sem, out_sem, *, nx, ny):
    e = pl.program_id(0)
    blk0 = blk0_ref[e]
    nblk = nblk_ref[e]
    used = blk0_ref[N_EXPERTS]
    rows = EXPERT_BLOCK
    ahead = EXPERT_IN_BUFS - 1

    def in_copy(gb):
        slot = gb % EXPERT_IN_BUFS
        return pltpu.make_async_copy(x_ref.at[pl.ds(pl.multiple_of(gb * rows * nx, rows * nx), rows * nx), :],
                                     xbuf_s.at[slot], in_sem.at[slot])

    def out_copy(gb):
        slot = gb % EXPERT_OUT_BUFS
        return pltpu.make_async_copy(ybuf_s.at[slot],
                                     y_ref.at[pl.ds(pl.multiple_of(gb * rows * ny, rows * ny), rows * ny), :],
                                     out_sem.at[slot])

    @pl.when(e == 0)
    def _():
        for b in range(ahead):
            @pl.when(b < used)
            def _():
                in_copy(b).start()

    @pl.when((e < N_EXPERTS) & (nblk > 0))
    def _():
        wg_s[...] = wg_ref[0].astype(BF16)
        wu_s[...] = wu_ref[0].astype(BF16)
        wd_s[...] = wd_ref[0].astype(BF16)

        def block(c, carry):
            gb = blk0 + c
            in_copy(gb).wait()

            @pl.when(gb + ahead < used)
            def _():
                in_copy(gb + ahead).start()

            @pl.when(gb >= EXPERT_OUT_BUFS)
            def _():
                out_copy(gb - EXPERT_OUT_BUFS).wait()

            xb = _unpack_bf16_pairs(_from_slabs(xbuf_s.at[gb % EXPERT_IN_BUFS], rows, nx))
            hid = (_silu(_dot(xb, wg_s[...])) * _dot(xb, wu_s[...])).astype(BF16)
            _to_slabs(ybuf_s.at[gb % EXPERT_OUT_BUFS], _pack_bf16_pairs(_dot(hid, wd_s[...])))
            out_copy(gb).start()
            return carry

        lax.fori_loop(0, nblk, block, 0)

    @pl.when(e == N_EXPERTS)
    def _():
        for b in range(EXPERT_OUT_BUFS, 0, -1):
            @pl.when(used >= b)
            def _():
                out_copy(used - b).wait()

        @pl.when(nblk > 0)
        def _():
            ybuf_s[0] = jnp.zeros(ybuf_s.shape[1:], jnp.uint32)

            def zero_block(c, carry):
                gb = blk0 + c
                cp = pltpu.make_async_copy(
                    ybuf_s.at[0],
                    y_ref.at[pl.ds(pl.multiple_of(gb * rows * ny, rows * ny), rows * ny), :],
                    out_sem.at[0])
                cp.start()
                cp.wait()
                return carry

            lax.fori_loop(0, nblk, zero_block, 0)


def _experts(xs_slabs, blk0, nblk, w_e_gate, w_e_up, w_e_down):
    d, ff = w_e_gate.shape[-2:]
    nx = d // 2 // LANES
    ny = d // 2 // LANES
    n_slots = xs_slabs.shape[0] // nx
    rows = EXPERT_BLOCK
    w_idx = lambda e, b0, nb: (jnp.minimum(e, N_EXPERTS - 1), 0, 0)
    grid_spec = pltpu.PrefetchScalarGridSpec(
        num_scalar_prefetch=2,
        grid=(N_EXPERTS + 1,),
        in_specs=[pl.BlockSpec(memory_space=pl.ANY),
                  pl.BlockSpec((1, d, ff), w_idx),
                  pl.BlockSpec((1, d, ff), w_idx),
                  pl.BlockSpec((1, ff, d), w_idx)],
        out_specs=pl.BlockSpec(memory_space=pl.ANY),
        scratch_shapes=[pltpu.VMEM((EXPERT_IN_BUFS, rows * nx, LANES), jnp.uint32),
                        pltpu.VMEM((EXPERT_OUT_BUFS, rows * ny, LANES), jnp.uint32),
                        pltpu.VMEM((d, ff), BF16), pltpu.VMEM((d, ff), BF16),
                        pltpu.VMEM((ff, d), BF16),
                        pltpu.SemaphoreType.DMA((EXPERT_IN_BUFS,)),
                        pltpu.SemaphoreType.DMA((EXPERT_OUT_BUFS,))],
    )
    return pl.pallas_call(
        functools.partial(_expert_kernel, nx=nx, ny=ny),
        grid_spec=grid_spec,
        out_shape=jax.ShapeDtypeStruct((n_slots * ny, LANES), jnp.uint32),
        compiler_params=pltpu.CompilerParams(dimension_semantics=("arbitrary",),
                                             has_side_effects=True),
        name="experts",
    )(blk0, nblk, xs_slabs, w_e_gate, w_e_up, w_e_down)


def _combine_kernel(dest_ref, dest_next_ref, h_ref, mod_ref, w_ref, y_ref, wsg_ref, wsu_ref, wsd_ref,
                    ln2g_ref, ln2b_ref, o_ref, ybuf_s, sem, *, tile, n, n_steps):
    i = pl.program_id(0)
    slot = i % 2

    def issue_tile(dref, s):
        def issue(t, carry):
            for k in range(TOP_K):
                d = dref[t * TOP_K + k]
                pltpu.make_async_copy(_slab(y_ref, d, n), _slab(ybuf_s.at[s, k], t, n),
                                      sem.at[s]).start(priority=k % 2)
            return carry
        lax.fori_loop(0, tile, issue, 0)

    @pl.when(i == 0)
    def _():
        issue_tile(dest_ref, 0)

    @pl.when(i + 1 < n_steps)
    def _():
        issue_tile(dest_next_ref, 1 - slot)

    h = h_ref[...]
    mod = mod_ref[0]
    m2 = (_layer_norm(h) * (1.0 + mod[1:2]) + mod[0:1]).astype(BF16)
    acc = _dot((_silu(_dot(m2, wsg_ref[...])) * _dot(m2, wsu_ref[...])).astype(BF16), wsd_ref[...])

    for k in range(TOP_K):
        pltpu.make_async_copy(y_ref.at[pl.ds(0, tile * n), :], ybuf_s.at[slot, k],
                              sem.at[slot]).wait()
    w = w_ref[...]
    for k in range(TOP_K):
        acc = acc + _unpack_pairs_f32(_from_slabs(ybuf_s.at[slot, k], tile, n)) * w[:, k:k + 1]
    o_ref[...] = _layer_norm(DEEPNORM_ALPHA * h + mod[2:3] * acc) * ln2g_ref[...] + ln2b_ref[...]


def _combine(h1, mod2, dest_flat, topw, y_slabs, w_sh_gate, w_sh_up, w_sh_down, ln2_g, ln2_b,
             tokens_per_batch):
    t, d = h1.shape
    n = d // 2 // LANES
    ff = w_sh_gate.shape[-1]
    tile = min(COMBINE_TILE, tokens_per_batch)
    per_b = tokens_per_batch // tile
    n_steps = t // tile
    kern = functools.partial(_combine_kernel, tile=tile, n=n, n_steps=n_steps)
    const = lambda shape: pl.BlockSpec(shape, lambda i: (0,) * len(shape))
    return pl.pallas_call(
        kern,
        grid=(n_steps,),
        in_specs=[pl.BlockSpec((tile * TOP_K,), lambda i: (i,), memory_space=pltpu.SMEM),
                  pl.BlockSpec((tile * TOP_K,), lambda i: (jnp.minimum(i + 1, n_steps - 1),),
                               memory_space=pltpu.SMEM),
                  pl.BlockSpec((tile, d), lambda i: (i, 0)),
                  pl.BlockSpec((1, SUBLANES, d), lambda i: (i // per_b, 0, 0)),
                  pl.BlockSpec((tile, TOP_K), lambda i: (i, 0)),
                  pl.BlockSpec(memory_space=pl.ANY),
                  const((d, ff)), const((d, ff)), const((ff, d)), const((1, d)), const((1, d))],
        out_specs=pl.BlockSpec((tile, d), lambda i: (i, 0)),
        out_shape=jax.ShapeDtypeStruct((t, d), F32),
        scratch_shapes=[pltpu.VMEM((2, TOP_K, tile * n, LANES), jnp.uint32),
                        pltpu.SemaphoreType.DMA((2,))],
        compiler_params=pltpu.CompilerParams(dimension_semantics=("arbitrary",)),
        name="combine",
    )(dest_flat, dest_flat, h1, mod2, topw, y_slabs, w_sh_gate.astype(BF16), w_sh_up.astype(BF16),
      w_sh_down.astype(BF16), ln2_g.reshape(1, d), ln2_b.reshape(1, d))


def _moe(h1, routing, mod2, w_e_gate, w_e_up, w_e_down,
         w_sh_gate, w_sh_up, w_sh_down, ln2_g, ln2_b, tokens_per_batch):
    t, d = h1.shape
    topi_t, topw_t, rank_t, cnt = routing

    blk = EXPERT_BLOCK
    nb = -(-(t * TOP_K) // blk) + N_EXPERTS
    counts = cnt[:, 0].astype(jnp.int32)
    padded = (counts + blk - 1) // blk * blk
    pad_end = jnp.cumsum(padded)
    pad_start = pad_end - padded
    dest_flat = _slots(topi_t, rank_t, pad_start).T.reshape(t * TOP_K)
    used = pad_end[-1] // blk
    blk0 = jnp.concatenate([pad_start // blk, used[None]]).astype(jnp.int32)
    nblk = jnp.concatenate([padded // blk, (nb - used)[None]]).astype(jnp.int32)

    block_id = jnp.arange(nb, dtype=jnp.int32)
    is_last = jnp.any((block_id[:, None] == (pad_end // blk - 1)[None, :]) & (padded > 0)[None, :], axis=1)
    zero_flag = (is_last | (block_id >= used)).astype(jnp.int32)

    xs_slabs = _scatter(h1, mod2, dest_flat, zero_flag, tokens_per_batch)
    y_slabs = _experts(xs_slabs, blk0, nblk, w_e_gate, w_e_up, w_e_down)
    return _combine(h1, mod2, dest_flat, topw_t.T, y_slabs, w_sh_gate, w_sh_up, w_sh_down,
                    ln2_g, ln2_b, tokens_per_batch)


def _pad_rows(parts, d):
    rows = jnp.stack(parts, axis=1)
    return jnp.pad(rows, ((0, 0), (0, SUBLANES - rows.shape[1]), (0, 0)))


def kernel(x, c, w_ada, b_ada, w_in, conv_w, conv_b, dt_bias, a_log, d_skip, ssd_norm_w, gmlp_ln_g, gmlp_ln_b, gmlp_ws, gmlp_bs, w_proj_ssd, w_proj_gmlp, w_out, ln1_g, ln1_b, w_router, router_bias, w_e_gate, w_e_up, w_e_down, w_sh_gate, w_sh_up, w_sh_down, ln2_g, ln2_b):
    bsz, seq, d = x.shape
    depth = w_ada.shape[0]
    at = lambda w, i: w.reshape(w.shape[1:]) if depth == 1 else w[i]
    h = x
    for i in range(depth):
        ada = _ada(c, at(w_ada, i), at(b_ada, i))
        sh1, sc1, g1, sh2, sc2, g2 = jnp.split(ada, 6, axis=-1)
        mod2 = _pad_rows([sh2, sc2, g2], d)
        h, *routing = _mixer(
            h, _pad_rows([sh1, sc1, g1], d), mod2,
            *[at(w, i) for w in (w_in, conv_w, conv_b, dt_bias, a_log, d_skip, ssd_norm_w, gmlp_ln_g,
                                 gmlp_ln_b, gmlp_ws, gmlp_bs, w_proj_ssd, w_proj_gmlp, w_out, ln1_g,
                                 ln1_b, w_router, router_bias)])
        h = _moe(h.reshape(bsz * seq, d), routing, mod2,
                 *[at(w, i) for w in (w_e_gate, w_e_up, w_e_down, w_sh_gate, w_sh_up, w_sh_down,
                                      ln2_g, ln2_b)], seq).reshape(bsz, seq, d)
    return h
```

```python
import functools

import jax
import jax.numpy as jnp
from jax import lax
from jax.experimental import pallas as pl
from jax.experimental.pallas import tpu as pltpu

F32 = jnp.float32
BF16 = jnp.bfloat16

HEAD_DIM = 64
N_GROUPS = 8
HEADS_PER_GROUP = 4
N_HEADS = N_GROUPS * HEADS_PER_GROUP
D_STATE = 128
CONV_K = 4
CHUNK = 128
GMLP_GROUPS = 8
N_EXPERTS = 256
TOP_K = 8
N_EXPERT_GROUPS = 8
EXPERTS_PER_GROUP = N_EXPERTS // N_EXPERT_GROUPS
TOPK_GROUPS = 4
ROUTED_SCALE = 2.5
DEPTH = 1
DEEPNORM_ALPHA = (2 * DEPTH) ** 0.25
LN_EPS = 1e-5
RMS_EPS = 1e-5

LANES = 128
SUBLANES = 8
VMEM_BYTES_V7X = 64 * 1024 * 1024
MIXER_VMEM_LIMIT = VMEM_BYTES_V7X - 8 * 1024 * 1024

MIX_TILE = 256
SCATTER_TILE = 512
EXPERT_BLOCK = 512
EXPERT_IN_BUFS = 4
EXPERT_OUT_BUFS = 4
COMBINE_TILE = 512
SLOTS_TILE = 512


def _dot(a, b):
    return jnp.dot(a, b, preferred_element_type=F32)


def _dot_nt(a, b):
    return lax.dot_general(a, b, (((1,), (1,)), ((), ())), preferred_element_type=F32)


def _dot_tn(a, b):
    return lax.dot_general(a, b, (((0,), (0,)), ((), ())), preferred_element_type=F32)


def _layer_norm(x):
    xc = x - jnp.mean(x, -1, keepdims=True)
    var = jnp.mean(xc * xc, -1, keepdims=True)
    return xc * lax.rsqrt(var + LN_EPS)


def _silu(x):
    return x * jax.nn.sigmoid(x)


def _softplus(x):
    return jnp.maximum(x, 0.0) + jnp.log1p(jnp.exp(-jnp.abs(x)))


def _ada_kernel(c_ref, w_ref, b_ref, o_ref):
    s = _silu(c_ref[...]).astype(BF16)
    o_ref[...] = _dot(s, w_ref[...].astype(BF16)) + b_ref[...]


def _ada(c, w_ada, b_ada):
    bsz, d = c.shape
    n = w_ada.shape[1]
    tn = 1024
    return pl.pallas_call(
        _ada_kernel,
        grid=(n // tn,),
        in_specs=[pl.BlockSpec((bsz, d), lambda j: (0, 0)),
                  pl.BlockSpec((d, tn), lambda j: (0, j)),
                  pl.BlockSpec((1, tn), lambda j: (0, j))],
        out_specs=pl.BlockSpec((bsz, tn), lambda j: (0, j)),
        out_shape=jax.ShapeDtypeStruct((bsz, n), F32),
        name="ada",
    )(c, w_ada, b_ada.reshape(1, n))


def _split3_bf16(a):
    hi = a.astype(BF16).astype(F32)
    r1 = a - hi
    mid = r1.astype(BF16).astype(F32)
    lo = r1 - mid
    pad = jnp.zeros((a.shape[0], LANES - 3 * N_HEADS), F32)
    return jnp.concatenate([hi, mid, lo, pad], axis=1).astype(BF16)


def _mixer_kernel(x_ref, mod_ref, mod2_ref, w_ref, convw_ref, convb_ref, dtb_ref, alog_ref, dskip_ref,
                  normw_ref, glng_ref, glnb_ref, ws_ref, bsx_ref, e64_ref,
                  wpssd_ref, wpgmlp_ref, wout_ref, ln1g_ref, ln1b_ref, wr_ref, rbias_ref,
                  o_ref, topi_ref, topw_ref, rank_ref, cnt_ref,
                  m_s, xs_s, bm_s, cm_s, dt_s, y_s, state_s, tail_s, cnt_s,
                  *, tile, d_model, d_inner, offs):
    off_z, off_xbc, off_uv, off_ga, off_gb, off_dt = offs
    n_chunks = tile // CHUNK
    gw = HEADS_PER_GROUP * HEAD_DIM

    @pl.when(pl.program_id(1) == 0)
    def _():
        state_s[...] = jnp.zeros_like(state_s)
        tail_s[...] = jnp.zeros_like(tail_s)

    @pl.when((pl.program_id(0) == 0) & (pl.program_id(1) == 0))
    def _():
        cnt_s[...] = jnp.zeros_like(cnt_s)

    x = x_ref[0]
    mod = mod_ref[0]
    sh1, sc1, g1 = mod[0:1], mod[1:2], mod[2:3]
    m_s[...] = (_layer_norm(x) * (1.0 + sc1) + sh1).astype(BF16)

    row8 = lax.broadcasted_iota(jnp.int32, (SUBLANES, gw), 0)
    n_xblk = d_inner // gw
    n_bblk = N_GROUPS * D_STATE // gw
    for blk in range(n_xblk + 2 * n_bblk):
        c0 = blk * gw
        pre = _dot(m_s[...], w_ref[:, off_xbc + c0:off_xbc + c0 + gw])
        prev_tail = tail_s[:, c0:c0 + gw]
        tail_s[:, c0:c0 + gw] = pre[tile - SUBLANES:tile]
        cw = convw_ref[:, c0:c0 + gw]
        acc = pre * cw[CONV_K - 1:CONV_K] + convb_ref[:, c0:c0 + gw]
        for k in range(1, CONV_K):
            r = pltpu.roll(pre, k, axis=0)
            top = jnp.where(row8 < k, pltpu.roll(prev_tail, k, axis=0), r[0:SUBLANES])
            r = jnp.concatenate([top, r[SUBLANES:]], axis=0)
            acc = acc + r * cw[CONV_K - 1 - k:CONV_K - k]
        act = _silu(acc)
        if blk < n_xblk:
            xs_s[blk] = act
        else:
            dst = bm_s if blk < n_xblk + n_bblk else cm_s
            g0 = ((blk - n_xblk) % n_bblk) * (gw // D_STATE)
            for i in range(gw // D_STATE):
                dst[g0 + i] = act[:, i * D_STATE:(i + 1) * D_STATE].astype(BF16)

    dt_s[...] = _softplus(_dot(m_s[...], w_ref[:, off_dt:off_dt + LANES]) + dtb_ref[...])
    a_neg = -jnp.exp(alog_ref[...])

    rowc = lax.broadcasted_iota(jnp.int32, (CHUNK, LANES), 0)
    causal = (lax.broadcasted_iota(jnp.int32, (CHUNK, CHUNK), 0)
              >= lax.broadcasted_iota(jnp.int32, (CHUNK, CHUNK), 1))
    lane_g = lax.broadcasted_iota(jnp.int32, (CHUNK, gw), 1)
    lane_c = lax.broadcasted_iota(jnp.int32, (CHUNK, CHUNK), 1)

    def chunk_body(c, carry):
        r0 = c * CHUNK
        rows = pl.ds(r0, CHUNK)
        dt_c = dt_s[rows, :]
        acs = dt_c * a_neg
        sh = 1
        while sh < CHUNK:
            acs = acs + jnp.where(rowc >= sh, pltpu.roll(acs, sh, axis=0), 0.0)
            sh *= 2
        acs_x128 = jnp.concatenate(
            [jnp.broadcast_to(acs[:, h:h + 1], (CHUNK, CHUNK)) for h in range(N_HEADS)], axis=1)
        acs_x64 = jnp.concatenate(
            [jnp.where(lane_c < HEAD_DIM, acs_x128[:, (2 * i) * CHUNK:(2 * i + 1) * CHUNK],
                       acs_x128[:, (2 * i + 1) * CHUNK:(2 * i + 2) * CHUNK])
             for i in range(N_HEADS // 2)], axis=1)
        dt_x64 = _dot(_split3_bf16(dt_c[:, :N_HEADS]), e64_ref[...])
        acs_t = acs.T
        last = acs_x64[CHUNK - 1:CHUNK, :]
        grow = jnp.exp(acs_x64)
        to_end = jnp.exp(last - acs_x64)
        chunk_decay = jnp.exp(last)

        for g in range(N_GROUPS):
            cols = slice(g * gw, (g + 1) * gw)
            xs_g = xs_s[g, rows, :]
            bm_g = bm_s[g, rows, :]
            cm_g = cm_s[g, rows, :]
            xdt = xs_g * dt_x64[:, cols]
            cb = _dot_nt(cm_g, bm_g)
            ws, xr = [], []
            for r in range(HEADS_PER_GROUP):
                h = g * HEADS_PER_GROUP + r
                seg = acs_x128[:, h * CHUNK:(h + 1) * CHUNK] - acs_t[h:h + 1, :]
                ws.append((cb * jnp.exp(jnp.where(causal, seg, -jnp.inf))).astype(BF16))
                in_head = (lane_g >= r * HEAD_DIM) & (lane_g < (r + 1) * HEAD_DIM)
                xr.append(jnp.where(in_head, xdt, 0.0).astype(BF16))
            y_g = None
            for r in range(0, HEADS_PER_GROUP, 2):
                part = _dot(jnp.concatenate(ws[r:r + 2], axis=1), jnp.concatenate(xr[r:r + 2], axis=0))
                y_g = part if y_g is None else y_g + part
            st = state_s[g]
            y_g = y_g + _dot(cm_g, st.astype(BF16)) * grow[:, cols]
            state_s[g] = chunk_decay[:, cols] * st + _dot_tn(bm_g, (xdt * to_end[:, cols]).astype(BF16))
            y_s[rows, cols] = y_g + dskip_ref[:, cols] * xs_g
        return carry

    for c in range(n_chunks):
        chunk_body(c, 0)

    z = _dot(m_s[...], w_ref[:, off_z:off_z + d_inner])
    y = y_s[...] * _silu(z)
    y = y * lax.rsqrt(jnp.mean(y * y, -1, keepdims=True) + RMS_EPS) * normw_ref[...]
    y_a = _dot(y.astype(BF16), wpssd_ref[...])
    merged = jax.nn.sigmoid(_dot(m_s[...], w_ref[:, off_ga:off_ga + d_model])) * y_a

    uv = jax.nn.gelu(_dot(m_s[...], w_ref[:, off_uv:off_uv + 2 * d_model]))
    u = uv[:, :d_model]
    v = _layer_norm(uv[:, d_model:]) * glng_ref[...] + glnb_ref[...]
    v = v.astype(BF16)
    gd = d_model // GMLP_GROUPS
    ws_m = [jnp.where(causal, ws_ref[g], 0.0).astype(BF16) for g in range(GMLP_GROUPS)]
    v_rows = []
    for c in range(n_chunks):
        v_c = v[c * CHUNK:(c + 1) * CHUNK]
        v_rows.append(jnp.concatenate(
            [_dot(ws_m[g], v_c[:, g * gd:(g + 1) * gd]) for g in range(GMLP_GROUPS)], axis=1)
            + bsx_ref[...])
    v_mix = jnp.concatenate(v_rows, axis=0) if n_chunks > 1 else v_rows[0]
    y_b = _dot((u * v_mix).astype(BF16), wpgmlp_ref[...])
    merged = merged + jax.nn.sigmoid(_dot(m_s[...], w_ref[:, off_gb:off_gb + d_model])) * y_b

    mix = _dot(merged.astype(BF16), wout_ref[...])
    h1 = _layer_norm(DEEPNORM_ALPHA * x + g1 * mix) * ln1g_ref[...] + ln1b_ref[...]
    o_ref[0] = h1

    _route_tile(h1, mod2_ref[0], wr_ref, rbias_ref, topi_ref, topw_ref, rank_ref, cnt_ref, cnt_s, tile)


def _const_spec(shape):
    nd = len(shape)
    return pl.BlockSpec(shape, lambda b, j: (0,) * nd, pipeline_mode=pl.Buffered(1))


def _mixer(x, mod1, mod2, w_in, conv_w, conv_b, dt_bias, a_log, d_skip, ssd_norm_w,
           gmlp_ln_g, gmlp_ln_b, gmlp_ws, gmlp_bs, w_proj_ssd, w_proj_gmlp, w_out, ln1_g, ln1_b,
           w_router, router_bias):
    bsz, seq, d = x.shape
    d_inner = N_HEADS * HEAD_DIM
    gn = N_GROUPS * D_STATE
    d_xbc = d_inner + 2 * gn
    tile = min(MIX_TILE, seq)
    assert seq % tile == 0 and tile % CHUNK == 0 and d_inner == 2 * d

    s0, s1, s2, s3, s4 = (d_inner, d_inner + d_xbc, d_inner + d_xbc + N_HEADS,
                          d_inner + d_xbc + N_HEADS + 2 * d, d_inner + d_xbc + N_HEADS + 3 * d)
    w_b = w_in.astype(BF16)
    w_dt = jnp.pad(w_b[:, s1:s2], ((0, 0), (0, LANES - N_HEADS)))
    w_all = jnp.concatenate([w_b[:, :s1], w_b[:, s2:], w_dt], axis=1)
    off_z, off_xbc = 0, d_inner
    off_uv = s1
    off_ga = off_uv + 2 * d
    off_gb = off_ga + d
    off_dt = off_gb + d
    offs = (off_z, off_xbc, off_uv, off_ga, off_gb, off_dt)

    pad_h = lambda v: jnp.pad(v.reshape(1, N_HEADS), ((0, 0), (0, LANES - N_HEADS)))
    head_of_col64 = jnp.arange(d_inner) // HEAD_DIM
    piece_head = jnp.where(jnp.arange(LANES) < 3 * N_HEADS, jnp.arange(LANES) % N_HEADS, -1)
    e64 = (piece_head[:, None] == head_of_col64[None, :]).astype(BF16)
    dskip_x = jnp.broadcast_to(d_skip[:, None], (N_HEADS, HEAD_DIM)).reshape(1, d_inner)
    bs_x = jnp.broadcast_to(gmlp_bs.T[:, :, None], (CHUNK, GMLP_GROUPS, d // GMLP_GROUPS)).reshape(CHUNK, d)

    operands = [
        w_all, conv_w, conv_b.reshape(1, d_xbc), pad_h(dt_bias), pad_h(a_log), dskip_x,
        ssd_norm_w.reshape(1, d_inner), gmlp_ln_g.reshape(1, d), gmlp_ln_b.reshape(1, d),
        gmlp_ws, bs_x, e64,
        w_proj_ssd.astype(BF16), w_proj_gmlp.astype(BF16), w_out.astype(BF16),
        ln1_g.reshape(1, d), ln1_b.reshape(1, d),
        w_router.T.astype(BF16),
        jnp.broadcast_to(router_bias.reshape(N_EXPERTS, 1), (N_EXPERTS, tile)).astype(F32),
    ]
    gw = HEADS_PER_GROUP * HEAD_DIM
    n_j = seq // tile
    t = bsz * seq
    kern = functools.partial(_mixer_kernel, tile=tile, d_model=d, d_inner=d_inner, offs=offs)
    row_spec = pl.BlockSpec((TOP_K, tile), lambda b, j: (0, b * n_j + j))
    return pl.pallas_call(
        kern,
        grid=(bsz, n_j),
        in_specs=[pl.BlockSpec((1, tile, d), lambda b, j: (b, j, 0)),
                  pl.BlockSpec((1, SUBLANES, d), lambda b, j: (b, 0, 0)),
                  pl.BlockSpec((1, SUBLANES, d), lambda b, j: (b, 0, 0))]
                 + [_const_spec(op.shape) for op in operands],
        out_specs=[pl.BlockSpec((1, tile, d), lambda b, j: (b, j, 0)),
                   row_spec, row_spec, row_spec,
                   pl.BlockSpec((N_EXPERTS, LANES), lambda b, j: (0, 0))],
        out_shape=[jax.ShapeDtypeStruct((bsz, seq, d), F32),
                   jax.ShapeDtypeStruct((TOP_K, t), jnp.int32),
                   jax.ShapeDtypeStruct((TOP_K, t), F32),
                   jax.ShapeDtypeStruct((TOP_K, t), jnp.int32),
                   jax.ShapeDtypeStruct((N_EXPERTS, LANES), F32)],
        scratch_shapes=[
            pltpu.VMEM((tile, d), BF16),
            pltpu.VMEM((N_GROUPS, tile, gw), F32),
            pltpu.VMEM((N_GROUPS, tile, D_STATE), BF16),
            pltpu.VMEM((N_GROUPS, tile, D_STATE), BF16),
            pltpu.VMEM((tile, LANES), F32),
            pltpu.VMEM((tile, d_inner), F32),
            pltpu.VMEM((N_GROUPS, D_STATE, gw), F32),
            pltpu.VMEM((SUBLANES, d_xbc), F32),
            pltpu.VMEM((N_EXPERTS, LANES), F32),
        ],
        compiler_params=pltpu.CompilerParams(
            dimension_semantics=("arbitrary", "arbitrary"),
            vmem_limit_bytes=MIXER_VMEM_LIMIT),
        name="mixer",
    )(x, mod1, mod2, *operands)


def _route_tile(h, mod, wr_ref, bias_ref, topi_ref, topw_ref, rank_ref, cnt_ref, cnt_s, tile):
    m2 = _layer_norm(h) * (1.0 + mod[1:2]) + mod[0:1]
    scores = jax.nn.sigmoid(_dot_nt(wr_ref[...], m2.astype(BF16)))
    choice = scores + bias_ref[...]

    neg = -jnp.inf
    c3 = choice.reshape(N_EXPERT_GROUPS, EXPERTS_PER_GROUP, tile)
    i3 = lax.broadcasted_iota(jnp.int32, c3.shape, 1).astype(F32)
    m1 = jnp.max(c3, axis=1, keepdims=True)
    i1 = jnp.min(jnp.where(c3 == m1, i3, float(EXPERTS_PER_GROUP)), axis=1, keepdims=True)
    second = jnp.max(jnp.where(i3 == i1, neg, c3), axis=1)
    gs = m1[:, 0, :] + second

    gi = lax.broadcasted_iota(jnp.int32, gs.shape, 0).astype(F32)
    gsel = jnp.zeros(gs.shape, F32)
    for _ in range(TOPK_GROUPS):
        mx = jnp.max(gs, axis=0, keepdims=True)
        ix = jnp.min(jnp.where(gs == mx, gi, float(N_EXPERT_GROUPS)), axis=0, keepdims=True)
        hit = gi == ix
        gsel = jnp.where(hit, 1.0, gsel)
        gs = jnp.where(hit, neg, gs)
    emask = jnp.broadcast_to(gsel[:, None, :], c3.shape).reshape(N_EXPERTS, tile) > 0.0
    masked = jnp.where(emask, choice, neg)

    ei = lax.broadcasted_iota(jnp.int32, masked.shape, 0).astype(F32)
    hits, idx_rows, w_rows = [], [], []
    for _ in range(TOP_K):
        mx = jnp.max(masked, axis=0, keepdims=True)
        ix = jnp.min(jnp.where(masked == mx, ei, float(N_EXPERTS)), axis=0, keepdims=True)
        hit = ei == ix
        hits.append(hit)
        idx_rows.append(ix)
        w_rows.append(jnp.sum(jnp.where(hit, scores, 0.0), axis=0, keepdims=True))
        masked = jnp.where(hit, neg, masked)
    w_all = jnp.concatenate(w_rows, axis=0)
    topw_ref[...] = w_all / (jnp.sum(w_all, axis=0, keepdims=True) + 1e-20) * ROUTED_SCALE
    topi_ref[...] = jnp.concatenate(idx_rows, axis=0).astype(jnp.int32)

    assign = jnp.zeros(masked.shape, F32)
    for hit in hits:
        assign = jnp.where(hit, 1.0, assign)
    assign_b = assign.astype(BF16)
    t0 = lax.broadcasted_iota(jnp.int32, (tile, tile), 0)
    t1 = lax.broadcasted_iota(jnp.int32, (tile, tile), 1)
    before = jnp.where(t0 < t1, 1.0, 0.0).astype(BF16)
    base = cnt_s[...]
    pos = _dot(assign_b, before) + jnp.concatenate([base] * (tile // LANES), axis=1)
    rank_ref[...] = jnp.concatenate(
        [jnp.sum(jnp.where(hit, pos, 0.0), axis=0, keepdims=True) for hit in hits],
        axis=0).astype(jnp.int32)
    cnt_s[...] = base + _dot(assign_b, jnp.ones((tile, LANES), BF16))
    cnt_ref[...] = cnt_s[...]


def _slots_kernel(topi_ref, rank_ref, start_ref, dest_ref, *, tile):
    ei = lax.broadcasted_iota(jnp.int32, (N_EXPERTS, tile), 0)
    start = jnp.concatenate([start_ref[...]] * (tile // LANES), axis=1)
    topi = topi_ref[...]
    rows = [jnp.sum(jnp.where(ei == topi[k:k + 1, :], start, 0.0), axis=0, keepdims=True)
            for k in range(TOP_K)]
    dest_ref[...] = jnp.concatenate(rows, axis=0).astype(jnp.int32) + rank_ref[...]


def _slots(topi_t, rank_t, pad_start):
    t = topi_t.shape[1]
    tile = min(SLOTS_TILE, t)
    start_x = jnp.broadcast_to(pad_start.astype(F32).reshape(N_EXPERTS, 1), (N_EXPERTS, LANES))
    row_spec = pl.BlockSpec((TOP_K, tile), lambda i: (0, i))
    return pl.pallas_call(
        functools.partial(_slots_kernel, tile=tile),
        grid=(t // tile,),
        in_specs=[row_spec, row_spec, pl.BlockSpec((N_EXPERTS, LANES), lambda i: (0, 0))],
        out_specs=row_spec,
        out_shape=jax.ShapeDtypeStruct((TOP_K, t), jnp.int32),
        name="slots",
    )(topi_t, rank_t, start_x)


def _to_slabs(slab_ref, x):
    rows, width = x.shape
    n = width // LANES
    for j in range(n):
        slab_ref[pl.ds(j, rows, stride=n), :] = x[:, j * LANES:(j + 1) * LANES]


def _from_slabs(slab_ref, rows, n):
    return jnp.concatenate([slab_ref[pl.ds(j, rows, stride=n), :] for j in range(n)], axis=1)


def _slab(ref, r, n):
    return ref.at[pl.ds(pl.multiple_of(r * n, n), n), :]


def _pack_bf16_pairs(x):
    n = x.shape[1] // 2
    bits = lax.bitcast_convert_type(x.astype(BF16).astype(F32), jnp.uint32)
    return bits[:, n:] | (bits[:, :n] >> 16)


def _unpack_pairs_f32(p):
    lo = lax.bitcast_convert_type(p << 16, F32)
    hi = lax.bitcast_convert_type(p & jnp.uint32(0xFFFF0000), F32)
    return jnp.concatenate([lo, hi], axis=1)


def _unpack_bf16_pairs(p):
    return _unpack_pairs_f32(p).astype(BF16)


def _scatter_kernel(zflag_ref, dest_ref, h_ref, mod_ref, out_ref, slab_s, zero_s, sem, zsem,
                    *, tile, n, n_steps, n_blocks):
    i = pl.program_id(0)
    slot = i % 2

    @pl.when(i == 0)
    def _():
        zero_s[...] = jnp.zeros_like(zero_s)
        brows = EXPERT_BLOCK * n

        def zero_copy(b):
            return pltpu.make_async_copy(
                zero_s, out_ref.at[pl.ds(pl.multiple_of(b * brows, brows), brows), :], zsem)

        def start(b, carry):
            @pl.when(zflag_ref[b] > 0)
            def _():
                zero_copy(b).start()
            return carry

        def wait(b, carry):
            @pl.when(zflag_ref[b] > 0)
            def _():
                zero_copy(b).wait()
            return carry

        lax.fori_loop(0, n_blocks, start, 0)
        lax.fori_loop(0, n_blocks, wait, 0)

    mod = mod_ref[0]
    _to_slabs(slab_s.at[slot],
              _pack_bf16_pairs(_layer_norm(h_ref[...]) * (1.0 + mod[1:2]) + mod[0:1]))

    def issue(t, carry):
        src = _slab(slab_s.at[slot], t, n)
        for k in range(TOP_K):
            d = dest_ref[t * TOP_K + k]
            pltpu.make_async_copy(src, _slab(out_ref, d, n), sem.at[slot]).start(priority=k % 2)
        return carry

    lax.fori_loop(0, tile, issue, 0)

    def drain(s):
        for _ in range(TOP_K):
            pltpu.make_async_copy(slab_s.at[s], out_ref.at[pl.ds(0, tile * n), :], sem.at[s]).wait()

    @pl.when(i >= 1)
    def _():
        drain(1 - slot)

    @pl.when(i == n_steps - 1)
    def _():
        drain(slot)


def _scatter(h1, mod2, dest_flat, zero_flag, tokens_per_batch):
    t, d = h1.shape
    n = d // 2 // LANES
    n_blocks = zero_flag.shape[0]
    tile = min(SCATTER_TILE, tokens_per_batch)
    per_b = tokens_per_batch // tile
    kern = functools.partial(_scatter_kernel, tile=tile, n=n, n_steps=t // tile, n_blocks=n_blocks)
    grid_spec = pltpu.PrefetchScalarGridSpec(
        num_scalar_prefetch=1,
        grid=(t // tile,),
        in_specs=[pl.BlockSpec((tile * TOP_K,), lambda i, zf: (i,), memory_space=pltpu.SMEM),
                  pl.BlockSpec((tile, d), lambda i, zf: (i, 0)),
                  pl.BlockSpec((1, SUBLANES, d), lambda i, zf: (i // per_b, 0, 0))],
        out_specs=pl.BlockSpec(memory_space=pl.ANY),
        scratch_shapes=[pltpu.VMEM((2, tile * n, LANES), jnp.uint32),
                        pltpu.VMEM((EXPERT_BLOCK * n, LANES), jnp.uint32),
                        pltpu.SemaphoreType.DMA((2,)), pltpu.SemaphoreType.DMA(())],
    )
    return pl.pallas_call(
        kern,
        grid_spec=grid_spec,
        out_shape=jax.ShapeDtypeStruct((n_blocks * EXPERT_BLOCK * n, LANES), jnp.uint32),
        compiler_params=pltpu.CompilerParams(dimension_semantics=("arbitrary",),
                                             has_side_effects=True),
        name="scatter",
    )(zero_flag, dest_flat, h1, mod2)


def _expert_kernel(blk0_ref, nblk_ref, x_ref, wg_ref, wu_ref, wd_ref, y_ref,
                   xbuf_s, ybuf_s, wg_s, wu_s, wd_s, in_sem, out_sem, *, nx, ny):
    e = pl.program_id(0)
    blk0 = blk0_ref[e]
    nblk = nblk_ref[e]
    used = blk0_ref[N_EXPERTS]
    rows = EXPERT_BLOCK
    ahead = EXPERT_IN_BUFS - 1

    def in_copy(gb):
        slot = gb % EXPERT_IN_BUFS
        return pltpu.make_async_copy(x_ref.at[pl.ds(pl.multiple_of(gb * rows * nx, rows * nx), rows * nx), :],
                                     xbuf_s.at[slot], in_sem.at[slot])

    def out_copy(gb):
        slot = gb % EXPERT_OUT_BUFS
        return pltpu.make_async_copy(ybuf_s.at[slot],
                                     y_ref.at[pl.ds(pl.multiple_of(gb * rows * ny, rows * ny), rows * ny), :],
                                     out_sem.at[slot])

    @pl.when(e == 0)
    def _():
        for b in range(ahead):
            @pl.when(b < used)
            def _():
                in_copy(b).start()

    @pl.when((e < N_EXPERTS) & (nblk > 0))
    def _():
        wg_s[...] = wg_ref[0].astype(BF16)
        wu_s[...] = wu_ref[0].astype(BF16)
        wd_s[...] = wd_ref[0].astype(BF16)

        def block(c, carry):
            gb = blk0 + c
            in_copy(gb).wait()

            @pl.when(gb + ahead < used)
            def _():
                in_copy(gb + ahead).start()

            @pl.when(gb >= EXPERT_OUT_BUFS)
            def _():
                out_copy(gb - EXPERT_OUT_BUFS).wait()

            xb = _unpack_bf16_pairs(_from_slabs(xbuf_s.at[gb % EXPERT_IN_BUFS], rows, nx))
            hid = (_silu(_dot(xb, wg_s[...])) * _dot(xb, wu_s[...])).astype(BF16)
            _to_slabs(ybuf_s.at[gb % EXPERT_OUT_BUFS], _pack_bf16_pairs(_dot(hid, wd_s[...])))
            out_copy(gb).start()
            return carry

        lax.fori_loop(0, nblk, block, 0)

    @pl.when(e == N_EXPERTS)
    def _():
        for b in range(EXPERT_OUT_BUFS, 0, -1):
            @pl.when(used >= b)
            def _():
                out_copy(used - b).wait()

        @pl.when(nblk > 0)
        def _():
            ybuf_s[0] = jnp.zeros(ybuf_s.shape[1:], jnp.uint32)

            def zero_block(c, carry):
                gb = blk0 + c
                cp = pltpu.make_async_copy(
                    ybuf_s.at[0],
                    y_ref.at[pl.ds(pl.multiple_of(gb * rows * ny, rows * ny), rows * ny), :],
                    out_sem.at[0])
                cp.start()
                cp.wait()
                return carry

            lax.fori_loop(0, nblk, zero_block, 0)


def _experts(xs_slabs, blk0, nblk, w_e_gate, w_e_up, w_e_down):
    d, ff = w_e_gate.shape[-2:]
    nx = d // 2 // LANES
    ny = d // 2 // LANES
    n_slots = xs_slabs.shape[0] // nx
    rows = EXPERT_BLOCK
    w_idx = lambda e, b0, nb: (jnp.minimum(e, N_EXPERTS - 1), 0, 0)
    grid_spec = pltpu.PrefetchScalarGridSpec(
        num_scalar_prefetch=2,
        grid=(N_EXPERTS + 1,),
        in_specs=[pl.BlockSpec(memory_space=pl.ANY),
                  pl.BlockSpec((1, d, ff), w_idx),
                  pl.BlockSpec((1, d, ff), w_idx),
                  pl.BlockSpec((1, ff, d), w_idx)],
        out_specs=pl.BlockSpec(memory_space=pl.ANY),
        scratch_shapes=[pltpu.VMEM((EXPERT_IN_BUFS, rows * nx, LANES), jnp.uint32),
                        pltpu.VMEM((EXPERT_OUT_BUFS, rows * ny, LANES), jnp.uint32),
                        pltpu.VMEM((d, ff), BF16), pltpu.VMEM((d, ff), BF16),
                        pltpu.VMEM((ff, d), BF16),
                        pltpu.SemaphoreType.DMA((EXPERT_IN_BUFS,)),
                        pltpu.SemaphoreType.DMA((EXPERT_OUT_BUFS,))],
    )
    return pl.pallas_call(
        functools.partial(_expert_kernel, nx=nx, ny=ny),
        grid_spec=grid_spec,
        out_shape=jax.ShapeDtypeStruct((n_slots * ny, LANES), jnp.uint32),
        compiler_params=pltpu.CompilerParams(dimension_semantics=("arbitrary",),
                                             has_side_effects=True),
        name="experts",
    )(blk0, nblk, xs_slabs, w_e_gate, w_e_up, w_e_down)


def _combine_kernel(dest_ref, dest_next_ref, h_ref, mod_ref, w_ref, y_ref, wsg_ref, wsu_ref, wsd_ref,
                    ln2g_ref, ln2b_ref, o_ref, ybuf_s, sem, *, tile, n, n_steps):
    i = pl.program_id(0)
    slot = i % 2

    def issue_tile(dref, s):
        def issue(t, carry):
            for k in range(TOP_K):
                d = dref[t * TOP_K + k]
                pltpu.make_async_copy(_slab(y_ref, d, n), _slab(ybuf_s.at[s, k], t, n),
                                      sem.at[s]).start(priority=k % 2)
            return carry
        lax.fori_loop(0, tile, issue, 0)

    @pl.when(i == 0)
    def _():
        issue_tile(dest_ref, 0)

    @pl.when(i + 1 < n_steps)
    def _():
        issue_tile(dest_next_ref, 1 - slot)

    h = h_ref[...]
    mod = mod_ref[0]
    m2 = (_layer_norm(h) * (1.0 + mod[1:2]) + mod[0:1]).astype(BF16)
    acc = _dot((_silu(_dot(m2, wsg_ref[...])) * _dot(m2, wsu_ref[...])).astype(BF16), wsd_ref[...])

    for k in range(TOP_K):
        pltpu.make_async_copy(y_ref.at[pl.ds(0, tile * n), :], ybuf_s.at[slot, k],
                              sem.at[slot]).wait()
    w = w_ref[...]
    for k in range(TOP_K):
        acc = acc + _unpack_pairs_f32(_from_slabs(ybuf_s.at[slot, k], tile, n)) * w[:, k:k + 1]
    o_ref[...] = _layer_norm(DEEPNORM_ALPHA * h + mod[2:3] * acc) * ln2g_ref[...] + ln2b_ref[...]


def _combine(h1, mod2, dest_flat, topw, y_slabs, w_sh_gate, w_sh_up, w_sh_down, ln2_g, ln2_b,
             tokens_per_batch):
    t, d = h1.shape
    n = d // 2 // LANES
    ff = w_sh_gate.shape[-1]
    tile = min(COMBINE_TILE, tokens_per_batch)
    per_b = tokens_per_batch // tile
    n_steps = t // tile
    kern = functools.partial(_combine_kernel, tile=tile, n=n, n_steps=n_steps)
    const = lambda shape: pl.BlockSpec(shape, lambda i: (0,) * len(shape))
    return pl.pallas_call(
        kern,
        grid=(n_steps,),
        in_specs=[pl.BlockSpec((tile * TOP_K,), lambda i: (i,), memory_space=pltpu.SMEM),
                  pl.BlockSpec((tile * TOP_K,), lambda i: (jnp.minimum(i + 1, n_steps - 1),),
                               memory_space=pltpu.SMEM),
                  pl.BlockSpec((tile, d), lambda i: (i, 0)),
                  pl.BlockSpec((1, SUBLANES, d), lambda i: (i // per_b, 0, 0)),
                  pl.BlockSpec((tile, TOP_K), lambda i: (i, 0)),
                  pl.BlockSpec(memory_space=pl.ANY),
                  const((d, ff)), const((d, ff)), const((ff, d)), const((1, d)), const((1, d))],
        out_specs=pl.BlockSpec((tile, d), lambda i: (i, 0)),
        out_shape=jax.ShapeDtypeStruct((t, d), F32),
        scratch_shapes=[pltpu.VMEM((2, TOP_K, tile * n, LANES), jnp.uint32),
                        pltpu.SemaphoreType.DMA((2,))],
        compiler_params=pltpu.CompilerParams(dimension_semantics=("arbitrary",)),
        name="combine",
    )(dest_flat, dest_flat, h1, mod2, topw, y_slabs, w_sh_gate.astype(BF16), w_sh_up.astype(BF16),
      w_sh_down.astype(BF16), ln2_g.reshape(1, d), ln2_b.reshape(1, d))


def _moe(h1, routing, mod2, w_e_gate, w_e_up, w_e_down,
         w_sh_gate, w_sh_up, w_sh_down, ln2_g, ln2_b, tokens_per_batch):
    t, d = h1.shape
    topi_t, topw_t, rank_t, cnt = routing

    blk = EXPERT_BLOCK
    nb = -(-(t * TOP_K) // blk) + N_EXPERTS
    counts = cnt[:, 0].astype(jnp.int32)
    padded = (counts + blk - 1) // blk * blk
    pad_end = jnp.cumsum(padded)
    pad_start = pad_end - padded
    dest_flat = _slots(topi_t, rank_t, pad_start).T.reshape(t * TOP_K)
    used = pad_end[-1] // blk
    blk0 = jnp.concatenate([pad_start // blk, used[None]]).astype(jnp.int32)
    nblk = jnp.concatenate([padded // blk, (nb - used)[None]]).astype(jnp.int32)

    block_id = jnp.arange(nb, dtype=jnp.int32)
    is_last = jnp.any((block_id[:, None] == (pad_end // blk - 1)[None, :]) & (padded > 0)[None, :], axis=1)
    zero_flag = (is_last | (block_id >= used)).astype(jnp.int32)

    xs_slabs = _scatter(h1, mod2, dest_flat, zero_flag, tokens_per_batch)
    y_slabs = _experts(xs_slabs, blk0, nblk, w_e_gate, w_e_up, w_e_down)
    return _combine(h1, mod2, dest_flat, topw_t.T, y_slabs, w_sh_gate, w_sh_up, w_sh_down,
                    ln2_g, ln2_b, tokens_per_batch)


def _pad_rows(parts, d):
    rows = jnp.stack(parts, axis=1)
    return jnp.pad(rows, ((0, 0), (0, SUBLANES - rows.shape[1]), (0, 0)))


def kernel(x, c, w_ada, b_ada, w_in, conv_w, conv_b, dt_bias, a_log, d_skip, ssd_norm_w, gmlp_ln_g, gmlp_ln_b, gmlp_ws, gmlp_bs, w_proj_ssd, w_proj_gmlp, w_out, ln1_g, ln1_b, w_router, router_bias, w_e_gate, w_e_up, w_e_down, w_sh_gate, w_sh_up, w_sh_down, ln2_g, ln2_b):
    bsz, seq, d = x.shape
    depth = w_ada.shape[0]
    at = lambda w, i: w.reshape(w.shape[1:]) if depth == 1 else w[i]
    h = x
    for i in range(depth):
        ada = _ada(c, at(w_ada, i), at(b_ada, i))
        sh1, sc1, g1, sh2, sc2, g2 = jnp.split(ada, 6, axis=-1)
        mod2 = _pad_rows([sh2, sc2, g2], d)
        h, *routing = _mixer(
            h, _pad_rows([sh1, sc1, g1], d), mod2,
            *[at(w, i) for w in (w_in, conv_w, conv_b, dt_bias, a_log, d_skip, ssd_norm_w, gmlp_ln_g,
                                 gmlp_ln_b, gmlp_ws, gmlp_bs, w_proj_ssd, w_proj_gmlp, w_out, ln1_g,
                                 ln1_b, w_router, router_bias)])
        h = _moe(h.reshape(bsz * seq, d), routing, mod2,
                 *[at(w, i) for w in (w_e_gate, w_e_up, w_e_down, w_sh_gate, w_sh_up, w_sh_down,
                                      ln2_g, ln2_b)], seq).reshape(bsz, seq, d)
    return h
```
